```python
import math
import jax, jax.numpy as jnp
from jax import lax
import numpy as np

D_MODEL = 1024
BATCH = 16
SEQ = 256
DEPTH = 2
DEC_BATCH = 8
DEC_SEQ = 4096
PAST_LEN = 512

GRID_W = 64
D_MIX = D_MODEL
C_CONV = D_MIX // 4
CONV_WIDTH = 31
H_GLA = 4
DK_GLA = 64
DV_GLA = 64
W_GLA = H_GLA * DV_GLA
GLA_RANK = 16
GLA_TAU = 16.0
GLA_CHUNK = 64
H_DIFF = 4
DH_DIFF = 64
DV_DIFF = 2 * DH_DIFF
W_DIFF = H_DIFF * DV_DIFF
Q_BLOCK = 128
ROPE_BASE = 10000.0
IN_CONV = 2 * C_CONV
IN_GLA = 4 * W_GLA + 2 * GLA_RANK
IN_DIFF = 3 * W_DIFF
IN_COLS = IN_CONV + IN_GLA + IN_DIFF
D_FF = 2816
ALPHA = (2 * DEPTH) ** 0.25
BETA = (8 * DEPTH) ** -0.25
LN_EPS = 1e-5
N_MOD = 9

kernel_name = 'hybrid_dit_conv_gla_diffattn_step'


def layer_norm(x, g, b):
    xf = x.astype(jnp.float32)
    mu = jnp.mean(xf, axis=-1, keepdims=True)
    var = jnp.mean(jnp.square(xf - mu), axis=-1, keepdims=True)
    y = (xf - mu) * lax.rsqrt(var + LN_EPS)
    return (y * g.astype(jnp.float32) + b.astype(jnp.float32)).astype(x.dtype)


def rms_norm(x, g):
    xf = x.astype(jnp.float32)
    y = xf * lax.rsqrt(jnp.mean(jnp.square(xf), axis=-1, keepdims=True) + LN_EPS)
    return (y * g.astype(jnp.float32)).astype(x.dtype)


def swiglu_ffn(h, w_in, w_out):
    a, b = jnp.split(h @ w_in, 2, axis=-1)
    return (jax.nn.silu(a) * b) @ w_out


def conv_module(u, w, bias, g, b):
    a, gt = jnp.split(u, 2, axis=-1)
    y = a * jax.nn.sigmoid(gt)
    pad = CONV_WIDTH // 2
    y = lax.conv_general_dilated(y, w[:, None, :].astype(y.dtype), window_strides=(1,),
                                 padding=[(pad, pad)], dimension_numbers=('NWC', 'WIO', 'NWC'),
                                 feature_group_count=C_CONV) + bias
    return jax.nn.silu(layer_norm(y, g, b))


def gla_features(z, w_a2, b_a):
    B, T, _ = z.shape
    q, k, v, gate, lr_f, lr_b = jnp.split(
        z, [W_GLA, 2 * W_GLA, 3 * W_GLA, 4 * W_GLA, 4 * W_GLA + GLA_RANK], axis=-1)
    heads = lambda t: t.reshape(B, T, H_GLA, -1).transpose(0, 2, 1, 3)
    logdec = lambda lr, d: heads(jax.nn.log_sigmoid((lr @ w_a2[d] + b_a[d]).astype(jnp.float32)) / GLA_TAU)
    return heads(q) * DK_GLA ** -0.5, heads(k), heads(v), gate, logdec(lr_f, 0), logdec(lr_b, 1)


def gla_chunked(q, k, v, g, s0):
    B, H, T, DK = q.shape
    DV = v.shape[-1]
    N = T // GLA_CHUNK
    f32 = jnp.float32
    qc = q.astype(f32).reshape(B, H, N, GLA_CHUNK, DK)
    kc = k.astype(f32).reshape(B, H, N, GLA_CHUNK, DK)
    vc = v.astype(f32).reshape(B, H, N, GLA_CHUNK, DV)
    G = jnp.cumsum(g.astype(f32).reshape(B, H, N, GLA_CHUNK, DK), axis=3)
    G_last = G[:, :, :, -1:, :]
    q_t = qc * jnp.exp(G)
    k_t = kc * jnp.exp(-G)
    k_hat = kc * jnp.exp(G_last - G)
    lower = jnp.tril(jnp.ones((GLA_CHUNK, GLA_CHUNK), dtype=bool))
    A = jnp.where(lower, jnp.einsum('bhncd,bhnsd->bhncs', q_t, k_t), 0.0)
    o_intra = jnp.einsum('bhncs,bhnsv->bhncv', A, vc)
    kv = jnp.einsum('bhnsd,bhnsv->bhndv', k_hat, vc)
    decay = jnp.exp(G_last[:, :, :, 0, :])

    def step(S, inp):
        d, u = inp
        return d[..., None] * S + u, S

    s_final, s_in = lax.scan(step, s0.astype(f32), (jnp.moveaxis(decay, 2, 0), jnp.moveaxis(kv, 2, 0)))
    s_in = jnp.moveaxis(s_in, 0, 2)
    o_inter = jnp.einsum('bhncd,bhndv->bhncv', q_t, s_in)
    return (o_intra + o_inter).reshape(B, H, T, DV), s_final


def axial_rope_tables(rows):
    row = jnp.repeat(jnp.arange(rows, dtype=jnp.float32), GRID_W)
    col = jnp.tile(jnp.arange(GRID_W, dtype=jnp.float32), rows)
    seg = DH_DIFF // 2
    inv = ROPE_BASE ** (-jnp.arange(0, seg, 2, dtype=jnp.float32) / seg)
    a_r = row[:, None] * inv
    a_c = col[:, None] * inv
    ang = jnp.concatenate([a_r, a_r, a_c, a_c], axis=-1)
    return jnp.cos(ang), jnp.sin(ang)


def apply_rope(x, cos, sin):
    xf = x.astype(jnp.float32)
    xr = xf.reshape(x.shape[:-1] + (2, 2, DH_DIFF // 4))
    rot = jnp.stack([-xr[..., 1, :], xr[..., 0, :]], axis=-2).reshape(x.shape)
    return (xf * cos[:, None, :] + rot * sin[:, None, :]).astype(x.dtype)


def diff_lambda(lam_p, layer_idx):
    lam_init = 0.8 - 0.6 * math.exp(-0.3 * layer_idx)
    p = lam_p.astype(jnp.float32)
    lam = jnp.exp(jnp.sum(p[0] * p[1])) - jnp.exp(jnp.sum(p[2] * p[3])) + lam_init
    return lam, lam_init


def diff_attend(q, k, v, lam):
    B, H, Tq = q.shape[:3]
    nb = Tq // Q_BLOCK
    qb = q.reshape(B, H, nb, Q_BLOCK, 2, DH_DIFF).transpose(2, 0, 1, 3, 4, 5)
    scale = DH_DIFF ** -0.5

    def block(qi):
        s = jnp.einsum('bhqmd,bhkmd->bhmqk', qi, k).astype(jnp.float32) * scale
        p = jax.nn.softmax(s, axis=-1)
        w = p[:, :, 0] - lam * p[:, :, 1]
        return jnp.einsum('bhqk,bhkv->bhqv', w.astype(v.dtype), v)

    o = lax.map(block, qb)
    return o.transpose(1, 2, 0, 3, 4).reshape(B, H, Tq, DV_DIFF)


def token_mixer(h, l, P, ctx):
    B, T, _ = h.shape
    z = h @ P['w_in'][l]
    z_conv, z_gla, z_diff = jnp.split(z, [IN_CONV, IN_CONV + IN_GLA], axis=-1)
    y_conv = conv_module(z_conv, P['conv_w'][l], P['conv_b'][l], P['conv_ln_g'][l], P['conv_ln_b'][l])
    q, k, v, gate, g_f, g_b = gla_features(z_gla, P['gla_w_a2'][l], P['gla_b_a'][l])
    if ctx is None:
        s_f0 = jnp.zeros((B, H_GLA, DK_GLA, DV_GLA), jnp.float32)
        s_b0 = s_f0
    else:
        s_f0 = ctx['state'][:, 0]
        s_b0 = ctx['state'][:, 1]
    flip = lambda t: jnp.flip(t, axis=2)
    o_f, s_f = gla_chunked(q, k, v, g_f, s_f0)
    o_b, s_b = gla_chunked(flip(q), flip(k), flip(v), flip(g_b), s_b0)
    o = (o_f + flip(o_b)).astype(h.dtype)
    y_gla = rms_norm(o, P['gla_norm_g'][l]).transpose(0, 2, 1, 3).reshape(B, T, W_GLA) * jax.nn.silu(gate)
    qd, kd, vd = jnp.split(z_diff, 3, axis=-1)
    qd = qd.reshape(B, T, H_DIFF, 2, DH_DIFF).transpose(0, 2, 1, 3, 4)
    kd = kd.reshape(B, T, H_DIFF, 2, DH_DIFF).transpose(0, 2, 1, 3, 4)
    vd = vd.reshape(B, T, H_DIFF, DV_DIFF).transpose(0, 2, 1, 3)
    if ctx is None:
        keys, vals = kd, vd
    else:
        qd = apply_rope(qd, ctx['cos'], ctx['sin'])
        ck = ctx['k'].reshape(B, H_DIFF, -1, 2, DH_DIFF)
        keys = jnp.concatenate([apply_rope(kd, ctx['cos'], ctx['sin']), ck], axis=2)
        vals = jnp.concatenate([vd, ctx['v']], axis=2)
    lam, lam_init = diff_lambda(P['diff_lam'][l], l)
    od = diff_attend(qd, keys, vals, lam)
    y_diff = (rms_norm(od, P['diff_norm_g'][l]) * (1.0 - lam_init)).transpose(0, 2, 1, 3).reshape(B, T, W_DIFF)
    y = jnp.concatenate([y_conv, y_gla, y_diff.astype(h.dtype)], axis=-1) @ P['w_out'][l]
    if ctx is None:
        new_ctx = (kd.reshape(B, H_DIFF, T, 2 * DH_DIFF), vd,
                   jnp.stack([s_f, s_b], axis=1).astype(h.dtype))
    else:
        new_ctx = None
    return y, new_ctx


def trunk_layer(x, mod, l, P, ctx):
    sh0, sc0, g0, sh1, sc1, g1, sh2, sc2, g2 = jnp.split(mod, N_MOD, axis=-1)
    h = x * (1.0 + sc0) + sh0
    x = layer_norm(ALPHA * x + 0.5 * g0 * swiglu_ffn(h, P['w_ffn1_in'][l], P['w_ffn1_out'][l]),
                   P['ln_g'][l, 0], P['ln_b'][l, 0])
    h = x * (1.0 + sc1) + sh1
    y, new_ctx = token_mixer(h, l, P, ctx)
    x = layer_norm(ALPHA * x + g1 * y, P['ln_g'][l, 1], P['ln_b'][l, 1])
    h = x * (1.0 + sc2) + sh2
    x = layer_norm(ALPHA * x + 0.5 * g2 * swiglu_ffn(h, P['w_ffn2_in'][l], P['w_ffn2_out'][l]),
                   P['ln_g'][l, 2], P['ln_b'][l, 2])
    return x, new_ctx


def setup_inputs(seed: int = 0) -> dict:
    key = jax.random.key(seed)
    ks = jax.random.split(key, 32)
    f32 = jnp.float32
    nrm = lambda k, shape, s: jax.random.normal(k, shape, f32) * s
    D = D_MODEL
    return {
        'x_prompt': nrm(ks[0], (BATCH, SEQ, D), 1.0),
        'x_sample': nrm(ks[1], (DEC_BATCH, DEC_SEQ, D), 1.0),
        'cache_diff_k': nrm(ks[2], (DEC_BATCH, DEPTH, H_DIFF, PAST_LEN, 2 * DH_DIFF), 1.0),
        'cache_diff_v': nrm(ks[3], (DEC_BATCH, DEPTH, H_DIFF, PAST_LEN, DV_DIFF), 1.0),
        'state_gla': nrm(ks[4], (DEC_BATCH, DEPTH, 2, H_GLA, DK_GLA, DV_GLA), 1.0),
        'c': nrm(ks[5], (DEC_BATCH, D), 1.0),
        'c_ctx': nrm(ks[6], (D,), 1.0),
        'w_ada': nrm(ks[7], (DEPTH, D, N_MOD * D), 0.5 * D ** -0.5),
        'b_ada': nrm(ks[8], (DEPTH, N_MOD * D), 0.01),
        'w_ffn1_in': nrm(ks[9], (DEPTH, D, 2 * D_FF), D ** -0.5),
        'w_ffn1_out': nrm(ks[10], (DEPTH, D_FF, D), BETA * D_FF ** -0.5),
        'w_ffn2_in': nrm(ks[11], (DEPTH, D, 2 * D_FF), D ** -0.5),
        'w_ffn2_out': nrm(ks[12], (DEPTH, D_FF, D), BETA * D_FF ** -0.5),
        'w_in': nrm(ks[13], (DEPTH, D, IN_COLS), D ** -0.5),
        'conv_w': nrm(ks[14], (DEPTH, CONV_WIDTH, C_CONV), CONV_WIDTH ** -0.5),
        'conv_b': nrm(ks[15], (DEPTH, C_CONV), 0.01),
        'conv_ln_g': 1.0 + nrm(ks[16], (DEPTH, C_CONV), 0.01),
        'conv_ln_b': nrm(ks[17], (DEPTH, C_CONV), 0.01),
        'gla_w_a2': nrm(ks[18], (DEPTH, 2, GLA_RANK, W_GLA), GLA_RANK ** -0.5),
        'gla_b_a': nrm(ks[19], (DEPTH, 2, W_GLA), 0.01),
        'gla_norm_g': 1.0 + nrm(ks[20], (DEPTH, DV_GLA), 0.01),
        'diff_lam': nrm(ks[21], (DEPTH, 4, DH_DIFF), 0.1),
        'diff_norm_g': 1.0 + nrm(ks[22], (DEPTH, DV_DIFF), 0.01),
        'w_out': nrm(ks[23], (DEPTH, D_MIX, D), BETA * D_MIX ** -0.5),
        'ln_g': 1.0 + nrm(ks[24], (DEPTH, 3, D), 0.01),
        'ln_b': nrm(ks[25], (DEPTH, 3, D), 0.01),
    }


def reference(x_prompt, x_sample, cache_diff_k, cache_diff_v, state_gla, c, c_ctx,
              w_ada, b_ada, w_ffn1_in, w_ffn1_out, w_ffn2_in, w_ffn2_out, w_in,
              conv_w, conv_b, conv_ln_g, conv_ln_b, gla_w_a2, gla_b_a, gla_norm_g,
              diff_lam, diff_norm_g, w_out, ln_g, ln_b):
    P = dict(w_ffn1_in=w_ffn1_in, w_ffn1_out=w_ffn1_out, w_ffn2_in=w_ffn2_in, w_ffn2_out=w_ffn2_out,
             w_in=w_in, conv_w=conv_w, conv_b=conv_b, conv_ln_g=conv_ln_g, conv_ln_b=conv_ln_b,
             gla_w_a2=gla_w_a2, gla_b_a=gla_b_a, gla_norm_g=gla_norm_g, diff_lam=diff_lam,
             diff_norm_g=diff_norm_g, w_out=w_out, ln_g=ln_g, ln_b=ln_b)
    xp = x_prompt
    new_k, new_v, new_s = [], [], []
    for l in range(DEPTH):
        mod = (jax.nn.silu(c_ctx[None, :]) @ w_ada[l] + b_ada[l])[:, None, :]
        xp, (k_l, v_l, s_l) = trunk_layer(xp, mod, l, P, None)
        new_k.append(k_l)
        new_v.append(v_l)
        new_s.append(s_l)
    rows = x_sample.shape[1] // GRID_W
    cos, sin = axial_rope_tables(rows)
    xs = x_sample
    for l in range(DEPTH):
        mod = (jax.nn.silu(c) @ w_ada[l] + b_ada[l])[:, None, :]
        ctx = dict(k=cache_diff_k[:, l], v=cache_diff_v[:, l], state=state_gla[:, l], cos=cos, sin=sin)
        xs, _ = trunk_layer(xs, mod, l, P, ctx)
    new_diff_k = jnp.stack(new_k, axis=1)
    new_diff_v = jnp.stack(new_v, axis=1)
    new_gla = jnp.stack(new_s, axis=1)
    return (xp, xs, new_diff_k, new_diff_v, new_gla)
```

```python
import functools
import math

import jax
import jax.numpy as jnp
from jax import lax
from jax.experimental import pallas as pl
from jax.experimental.pallas import tpu as pltpu

F32 = jnp.float32
BF16 = jnp.bfloat16

GRID_W = 64
CONV_WIDTH = 31
H_GLA = 4
DK_GLA = 64
GLA_RANK = 16
GLA_TAU = 16.0
GLA_CHUNK = 64
H_DIFF = 4
DH_DIFF = 64
ROPE_BASE = 10000.0
LN_EPS = 1e-5
N_MOD = 9

LANES = 128
SUBLANES = 8
V7X_VMEM_BYTES = 64 * 1024 * 1024
VMEM_LIMIT = V7X_VMEM_BYTES - 8 * 1024 * 1024

CONV_HALO = 16
CONV_ROWS = 32


def _cparams(sem):
    return pltpu.CompilerParams(dimension_semantics=sem, vmem_limit_bytes=VMEM_LIMIT)


def _const_spec(shape):
    nd = len(shape)
    return pl.BlockSpec(shape, lambda *_: (0,) * nd, pipeline_mode=pl.Buffered(1))


def _dot(a, b):
    return jnp.dot(a, b, preferred_element_type=F32)


def _dot_nt(a, b):
    return lax.dot_general(a, b, (((1,), (1,)), ((), ())), preferred_element_type=F32)


def _dot_tn(a, b):
    return lax.dot_general(a, b, (((0,), (0,)), ((), ())), preferred_element_type=F32)


def _silu(x):
    return x * jax.nn.sigmoid(x)


def _layer_norm(y, g, b):
    mu = jnp.mean(y, axis=-1, keepdims=True)
    yc = y - mu
    var = jnp.mean(yc * yc, axis=-1, keepdims=True)
    return yc * lax.rsqrt(var + LN_EPS) * g + b


def _modulate(x, mod_ref, row0):
    sh = mod_ref[0, row0:row0 + 1, :]
    sc = mod_ref[0, row0 + 1:row0 + 2, :]
    g = mod_ref[0, row0 + 2:row0 + 3, :]
    return x * (1.0 + sc) + sh, g


def _ada_kernel(c_ref, w_ref, b_ref, o_ref):
    s = _silu(c_ref[...]).astype(BF16)
    o_ref[0] = _dot(s, w_ref[0].astype(BF16)) + b_ref[0]


def _ada(cvec, w_ada, b_ada):
    depth, d, n = w_ada.shape
    rows = cvec.shape[0]
    tn = d
    return pl.pallas_call(
        _ada_kernel,
        grid=(depth, n // tn),
        in_specs=[
            pl.BlockSpec((rows, d), lambda l, j: (0, 0)),
            pl.BlockSpec((1, d, tn), lambda l, j: (l, 0, j)),
            pl.BlockSpec((1, 1, tn), lambda l, j: (l, 0, j)),
        ],
        out_specs=pl.BlockSpec((1, rows, tn), lambda l, j: (l, 0, j)),
        out_shape=jax.ShapeDtypeStruct((depth, rows, n), F32),
        compiler_params=_cparams(("arbitrary", "arbitrary")),
        name="ada_mod",
    )(cvec, w_ada, b_ada.reshape(depth, 1, n))


def _ffn_kernel(x_ref, mod_ref, win_ref, wout_ref, lng_ref, lnb_ref, o_ref, acc_ref, *, row0, alpha):
    x = x_ref[0]
    h, g = _modulate(x, mod_ref, row0)
    h = h.astype(BF16)
    n_chunks, _, two_ck = win_ref.shape
    ck = two_ck // 2
    acc_ref[...] = jnp.zeros_like(acc_ref)

    def body(j, carry):
        u = _dot(h, win_ref[j])
        act = (_silu(u[:, :ck]) * u[:, ck:]).astype(BF16)
        acc_ref[...] += _dot(act, wout_ref[j])
        return carry

    lax.fori_loop(0, n_chunks, body, 0)
    y = alpha * x + (0.5 * g) * acc_ref[...]
    o_ref[0] = _layer_norm(y, lng_ref[...], lnb_ref[...])


def _ffn(x, mod, win, wout, lng, lnb, *, row0, alpha, tm):
    b, t, d = x.shape
    mod_rows = mod.shape[0]
    mod_idx = (lambda bi, ti: (bi, 0, 0)) if mod_rows > 1 else (lambda bi, ti: (0, 0, 0))
    kern = functools.partial(_ffn_kernel, row0=row0, alpha=alpha)
    return pl.pallas_call(
        kern,
        grid=(b, t // tm),
        in_specs=[
            pl.BlockSpec((1, tm, d), lambda bi, ti: (bi, ti, 0)),
            pl.BlockSpec((1, N_MOD, d), mod_idx),
            _const_spec(win.shape),
            _const_spec(wout.shape),
            _const_spec((1, d)),
            _const_spec((1, d)),
        ],
        out_specs=pl.BlockSpec((1, tm, d), lambda bi, ti: (bi, ti, 0)),
        out_shape=jax.ShapeDtypeStruct((b, t, d), F32),
        scratch_shapes=[pltpu.VMEM((tm, d), F32)],
        compiler_params=_cparams(("parallel", "parallel")),
        name="ffn",
    )(x, mod, win, wout, lng, lnb)


def _log_sigmoid(x):
    return jnp.minimum(x, 0.0) - jnp.log(1.0 + jnp.exp(-jnp.abs(x)))


def _rope(x, cos, sin_signed, first_half):
    rot = jnp.where(first_half, pltpu.roll(x, LANES - 16, 1), pltpu.roll(x, 16, 1))
    return x * cos + rot * sin_signed


def _mixin_kernel(*refs, rope, c_conv, w_gla, w_diff):
    if rope:
        (x_ref, mod_ref, w_ref, w2_ref, b2_ref, cos_ref, sin_ref,
         yglu_ref, gq_ref, gk_ref, gv_ref, gate_ref, gdec_ref, dq_ref, dk_ref, dv_ref) = refs
    else:
        (x_ref, mod_ref, w_ref, w2_ref, b2_ref,
         yglu_ref, gq_ref, gk_ref, gv_ref, gate_ref, gdec_ref, dq_ref, dk_ref, dv_ref) = refs
    x = x_ref[0]
    h, _ = _modulate(x, mod_ref, 3)
    h = h.astype(BF16)
    o_gla = 2 * c_conv
    o_diff = o_gla + 4 * w_gla
    o_lr = o_diff + 3 * w_diff

    zc = _dot(h, w_ref[:, 0:o_gla])
    yglu_ref[0] = zc[:, :c_conv] * jax.nn.sigmoid(zc[:, c_conv:])

    zg = _dot(h, w_ref[:, o_gla:o_diff])
    gq_ref[0] = zg[:, 0:w_gla] * (DK_GLA ** -0.5)
    gk_ref[0] = zg[:, w_gla:2 * w_gla]
    gv_ref[0] = zg[:, 2 * w_gla:3 * w_gla].astype(BF16)
    gate_ref[0] = zg[:, 3 * w_gla:4 * w_gla]

    zl = _dot(h, w_ref[:, o_lr:o_lr + LANES]).astype(BF16)
    pre = _dot(zl, w2_ref[...]) + b2_ref[...]
    gdec_ref[0] = _log_sigmoid(pre) * (1.0 / GLA_TAU)

    zd = _dot(h, w_ref[:, o_diff:o_lr])
    if rope:
        cos = cos_ref[...]
        sin = sin_ref[...]
        lane = lax.broadcasted_iota(jnp.int32, cos.shape, 1)
        first_half = (lane % 32) < 16
    for hd in range(H_DIFF):
        sl = slice(hd * LANES, (hd + 1) * LANES)
        q = zd[:, sl]
        k = zd[:, w_diff + hd * LANES:w_diff + (hd + 1) * LANES]
        v = zd[:, 2 * w_diff + hd * LANES:2 * w_diff + (hd + 1) * LANES]
        if rope:
            q = _rope(q, cos, sin, first_half)
            k = _rope(k, cos, sin, first_half)
        dq_ref[0, :, sl] = (q * (DH_DIFF ** -0.5)).astype(BF16)
        dk_ref[0, hd] = k.astype(dk_ref.dtype)
        dv_ref[0, hd] = v.astype(dv_ref.dtype)


def _mixin(x, mod, w, w2, b2, cos, sin, *, tm):
    b, t, d = x.shape
    rope = cos is not None
    c_conv = 256
    w_gla = H_GLA * DK_GLA
    w_diff = H_DIFF * 2 * DH_DIFF
    kv_dtype = BF16 if rope else F32
    mod_rows = mod.shape[0]
    mod_idx = (lambda bi, ti: (bi, 0, 0)) if mod_rows > 1 else (lambda bi, ti: (0, 0, 0))
    tok = lambda n: pl.BlockSpec((1, tm, n), lambda bi, ti: (bi, ti, 0))
    heads = pl.BlockSpec((1, H_DIFF, tm, LANES), lambda bi, ti: (bi, 0, ti, 0))
    in_specs = [tok(d), pl.BlockSpec((1, N_MOD, d), mod_idx),
                _const_spec(w.shape), _const_spec(w2.shape), _const_spec(b2.shape)]
    args = [x, mod, w, w2, b2]
    if rope:
        in_specs += [pl.BlockSpec((tm, LANES), lambda bi, ti: (ti, 0))] * 2
        args += [cos, sin]
    sds = jax.ShapeDtypeStruct
    out_shape = [
        sds((b, t, c_conv), F32),
        sds((b, t, w_gla), F32),
        sds((b, t, w_gla), F32),
        sds((b, t, w_gla), BF16),
        sds((b, t, w_gla), F32),
        sds((b, t, 2 * w_gla), F32),
        sds((b, t, w_diff), BF16),
        sds((b, H_DIFF, t, LANES), kv_dtype),
        sds((b, H_DIFF, t, LANES), kv_dtype),
    ]
    out_specs = [tok(c_conv), tok(w_gla), tok(w_gla), tok(w_gla), tok(w_gla), tok(2 * w_gla),
                 tok(w_diff), heads, heads]
    kern = functools.partial(_mixin_kernel, rope=rope, c_conv=c_conv, w_gla=w_gla, w_diff=w_diff)
    return pl.pallas_call(
        kern,
        grid=(b, t // tm),
        in_specs=in_specs,
        out_specs=out_specs,
        out_shape=out_shape,
        compiler_params=_cparams(("parallel", "parallel")),
        name="mixer_in",
    )(*args)


def _conv_kernel(prev_ref, cur_ref, next_ref, w_ref, b_ref, g_ref, beta_ref, o_ref, win_ref, *, tc, nt):
    i = pl.program_id(1)
    pad = CONV_WIDTH // 2
    win_ref[0:CONV_HALO] = jnp.where(i > 0, prev_ref[0], 0.0)
    win_ref[CONV_HALO:CONV_HALO + tc] = cur_ref[0]
    win_ref[CONV_HALO + tc:2 * CONV_HALO + tc] = jnp.where(i < nt - 1, next_ref[0], 0.0)
    w = w_ref[...]
    for r in range(0, tc, CONV_ROWS):
        acc = jnp.zeros((CONV_ROWS, w.shape[1]), F32)
        for k in range(CONV_WIDTH):
            start = CONV_HALO - pad + r + k
            acc = acc + win_ref[start:start + CONV_ROWS, :] * w[k:k + 1, :]
        y = _layer_norm(acc + b_ref[...], g_ref[...], beta_ref[...])
        o_ref[0, r:r + CONV_ROWS, :] = _silu(y).astype(o_ref.dtype)


def _conv(yglu, w, bias, g, beta, *, tc):
    b, t, c = yglu.shape
    nt = t // tc
    hb = tc // CONV_HALO
    n_halo = t // CONV_HALO
    kern = functools.partial(_conv_kernel, tc=tc, nt=nt)
    return pl.pallas_call(
        kern,
        grid=(b, nt),
        in_specs=[
            pl.BlockSpec((1, CONV_HALO, c), lambda bi, i: (bi, jnp.maximum(i * hb - 1, 0), 0)),
            pl.BlockSpec((1, tc, c), lambda bi, i: (bi, i, 0)),
            pl.BlockSpec((1, CONV_HALO, c), lambda bi, i: (bi, jnp.minimum((i + 1) * hb, n_halo - 1), 0)),
            _const_spec(w.shape), _const_spec((1, c)), _const_spec((1, c)), _const_spec((1, c)),
        ],
        out_specs=pl.BlockSpec((1, tc, c), lambda bi, i: (bi, i, 0)),
        out_shape=jax.ShapeDtypeStruct((b, t, c), BF16),
        scratch_shapes=[pltpu.VMEM((tc + 2 * CONV_HALO, c), F32)],
        compiler_params=_cparams(("parallel", "parallel")),
        name="conv_module",
    )(yglu, yglu, yglu, w, bias, g, beta)


def _split3(x):
    hi = x.astype(BF16)
    r1 = x - hi.astype(F32)
    mid = r1.astype(BF16)
    lo = (r1 - mid.astype(F32)).astype(BF16)
    return hi, mid, lo


def _gla_kernel(*refs, nc, has_state, want_state):
    it = iter(refs)
    q_ref, k_ref, v_ref, g_ref = next(it), next(it), next(it), next(it)
    s0_ref = next(it) if has_state else None
    o_ref = next(it)
    sfin_ref = next(it) if want_state else None
    s_scr = next(it)

    d = pl.program_id(1)
    i = pl.program_id(2)
    nblk = pl.num_programs(2)
    c = GLA_CHUNK
    width = s_scr.shape[0]
    nh = width // c

    @pl.when(i == 0)
    def _():
        if has_state:
            s_scr[...] = s0_ref[0, 0]
        else:
            s_scr[...] = jnp.zeros_like(s_scr)

    fwd = d == 0
    row = lax.broadcasted_iota(jnp.int32, (c, c), 0)
    col = lax.broadcasted_iota(jnp.int32, (c, c), 1)
    sgn = jnp.where(fwd, 1, -1)
    tri = (row - col) * sgn >= 0
    tri_b = jnp.where(tri, 1.0, 0.0).astype(BF16)
    rowa = lax.broadcasted_iota(jnp.int32, (c, width), 0)
    cola = lax.broadcasted_iota(jnp.int32, (c, width), 1) % c
    tri_cat = (rowa - cola) * sgn >= 0
    rb = lax.broadcasted_iota(jnp.int32, (width, width), 0) // c
    cb = lax.broadcasted_iota(jnp.int32, (width, width), 1) // c
    bd = rb == cb

    def chunk(j, carry):
        cidx = jnp.where(fwd, j, nc - 1 - j)
        off = pl.multiple_of(cidx * c, c)
        q = q_ref[0, pl.ds(off, c), :]
        k = k_ref[0, pl.ds(off, c), :]
        v = v_ref[0, pl.ds(off, c), :]
        g = g_ref[0, pl.ds(off, c), :]
        ghi, gmid, glo = _split3(g)
        gcum = _dot(tri_b, ghi) + _dot(tri_b, gmid) + _dot(tri_b, glo)
        gtot = jnp.where(fwd, gcum[c - 1:c, :], gcum[0:1, :])
        q_t = (q * jnp.exp(gcum)).astype(BF16)
        k_t = k * jnp.exp(-gcum)
        k_hat = (k * jnp.exp(gtot - gcum)).astype(BF16)
        decay = jnp.exp(gtot)
        s_in = s_scr[...]
        k_bd = jnp.where(bd, jnp.concatenate([k_t] * nh, axis=0), 0.0).astype(BF16)
        a = _dot_nt(q_t, k_bd)
        a = jnp.where(tri_cat, a, 0.0).astype(BF16)
        v_bd = jnp.where(bd, jnp.concatenate([v] * nh, axis=0), jnp.zeros((), v.dtype))
        o = _dot(a, v_bd) + _dot_nt(q_t, s_in.astype(BF16))
        o_ref[0, 0, pl.ds(off, c), :] = o
        kv_t = _dot_tn(v, k_hat)
        s_scr[...] = s_in * decay + jnp.where(bd, kv_t, 0.0)
        return carry

    lax.fori_loop(0, nc, chunk, 0)

    if want_state:
        @pl.when(i == nblk - 1)
        def _():
            sfin_ref[0, 0] = s_scr[...]


def _gla(q, k, v, gdec, s0, *, tb, want_state):
    b, t, w = q.shape
    nblk = t // tb
    nc = tb // GLA_CHUNK
    blk = lambda d, i: i + d * (nblk - 1 - 2 * i)
    tok = pl.BlockSpec((1, tb, w), lambda bi, d, i: (bi, blk(d, i), 0))
    in_specs = [tok, tok, tok, pl.BlockSpec((1, tb, w), lambda bi, d, i: (bi, blk(d, i), d))]
    args = [q, k, v, gdec]
    has_state = s0 is not None
    if has_state:
        in_specs.append(pl.BlockSpec((1, 1, w, w), lambda bi, d, i: (bi, d, 0, 0)))
        args.append(s0)
    out_shape = [jax.ShapeDtypeStruct((2, b, t, w), F32)]
    out_specs = [pl.BlockSpec((1, 1, tb, w), lambda bi, d, i: (d, bi, blk(d, i), 0))]
    if want_state:
        out_shape.append(jax.ShapeDtypeStruct((b, 2, w, w), F32))
        out_specs.append(pl.BlockSpec((1, 1, w, w), lambda bi, d, i: (bi, d, 0, 0)))
    kern = functools.partial(_gla_kernel, nc=nc, has_state=has_state, want_state=want_state)
    return pl.pallas_call(
        kern,
        grid=(b, 2, nblk),
        in_specs=in_specs,
        out_specs=out_specs,
        out_shape=out_shape,
        scratch_shapes=[pltpu.VMEM((w, w), F32)],
        compiler_params=_cparams(("parallel", "parallel", "arbitrary")),
        name="gla",
    )(*args)


def _attn_kernel(*refs, has_cache, lam_init):
    it = iter(refs)
    q_ref, k_ref, v_ref = next(it), next(it), next(it)
    kc_ref = next(it) if has_cache else None
    vc_ref = next(it) if has_cache else None
    lam_ref, g_ref, o_ref = next(it), next(it), next(it)

    p = lam_ref[...]
    lam = (jnp.exp(jnp.sum(p[0:1] * p[1:2], axis=-1, keepdims=True))
           - jnp.exp(jnp.sum(p[2:3] * p[3:4], axis=-1, keepdims=True)) + lam_init)

    q = q_ref[0]
    lane = lax.broadcasted_iota(jnp.int32, q.shape, 1)
    zero = jnp.zeros((), q.dtype)
    q1 = jnp.where(lane < DH_DIFF, q, zero)
    q2 = jnp.where(lane >= DH_DIFF, q, zero)
    keys = [k_ref[0, 0].astype(BF16)]
    vals = [v_ref[0, 0].astype(BF16)]
    if has_cache:
        keys.append(kc_ref[0, 0].astype(BF16))
        vals.append(vc_ref[0, 0].astype(BF16))

    def probs(qm):
        s = [_dot_nt(qm, kk) for kk in keys]
        m = functools.reduce(jnp.maximum, [jnp.max(x, axis=-1, keepdims=True) for x in s])
        e = [jnp.exp(x - m) for x in s]
        l = functools.reduce(lambda a, b: a + b, [jnp.sum(x, axis=-1, keepdims=True) for x in e])
        return e, l

    e1, l1 = probs(q1)
    e2, l2 = probs(q2)
    r1 = 1.0 / l1
    r2 = lam / l2
    o = None
    for x1, x2, vv in zip(e1, e2, vals):
        w = (x1 * r1 - x2 * r2).astype(BF16)
        t = _dot(w, vv)
        o = t if o is None else o + t
    ms = jnp.mean(o * o, axis=-1, keepdims=True)
    y = o * lax.rsqrt(ms + LN_EPS) * g_ref[...] * (1.0 - lam_init)
    o_ref[0] = y.astype(o_ref.dtype)


def _attn(dq, dk, dv, ck, cv, lam_p, g, *, lam_init, tq):
    b, t, _ = dq.shape
    tk = dk.shape[2]
    has_cache = ck is not None
    qspec = pl.BlockSpec((1, tq, LANES), lambda bi, h, qi: (bi, qi, h))
    kvspec = lambda n: pl.BlockSpec((1, 1, n, LANES), lambda bi, h, qi: (bi, h, 0, 0))
    in_specs = [qspec, kvspec(tk), kvspec(tk)]
    args = [dq, dk, dv]
    if has_cache:
        in_specs += [kvspec(ck.shape[2]), kvspec(cv.shape[2])]
        args += [ck, cv]
    in_specs += [_const_spec(lam_p.shape), _const_spec(g.shape)]
    args += [lam_p, g]
    kern = functools.partial(_attn_kernel, has_cache=has_cache, lam_init=lam_init)
    return pl.pallas_call(
        kern,
        grid=(b, H_DIFF, t // tq),
        in_specs=in_specs,
        out_specs=qspec,
        out_shape=jax.ShapeDtypeStruct(dq.shape, BF16),
        compiler_params=_cparams(("parallel", "parallel", "parallel")),
        name="diff_attn",
    )(*args)


def _mixout_kernel(x_ref, mod_ref, yc_ref, of_ref, ob_ref, gate_ref, yd_ref, gn_ref, w_ref,
                   lng_ref, lnb_ref, o_ref, *, alpha):
    x = x_ref[0]
    _, g1 = _modulate(x, mod_ref, 3)
    o = of_ref[0, 0] + ob_ref[0, 0]
    width = o.shape[1]
    rb = lax.broadcasted_iota(jnp.int32, (width, width), 0) // DK_GLA
    cb = lax.broadcasted_iota(jnp.int32, (width, width), 1) // DK_GLA
    ones_bd = jnp.where(rb == cb, 1.0, 0.0).astype(BF16)
    sq = o * o
    hi = sq.astype(BF16)
    lo = (sq - hi.astype(F32)).astype(BF16)
    ms = (_dot(hi, ones_bd) + _dot(lo, ones_bd)) * (1.0 / DK_GLA)
    yg = (o * lax.rsqrt(ms + LN_EPS) * gn_ref[...] * _silu(gate_ref[0])).astype(BF16)
    c_conv = yc_ref.shape[2]
    y = (_dot(yc_ref[0], w_ref[0:c_conv, :])
         + _dot(yg, w_ref[c_conv:c_conv + width, :])
         + _dot(yd_ref[0], w_ref[c_conv + width:, :]))
    o_ref[0] = _layer_norm(alpha * x + g1 * y, lng_ref[...], lnb_ref[...])


def _mixout(x, mod, yconv, o_gla, gate, ydiff, gn, w, lng, lnb, *, alpha, tm):
    b, t, d = x.shape
    mod_rows = mod.shape[0]
    mod_idx = (lambda bi, ti: (bi, 0, 0)) if mod_rows > 1 else (lambda bi, ti: (0, 0, 0))
    tok = lambda n: pl.BlockSpec((1, tm, n), lambda bi, ti: (bi, ti, 0))
    wg = o_gla.shape[3]
    kern = functools.partial(_mixout_kernel, alpha=alpha)
    return pl.pallas_call(
        kern,
        grid=(b, t // tm),
        in_specs=[
            tok(d), pl.BlockSpec((1, N_MOD, d), mod_idx), tok(yconv.shape[2]),
            pl.BlockSpec((1, 1, tm, wg), lambda bi, ti: (0, bi, ti, 0)),
            pl.BlockSpec((1, 1, tm, wg), lambda bi, ti: (1, bi, ti, 0)),
            tok(wg), tok(ydiff.shape[2]),
            _const_spec(gn.shape), _const_spec(w.shape), _const_spec((1, d)), _const_spec((1, d)),
        ],
        out_specs=tok(d),
        out_shape=jax.ShapeDtypeStruct((b, t, d), F32),
        compiler_params=_cparams(("parallel", "parallel")),
        name="mixer_out",
    )(x, mod, yconv, o_gla, o_gla, gate, ydiff, gn, w, lng, lnb)


def _rope_tables(t):
    rows = t // GRID_W
    row = jnp.repeat(jnp.arange(rows, dtype=F32), GRID_W)
    col = jnp.tile(jnp.arange(GRID_W, dtype=F32), rows)
    seg = DH_DIFF // 2
    inv = ROPE_BASE ** (-jnp.arange(0, seg, 2, dtype=F32) / seg)
    a_r = row[:, None] * inv
    a_c = col[:, None] * inv
    ang = jnp.concatenate([a_r, a_r, a_c, a_c], axis=-1)
    ang = jnp.concatenate([ang, ang], axis=-1)
    sign = jnp.where((jnp.arange(LANES) % 32) < 16, -1.0, 1.0).astype(F32)
    return jnp.cos(ang), jnp.sin(ang) * sign


def _ffn_weights(w_in, w_out, ck):
    d, two_ff = w_in.shape
    ff = two_ff // 2
    n = ff // ck
    win = w_in.reshape(d, 2, n, ck).transpose(2, 0, 1, 3).reshape(n, d, 2 * ck).astype(BF16)
    wout = w_out.reshape(n, ck, d).astype(BF16)
    return win, wout


def _pick_tile(t, pref):
    return pref if t % pref == 0 else t


def kernel(x_prompt, x_sample, cache_diff_k, cache_diff_v, state_gla, c, c_ctx, w_ada, b_ada, w_ffn1_in,
           w_ffn1_out, w_ffn2_in, w_ffn2_out, w_in, conv_w, conv_b, conv_ln_g, conv_ln_b, gla_w_a2, gla_b_a,
           gla_norm_g, diff_lam, diff_norm_g, w_out, ln_g, ln_b):
    depth, d, _ = w_ada.shape
    alpha = (2 * depth) ** 0.25
    n_dec = c.shape[0]
    c_conv = conv_w.shape[2]
    w_gla = H_GLA * DK_GLA
    w_diff = H_DIFF * 2 * DH_DIFF
    in_conv, in_gla = 2 * c_conv, 4 * w_gla + 2 * GLA_RANK
    ff_chunk = 256

    rows = -(-(n_dec + 1) // SUBLANES) * SUBLANES
    cvec = jnp.zeros((rows, d), F32).at[:n_dec].set(c).at[n_dec].set(c_ctx)
    mod = _ada(cvec, w_ada, b_ada).reshape(depth, rows, N_MOD, d)

    cos, sin = _rope_tables(x_sample.shape[1])

    layers = []
    for l in range(depth):
        wi = w_in[l]
        w_lr = jnp.pad(wi[:, in_conv + 4 * w_gla:in_conv + in_gla], ((0, 0), (0, LANES - 2 * GLA_RANK)))
        w_all = jnp.concatenate([wi[:, :in_conv + 4 * w_gla], wi[:, in_conv + in_gla:], w_lr], axis=1).astype(BF16)
        w2 = jnp.zeros((LANES, 2 * w_gla), F32)
        w2 = w2.at[:GLA_RANK, :w_gla].set(gla_w_a2[l, 0]).at[GLA_RANK:2 * GLA_RANK, w_gla:].set(gla_w_a2[l, 1])
        layers.append(dict(
            ffn1=_ffn_weights(w_ffn1_in[l], w_ffn1_out[l], ff_chunk),
            ffn2=_ffn_weights(w_ffn2_in[l], w_ffn2_out[l], ff_chunk),
            w_all=w_all, w2=w2.astype(BF16), b2=gla_b_a[l].reshape(1, 2 * w_gla),
            conv_w=conv_w[l], conv_b=conv_b[l][None], conv_g=conv_ln_g[l][None], conv_beta=conv_ln_b[l][None],
            gn=jnp.tile(gla_norm_g[l], H_GLA)[None], lam=diff_lam[l], dg=diff_norm_g[l][None],
            w_out=w_out[l].astype(BF16),
            lng=[ln_g[l, i][None] for i in range(3)], lnb=[ln_b[l, i][None] for i in range(3)],
            lam_init=0.8 - 0.6 * math.exp(-0.3 * l),
        ))

    def run_layer(x, mod_l, P, ctx):
        t = x.shape[1]
        tm = _pick_tile(t, 512)
        x = _ffn(x, mod_l, *P['ffn1'], P['lng'][0], P['lnb'][0], row0=0, alpha=alpha, tm=tm)
        rope = ctx is not None
        yglu, gq, gk, gv, gate, gdec, dq, dk, dv = _mixin(
            x, mod_l, P['w_all'], P['w2'], P['b2'], cos if rope else None, sin if rope else None, tm=tm)
        yconv = _conv(yglu, P['conv_w'], P['conv_b'], P['conv_g'], P['conv_beta'], tc=_pick_tile(t, 256))
        gla_out = _gla(gq, gk, gv, gdec, ctx['s0'] if rope else None, tb=_pick_tile(t, 512), want_state=not rope)
        ydiff = _attn(dq, dk, dv, ctx['k'] if rope else None, ctx['v'] if rope else None, P['lam'], P['dg'],
                      lam_init=P['lam_init'], tq=_pick_tile(t, 256))
        x = _mixout(x, mod_l, yconv, gla_out[0], gate, ydiff, P['gn'], P['w_out'], P['lng'][1], P['lnb'][1],
                    alpha=alpha, tm=tm)
        x = _ffn(x, mod_l, *P['ffn2'], P['lng'][2], P['lnb'][2], row0=6, alpha=alpha, tm=tm)
        return x, (dk, dv, gla_out[1] if not rope else None)

    xp = x_prompt
    new_k, new_v, new_s = [], [], []
    for l in range(depth):
        xp, (k_l, v_l, s_l) = run_layer(xp, mod[l, n_dec:n_dec + 1], layers[l], None)
        new_k.append(k_l)
        new_v.append(v_l)
        bsz = s_l.shape[0]
        s6 = s_l.reshape(bsz, 2, H_GLA, DK_GLA, H_GLA, DK_GLA)
        s_heads = jnp.stack([s6[:, :, h, :, h, :] for h in range(H_GLA)], axis=2)
        new_s.append(jnp.swapaxes(s_heads, -1, -2))

    xs = x_sample
    for l in range(depth):
        st = jnp.swapaxes(state_gla[:, l], -1, -2)
        s0 = jnp.zeros(st.shape[:2] + (H_GLA, DK_GLA, H_GLA, DK_GLA), F32)
        for h in range(H_GLA):
            s0 = s0.at[:, :, h, :, h, :].set(st[:, :, h])
        s0 = s0.reshape(st.shape[0], 2, w_gla, w_gla)
        ctx = dict(k=cache_diff_k[:, l].astype(BF16), v=cache_diff_v[:, l].astype(BF16), s0=s0)
        xs, _ = run_layer(xs, mod[l, :n_dec], layers[l], ctx)

    return (xp, xs, jnp.stack(new_k, axis=1), jnp.stack(new_v, axis=1), jnp.stack(new_s, axis=1))
```

```python
import functools
import math

import jax
import jax.numpy as jnp
from jax import lax
from jax.experimental import pallas as pl
from jax.experimental.pallas import tpu as pltpu

F32 = jnp.float32
BF16 = jnp.bfloat16

GRID_W = 64
CONV_WIDTH = 31
H_GLA = 4
DK_GLA = 64
GLA_RANK = 16
GLA_TAU = 16.0
GLA_CHUNK = 64
H_DIFF = 4
DH_DIFF = 64
ROPE_BASE = 10000.0
LN_EPS = 1e-5
N_MOD = 9

LANES = 128
SUBLANES = 8
V7X_VMEM_BYTES = 64 * 1024 * 1024
VMEM_LIMIT = V7X_VMEM_BYTES - 8 * 1024 * 1024

CONV_HALO = 16
CONV_ROWS = 32
ATTN_STRIP = 32


def _cparams(sem):
    return pltpu.CompilerParams(dimension_semantics=sem, vmem_limit_bytes=VMEM_LIMIT)


def _const_spec(shape):
    nd = len(shape)
    return pl.BlockSpec(shape, lambda *_: (0,) * nd, pipeline_mode=pl.Buffered(1))


def _dot(a, b):
    return jnp.dot(a, b, preferred_element_type=F32)


def _dot_nt(a, b):
    return lax.dot_general(a, b, (((1,), (1,)), ((), ())), preferred_element_type=F32)


def _dot_tn(a, b):
    return lax.dot_general(a, b, (((0,), (0,)), ((), ())), preferred_element_type=F32)


def _silu(x):
    return x * jax.nn.sigmoid(x)


def _layer_norm(y, g, b):
    mu = jnp.mean(y, axis=-1, keepdims=True)
    yc = y - mu
    var = jnp.mean(yc * yc, axis=-1, keepdims=True)
    return yc * lax.rsqrt(var + LN_EPS) * g + b


def _modulate(x, mod_ref, row0):
    sh = mod_ref[0, row0:row0 + 1, :]
    sc = mod_ref[0, row0 + 1:row0 + 2, :]
    g = mod_ref[0, row0 + 2:row0 + 3, :]
    return x * (1.0 + sc) + sh, g


def _ada_kernel(c_ref, w_ref, b_ref, o_ref):
    s = _silu(c_ref[...]).astype(BF16)
    o_ref[0] = _dot(s, w_ref[0].astype(BF16)) + b_ref[0]


def _ada(cvec, w_ada, b_ada):
    depth, d, n = w_ada.shape
    rows = cvec.shape[0]
    tn = d
    return pl.pallas_call(
        _ada_kernel,
        grid=(depth, n // tn),
        in_specs=[
            pl.BlockSpec((rows, d), lambda l, j: (0, 0)),
            pl.BlockSpec((1, d, tn), lambda l, j: (l, 0, j)),
            pl.BlockSpec((1, 1, tn), lambda l, j: (l, 0, j)),
        ],
        out_specs=pl.BlockSpec((1, rows, tn), lambda l, j: (l, 0, j)),
        out_shape=jax.ShapeDtypeStruct((depth, rows, n), F32),
        compiler_params=_cparams(("arbitrary", "arbitrary")),
        name="ada_mod",
    )(cvec, w_ada, b_ada.reshape(depth, 1, n))


def _ffn_kernel(x_ref, mod_ref, win_ref, wout_ref, lng_ref, lnb_ref, o_ref, acc_ref, *, row0, alpha):
    x = x_ref[0]
    h, g = _modulate(x, mod_ref, row0)
    h = h.astype(BF16)
    n_chunks, _, two_ck = win_ref.shape
    ck = two_ck // 2
    acc_ref[...] = jnp.zeros_like(acc_ref)

    def body(j, carry):
        u = _dot(h, win_ref[j])
        act = (_silu(u[:, :ck]) * u[:, ck:]).astype(BF16)
        acc_ref[...] += _dot(act, wout_ref[j])
        return carry

    lax.fori_loop(0, n_chunks, body, 0)
    y = alpha * x + (0.5 * g) * acc_ref[...]
    o_ref[0] = _layer_norm(y, lng_ref[...], lnb_ref[...])


def _ffn(x, mod, win, wout, lng, lnb, *, row0, alpha, tm):
    b, t, d = x.shape
    mod_rows = mod.shape[0]
    mod_idx = (lambda bi, ti: (bi, 0, 0)) if mod_rows > 1 else (lambda bi, ti: (0, 0, 0))
    kern = functools.partial(_ffn_kernel, row0=row0, alpha=alpha)
    return pl.pallas_call(
        kern,
        grid=(b, t // tm),
        in_specs=[
            pl.BlockSpec((1, tm, d), lambda bi, ti: (bi, ti, 0)),
            pl.BlockSpec((1, N_MOD, d), mod_idx),
            _const_spec(win.shape),
            _const_spec(wout.shape),
            _const_spec((1, d)),
            _const_spec((1, d)),
        ],
        out_specs=pl.BlockSpec((1, tm, d), lambda bi, ti: (bi, ti, 0)),
        out_shape=jax.ShapeDtypeStruct((b, t, d), F32),
        scratch_shapes=[pltpu.VMEM((tm, d), F32)],
        compiler_params=_cparams(("parallel", "parallel")),
        name="ffn",
    )(x, mod, win, wout, lng, lnb)


def _log_sigmoid(x):
    return jnp.minimum(x, 0.0) - jnp.log(1.0 + jnp.exp(-jnp.abs(x)))


def _rope(x, cos, sin_signed, first_half):
    rot = jnp.where(first_half, pltpu.roll(x, LANES - 16, 1), pltpu.roll(x, 16, 1))
    return x * cos + rot * sin_signed


def _mixin_kernel(*refs, rope, c_conv, w_gla, w_diff):
    if rope:
        (x_ref, mod_ref, w_ref, w2_ref, b2_ref, cos_ref, sin_ref,
         yglu_ref, gq_ref, gk_ref, gv_ref, gate_ref, gdec_ref, dq_ref, dk_ref, dvt_ref) = refs
    else:
        (x_ref, mod_ref, w_ref, w2_ref, b2_ref,
         yglu_ref, gq_ref, gk_ref, gv_ref, gate_ref, gdec_ref, dq_ref, dk_ref, dvt_ref, dv_ref) = refs
    x = x_ref[0]
    h, _ = _modulate(x, mod_ref, 3)
    h = h.astype(BF16)
    o_gla = 2 * c_conv
    o_diff = o_gla + 4 * w_gla
    o_lr = o_diff + 3 * w_diff

    zc = _dot(h, w_ref[:, 0:o_gla])
    yglu_ref[0] = zc[:, :c_conv] * jax.nn.sigmoid(zc[:, c_conv:])

    zg = _dot(h, w_ref[:, o_gla:o_diff])
    gq_ref[0] = zg[:, 0:w_gla] * (DK_GLA ** -0.5)
    gk_ref[0] = zg[:, w_gla:2 * w_gla]
    gv_ref[0] = zg[:, 2 * w_gla:3 * w_gla].astype(BF16)
    gate_ref[0] = zg[:, 3 * w_gla:4 * w_gla]

    zl = _dot(h, w_ref[:, o_lr:o_lr + LANES]).astype(BF16)
    pre = _dot(zl, w2_ref[...]) + b2_ref[...]
    gdec_ref[0] = _log_sigmoid(pre) * (1.0 / GLA_TAU)

    zd = _dot(h, w_ref[:, o_diff:o_lr])
    if rope:
        cos = cos_ref[...]
        sin = sin_ref[...]
        lane = lax.broadcasted_iota(jnp.int32, cos.shape, 1)
        first_half = (lane % 32) < 16
    for hd in range(H_DIFF):
        sl = slice(hd * LANES, (hd + 1) * LANES)
        q = zd[:, sl]
        k = zd[:, w_diff + hd * LANES:w_diff + (hd + 1) * LANES]
        v = zd[:, 2 * w_diff + hd * LANES:2 * w_diff + (hd + 1) * LANES]
        if rope:
            q = _rope(q, cos, sin, first_half)
            k = _rope(k, cos, sin, first_half)
        dq_ref[0, :, sl] = (q * (DH_DIFF ** -0.5 * math.log2(math.e))).astype(BF16)
        dk_ref[0, hd] = k.astype(dk_ref.dtype)
        dvt_ref[0, hd, 0] = v.T.astype(BF16)
        if not rope:
            dv_ref[0, hd] = v


def _mixin(x, mod, w, w2, b2, cos, sin, *, tm):
    b, t, d = x.shape
    rope = cos is not None
    c_conv = 256
    w_gla = H_GLA * DK_GLA
    w_diff = H_DIFF * 2 * DH_DIFF
    kv_dtype = BF16 if rope else F32
    mod_rows = mod.shape[0]
    mod_idx = (lambda bi, ti: (bi, 0, 0)) if mod_rows > 1 else (lambda bi, ti: (0, 0, 0))
    tok = lambda n: pl.BlockSpec((1, tm, n), lambda bi, ti: (bi, ti, 0))
    heads = pl.BlockSpec((1, H_DIFF, tm, LANES), lambda bi, ti: (bi, 0, ti, 0))
    in_specs = [tok(d), pl.BlockSpec((1, N_MOD, d), mod_idx),
                _const_spec(w.shape), _const_spec(w2.shape), _const_spec(b2.shape)]
    args = [x, mod, w, w2, b2]
    if rope:
        in_specs += [pl.BlockSpec((tm, LANES), lambda bi, ti: (ti, 0))] * 2
        args += [cos, sin]
    sds = jax.ShapeDtypeStruct
    out_shape = [
        sds((b, t, c_conv), F32),
        sds((b, t, w_gla), F32),
        sds((b, t, w_gla), F32),
        sds((b, t, w_gla), BF16),
        sds((b, t, w_gla), F32),
        sds((b, t, 2 * w_gla), F32),
        sds((b, t, w_diff), BF16),
        sds((b, H_DIFF, t, LANES), kv_dtype),
        sds((b, H_DIFF, t // tm, LANES, tm), BF16),
    ]
    out_specs = [tok(c_conv), tok(w_gla), tok(w_gla), tok(w_gla), tok(w_gla), tok(2 * w_gla),
                 tok(w_diff), heads,
                 pl.BlockSpec((1, H_DIFF, 1, LANES, tm), lambda bi, ti: (bi, 0, ti, 0, 0))]
    if not rope:
        out_shape.append(sds((b, H_DIFF, t, LANES), F32))
        out_specs.append(heads)
    kern = functools.partial(_mixin_kernel, rope=rope, c_conv=c_conv, w_gla=w_gla, w_diff=w_diff)
    return pl.pallas_call(
        kern,
        grid=(b, t // tm),
        in_specs=in_specs,
        out_specs=out_specs,
        out_shape=out_shape,
        compiler_params=_cparams(("parallel", "parallel")),
        name="mixer_in",
    )(*args)


def _conv_kernel(prev_ref, cur_ref, next_ref, w_ref, b_ref, g_ref, beta_ref, o_ref, win_ref, *, tc, nt):
    i = pl.program_id(1)
    pad = CONV_WIDTH // 2
    win_ref[0:CONV_HALO] = jnp.where(i > 0, prev_ref[0], 0.0)
    win_ref[CONV_HALO:CONV_HALO + tc] = cur_ref[0]
    win_ref[CONV_HALO + tc:2 * CONV_HALO + tc] = jnp.where(i < nt - 1, next_ref[0], 0.0)
    w = w_ref[...]
    for r in range(0, tc, CONV_ROWS):
        acc = jnp.zeros((CONV_ROWS, w.shape[1]), F32)
        for k in range(CONV_WIDTH):
            start = CONV_HALO - pad + r + k
            acc = acc + win_ref[start:start + CONV_ROWS, :] * w[k:k + 1, :]
        y = _layer_norm(acc + b_ref[...], g_ref[...], beta_ref[...])
        o_ref[0, r:r + CONV_ROWS, :] = _silu(y).astype(o_ref.dtype)


def _conv(yglu, w, bias, g, beta, *, tc):
    b, t, c = yglu.shape
    nt = t // tc
    hb = tc // CONV_HALO
    n_halo = t // CONV_HALO
    kern = functools.partial(_conv_kernel, tc=tc, nt=nt)
    return pl.pallas_call(
        kern,
        grid=(b, nt),
        in_specs=[
            pl.BlockSpec((1, CONV_HALO, c), lambda bi, i: (bi, jnp.maximum(i * hb - 1, 0), 0)),
            pl.BlockSpec((1, tc, c), lambda bi, i: (bi, i, 0)),
            pl.BlockSpec((1, CONV_HALO, c), lambda bi, i: (bi, jnp.minimum((i + 1) * hb, n_halo - 1), 0)),
            _const_spec(w.shape), _const_spec((1, c)), _const_spec((1, c)), _const_spec((1, c)),
        ],
        out_specs=pl.BlockSpec((1, tc, c), lambda bi, i: (bi, i, 0)),
        out_shape=jax.ShapeDtypeStruct((b, t, c), BF16),
        scratch_shapes=[pltpu.VMEM((tc + 2 * CONV_HALO, c), F32)],
        compiler_params=_cparams(("parallel", "parallel")),
        name="conv_module",
    )(yglu, yglu, yglu, w, bias, g, beta)


def _split3(x):
    hi = x.astype(BF16)
    r1 = x - hi.astype(F32)
    mid = r1.astype(BF16)
    lo = (r1 - mid.astype(F32)).astype(BF16)
    return hi, mid, lo


def _gla_kernel(*refs, nc, has_state, want_state):
    it = iter(refs)
    q_ref, k_ref, v_ref, g_ref = next(it), next(it), next(it), next(it)
    s0_ref = next(it) if has_state else None
    o_ref = next(it)
    sfin_ref = next(it) if want_state else None
    s_scr = next(it)

    d = pl.program_id(1)
    i = pl.program_id(2)
    nblk = pl.num_programs(2)
    c = GLA_CHUNK
    width = s_scr.shape[0]
    nh = width // c

    @pl.when(i == 0)
    def _():
        if has_state:
            s_scr[...] = s0_ref[0, 0]
        else:
            s_scr[...] = jnp.zeros_like(s_scr)

    fwd = d == 0
    row = lax.broadcasted_iota(jnp.int32, (c, c), 0)
    col = lax.broadcasted_iota(jnp.int32, (c, c), 1)
    sgn = jnp.where(fwd, 1, -1)
    tri = (row - col) * sgn >= 0
    tri_b = jnp.where(tri, 1.0, 0.0).astype(BF16)
    rowa = lax.broadcasted_iota(jnp.int32, (c, width), 0)
    cola = lax.broadcasted_iota(jnp.int32, (c, width), 1) % c
    tri_cat = (rowa - cola) * sgn >= 0
    rb = lax.broadcasted_iota(jnp.int32, (width, width), 0) // c
    cb = lax.broadcasted_iota(jnp.int32, (width, width), 1) // c
    bd = rb == cb

    def chunk(j, carry):
        cidx = jnp.where(fwd, j, nc - 1 - j)
        off = pl.multiple_of(cidx * c, c)
        q = q_ref[0, pl.ds(off, c), :]
        k = k_ref[0, pl.ds(off, c), :]
        v = v_ref[0, pl.ds(off, c), :]
        g = g_ref[0, pl.ds(off, c), :]
        ghi, gmid, glo = _split3(g)
        gcum = _dot(tri_b, ghi) + _dot(tri_b, gmid) + _dot(tri_b, glo)
        gtot = jnp.where(fwd, gcum[c - 1:c, :], gcum[0:1, :])
        q_t = (q * jnp.exp(gcum)).astype(BF16)
        k_t = k * jnp.exp(-gcum)
        k_hat = (k * jnp.exp(gtot - gcum)).astype(BF16)
        decay = jnp.exp(gtot)
        s_in = s_scr[...]
        k_bd = jnp.where(bd, jnp.concatenate([k_t] * nh, axis=0), 0.0).astype(BF16)
        a = _dot_nt(q_t, k_bd)
        a = jnp.where(tri_cat, a, 0.0).astype(BF16)
        v_bd = jnp.where(bd, jnp.concatenate([v] * nh, axis=0), jnp.zeros((), v.dtype))
        o = _dot(a, v_bd) + _dot_nt(q_t, s_in.astype(BF16))
        o_ref[0, 0, pl.ds(off, c), :] = o
        kv_t = _dot_tn(v, k_hat)
        s_scr[...] = s_in * decay + jnp.where(bd, kv_t, 0.0)
        return carry

    lax.fori_loop(0, nc, chunk, 0)

    if want_state:
        @pl.when(i == nblk - 1)
        def _():
            sfin_ref[0, 0] = s_scr[...]


def _gla(q, k, v, gdec, s0, *, tb, want_state):
    b, t, w = q.shape
    nblk = t // tb
    nc = tb // GLA_CHUNK
    blk = lambda d, i: i + d * (nblk - 1 - 2 * i)
    tok = pl.BlockSpec((1, tb, w), lambda bi, d, i: (bi, blk(d, i), 0))
    in_specs = [tok, tok, tok, pl.BlockSpec((1, tb, w), lambda bi, d, i: (bi, blk(d, i), d))]
    args = [q, k, v, gdec]
    has_state = s0 is not None
    if has_state:
        in_specs.append(pl.BlockSpec((1, 1, w, w), lambda bi, d, i: (bi, d, 0, 0)))
        args.append(s0)
    out_shape = [jax.ShapeDtypeStruct((2, b, t, w), F32)]
    out_specs = [pl.BlockSpec((1, 1, tb, w), lambda bi, d, i: (d, bi, blk(d, i), 0))]
    if want_state:
        out_shape.append(jax.ShapeDtypeStruct((b, 2, w, w), F32))
        out_specs.append(pl.BlockSpec((1, 1, w, w), lambda bi, d, i: (bi, d, 0, 0)))
    kern = functools.partial(_gla_kernel, nc=nc, has_state=has_state, want_state=want_state)
    return pl.pallas_call(
        kern,
        grid=(b, 2, nblk),
        in_specs=in_specs,
        out_specs=out_specs,
        out_shape=out_shape,
        scratch_shapes=[pltpu.VMEM((w, w), F32)],
        compiler_params=_cparams(("parallel", "parallel", "arbitrary")),
        name="gla",
    )(*args)


def _attn_kernel(q_ref, k_ref, vt_ref, lam_ref, g_ref, o_ref, s_scr, e_scr, l_scr, acc_scr, *, lam_init):
    p = lam_ref[...]
    lam = (jnp.exp(jnp.sum(p[0:1] * p[1:2], axis=-1, keepdims=True))
           - jnp.exp(jnp.sum(p[2:3] * p[3:4], axis=-1, keepdims=True)) + lam_init)

    q = q_ref[0]
    tq = q.shape[0]
    lane = lax.broadcasted_iota(jnp.int32, q.shape, 1)
    zero = jnp.zeros((), q.dtype)
    qs = jnp.concatenate([jnp.where(lane < DH_DIFF, q, zero), jnp.where(lane >= DH_DIFF, q, zero)], axis=0)

    n, tk = vt_ref.shape[2], vt_ref.shape[4]
    nq = 2 * tq
    l_scr[...] = jnp.zeros_like(l_scr)
    acc_scr[...] = jnp.zeros_like(acc_scr)

    def scores(j, slot):
        off = j * tk if isinstance(j, int) else pl.multiple_of(j * tk, tk)
        s = _dot_nt(k_ref[0, 0, pl.ds(off, tk), :].astype(BF16), qs)
        s_scr[slot] = s
        return jnp.max(s, axis=0, keepdims=True)

    def softmax(slot, mt, m):
        m_new = jnp.maximum(m, mt)
        alpha = jnp.exp2(m - m_new)
        lt = jnp.zeros((SUBLANES, nq), F32)
        for r in range(0, tk, ATTN_STRIP):
            e = jnp.exp2(s_scr[slot, r:r + ATTN_STRIP, :] - m_new)
            lt = lt + functools.reduce(lambda a, b: a + b,
                                       [e[i:i + SUBLANES] for i in range(0, ATTN_STRIP, SUBLANES)])
            e_scr[slot, r:r + ATTN_STRIP, :] = e.astype(BF16)
        l_scr[...] = l_scr[...] * alpha + lt
        return m_new, alpha

    def weighted(j, slot, alpha):
        acc_scr[...] = acc_scr[...] * alpha + _dot(vt_ref[0, 0, j], e_scr[slot])

    def step(j, slot, mt, m, alpha_prev, last=False):
        mt_next = mt if last else scores(j + 1, 1 - slot)
        m, alpha = softmax(slot, mt, m)
        weighted(j - 1, 1 - slot, alpha_prev)
        return mt_next, m, alpha

    m = jnp.full((1, nq), -1e30, F32)
    mt = scores(0, 0)
    if n == 1:
        m, alpha = softmax(0, mt, m)
    else:
        mt_next = scores(1, 1)
        m, alpha = softmax(0, mt, m)
        mt = mt_next
        for j in range(1, n - 1):
            mt, m, alpha = step(j, j % 2, mt, m, alpha)
        mt, m, alpha = step(n - 1, (n - 1) % 2, mt, m, alpha, last=True)
    weighted(n - 1, (n - 1) % 2, alpha)
    on = acc_scr[...] / jnp.sum(l_scr[...], axis=0, keepdims=True)
    o = on[:, :tq] - lam * on[:, tq:]
    ms = jnp.mean(o * o, axis=0, keepdims=True)
    y = o * lax.rsqrt(ms + LN_EPS) * g_ref[...] * (1.0 - lam_init)
    o_ref[0] = y.T.astype(o_ref.dtype)


def _attn(dq, dk, dvt, lam_p, g_col, *, lam_init, tq):
    b, t, _ = dq.shape
    _, _, n_tiles, dv, tk = dvt.shape
    qspec = pl.BlockSpec((1, tq, LANES), lambda bi, h, qi: (bi, qi, h))
    kspec = pl.BlockSpec((1, 1, n_tiles * tk, LANES), lambda bi, h, qi: (bi, h, 0, 0))
    vspec = pl.BlockSpec((1, 1, n_tiles, dv, tk), lambda bi, h, qi: (bi, h, 0, 0, 0))
    kern = functools.partial(_attn_kernel, lam_init=lam_init)
    return pl.pallas_call(
        kern,
        grid=(b, H_DIFF, t // tq),
        in_specs=[qspec, kspec, vspec, _const_spec(lam_p.shape), _const_spec(g_col.shape)],
        out_specs=qspec,
        out_shape=jax.ShapeDtypeStruct(dq.shape, BF16),
        scratch_shapes=[pltpu.VMEM((2, tk, 2 * tq), F32), pltpu.VMEM((2, tk, 2 * tq), BF16),
                        pltpu.VMEM((SUBLANES, 2 * tq), F32), pltpu.VMEM((dv, 2 * tq), F32)],
        compiler_params=_cparams(("parallel", "parallel", "parallel")),
        name="diff_attn",
    )(dq, dk, dvt, lam_p, g_col)


def _mixout_kernel(x_ref, mod_ref, yc_ref, of_ref, ob_ref, gate_ref, yd_ref, gn_ref, w_ref,
                   lng_ref, lnb_ref, o_ref, *, alpha):
    x = x_ref[0]
    _, g1 = _modulate(x, mod_ref, 3)
    o = of_ref[0, 0] + ob_ref[0, 0]
    width = o.shape[1]
    rb = lax.broadcasted_iota(jnp.int32, (width, width), 0) // DK_GLA
    cb = lax.broadcasted_iota(jnp.int32, (width, width), 1) // DK_GLA
    ones_bd = jnp.where(rb == cb, 1.0, 0.0).astype(BF16)
    sq = o * o
    hi = sq.astype(BF16)
    lo = (sq - hi.astype(F32)).astype(BF16)
    ms = (_dot(hi, ones_bd) + _dot(lo, ones_bd)) * (1.0 / DK_GLA)
    yg = (o * lax.rsqrt(ms + LN_EPS) * gn_ref[...] * _silu(gate_ref[0])).astype(BF16)
    c_conv = yc_ref.shape[2]
    y = (_dot(yc_ref[0], w_ref[0:c_conv, :])
         + _dot(yg, w_ref[c_conv:c_conv + width, :])
         + _dot(yd_ref[0], w_ref[c_conv + width:, :]))
    o_ref[0] = _layer_norm(alpha * x + g1 * y, lng_ref[...], lnb_ref[...])


def _mixout(x, mod, yconv, o_gla, gate, ydiff, gn, w, lng, lnb, *, alpha, tm):
    b, t, d = x.shape
    mod_rows = mod.shape[0]
    mod_idx = (lambda bi, ti: (bi, 0, 0)) if mod_rows > 1 else (lambda bi, ti: (0, 0, 0))
    tok = lambda n: pl.BlockSpec((1, tm, n), lambda bi, ti: (bi, ti, 0))
    wg = o_gla.shape[3]
    kern = functools.partial(_mixout_kernel, alpha=alpha)
    return pl.pallas_call(
        kern,
        grid=(b, t // tm),
        in_specs=[
            tok(d), pl.BlockSpec((1, N_MOD, d), mod_idx), tok(yconv.shape[2]),
            pl.BlockSpec((1, 1, tm, wg), lambda bi, ti: (0, bi, ti, 0)),
            pl.BlockSpec((1, 1, tm, wg), lambda bi, ti: (1, bi, ti, 0)),
            tok(wg), tok(ydiff.shape[2]),
            _const_spec(gn.shape), _const_spec(w.shape), _const_spec((1, d)), _const_spec((1, d)),
        ],
        out_specs=tok(d),
        out_shape=jax.ShapeDtypeStruct((b, t, d), F32),
        compiler_params=_cparams(("parallel", "parallel")),
        name="mixer_out",
    )(x, mod, yconv, o_gla, o_gla, gate, ydiff, gn, w, lng, lnb)


def _rope_tables(t):
    rows = t // GRID_W
    row = jnp.repeat(jnp.arange(rows, dtype=F32), GRID_W)
    col = jnp.tile(jnp.arange(GRID_W, dtype=F32), rows)
    seg = DH_DIFF // 2
    inv = ROPE_BASE ** (-jnp.arange(0, seg, 2, dtype=F32) / seg)
    a_r = row[:, None] * inv
    a_c = col[:, None] * inv
    ang = jnp.concatenate([a_r, a_r, a_c, a_c], axis=-1)
    ang = jnp.concatenate([ang, ang], axis=-1)
    sign = jnp.where((jnp.arange(LANES) % 32) < 16, -1.0, 1.0).astype(F32)
    return jnp.cos(ang), jnp.sin(ang) * sign


def _ffn_weights(w_in, w_out, ck):
    d, two_ff = w_in.shape
    ff = two_ff // 2
    n = ff // ck
    win = w_in.reshape(d, 2, n, ck).transpose(2, 0, 1, 3).reshape(n, d, 2 * ck).astype(BF16)
    wout = w_out.reshape(n, ck, d).astype(BF16)
    return win, wout


def _pick_tile(t, pref):
    return pref if t % pref == 0 else t


def kernel(x_prompt, x_sample, cache_diff_k, cache_diff_v, state_gla, c, c_ctx, w_ada, b_ada, w_ffn1_in,
           w_ffn1_out, w_ffn2_in, w_ffn2_out, w_in, conv_w, conv_b, conv_ln_g, conv_ln_b, gla_w_a2, gla_b_a,
           gla_norm_g, diff_lam, diff_norm_g, w_out, ln_g, ln_b):
    depth, d, _ = w_ada.shape
    alpha = (2 * depth) ** 0.25
    n_dec = c.shape[0]
    c_conv = conv_w.shape[2]
    w_gla = H_GLA * DK_GLA
    w_diff = H_DIFF * 2 * DH_DIFF
    in_conv, in_gla = 2 * c_conv, 4 * w_gla + 2 * GLA_RANK
    ff_chunk = 256

    rows = -(-(n_dec + 1) // SUBLANES) * SUBLANES
    cvec = jnp.zeros((rows, d), F32).at[:n_dec].set(c).at[n_dec].set(c_ctx)
    mod = _ada(cvec, w_ada, b_ada).reshape(depth, rows, N_MOD, d)

    cos, sin = _rope_tables(x_sample.shape[1])

    layers = []
    for l in range(depth):
        wi = w_in[l]
        w_lr = jnp.pad(wi[:, in_conv + 4 * w_gla:in_conv + in_gla], ((0, 0), (0, LANES - 2 * GLA_RANK)))
        w_all = jnp.concatenate([wi[:, :in_conv + 4 * w_gla], wi[:, in_conv + in_gla:], w_lr], axis=1).astype(BF16)
        w2 = jnp.zeros((LANES, 2 * w_gla), F32)
        w2 = w2.at[:GLA_RANK, :w_gla].set(gla_w_a2[l, 0]).at[GLA_RANK:2 * GLA_RANK, w_gla:].set(gla_w_a2[l, 1])
        layers.append(dict(
            ffn1=_ffn_weights(w_ffn1_in[l], w_ffn1_out[l], ff_chunk),
            ffn2=_ffn_weights(w_ffn2_in[l], w_ffn2_out[l], ff_chunk),
            w_all=w_all, w2=w2.astype(BF16), b2=gla_b_a[l].reshape(1, 2 * w_gla),
            conv_w=conv_w[l], conv_b=conv_b[l][None], conv_g=conv_ln_g[l][None], conv_beta=conv_ln_b[l][None],
            gn=jnp.tile(gla_norm_g[l], H_GLA)[None], lam=diff_lam[l], dg=diff_norm_g[l][:, None],
            w_out=w_out[l].astype(BF16),
            lng=[ln_g[l, i][None] for i in range(3)], lnb=[ln_b[l, i][None] for i in range(3)],
            lam_init=0.8 - 0.6 * math.exp(-0.3 * l),
        ))

    def run_layer(x, mod_l, P, ctx):
        t = x.shape[1]
        tm = _pick_tile(t, 512)
        x = _ffn(x, mod_l, *P['ffn1'], P['lng'][0], P['lnb'][0], row0=0, alpha=alpha, tm=tm)
        rope = ctx is not None
        yglu, gq, gk, gv, gate, gdec, dq, dk, dvt, *dv = _mixin(
            x, mod_l, P['w_all'], P['w2'], P['b2'], cos if rope else None, sin if rope else None, tm=tm)
        yconv = _conv(yglu, P['conv_w'], P['conv_b'], P['conv_g'], P['conv_beta'], tc=_pick_tile(t, 256))
        gla_out = _gla(gq, gk, gv, gdec, ctx['s0'] if rope else None, tb=_pick_tile(t, 512), want_state=not rope)
        keys, vals_t = dk, dvt
        if rope:
            keys = jnp.concatenate([dk, ctx['k']], axis=2)
            vals_t = jnp.concatenate([dvt, ctx['vt']], axis=2)
        ydiff = _attn(dq, keys, vals_t, P['lam'], P['dg'], lam_init=P['lam_init'], tq=_pick_tile(t, 512))
        x = _mixout(x, mod_l, yconv, gla_out[0], gate, ydiff, P['gn'], P['w_out'], P['lng'][1], P['lnb'][1],
                    alpha=alpha, tm=tm)
        x = _ffn(x, mod_l, *P['ffn2'], P['lng'][2], P['lnb'][2], row0=6, alpha=alpha, tm=tm)
        return x, (dk, dv[0] if dv else None, gla_out[1] if not rope else None)

    xp = x_prompt
    new_k, new_v, new_s = [], [], []
    for l in range(depth):
        xp, (k_l, v_l, s_l) = run_layer(xp, mod[l, n_dec:n_dec + 1], layers[l], None)
        new_k.append(k_l)
        new_v.append(v_l)
        bsz = s_l.shape[0]
        s6 = s_l.reshape(bsz, 2, H_GLA, DK_GLA, H_GLA, DK_GLA)
        s_heads = jnp.stack([s6[:, :, h, :, h, :] for h in range(H_GLA)], axis=2)
        new_s.append(jnp.swapaxes(s_heads, -1, -2))

    xs = x_sample
    for l in range(depth):
        st = jnp.swapaxes(state_gla[:, l], -1, -2)
        s0 = jnp.zeros(st.shape[:2] + (H_GLA, DK_GLA, H_GLA, DK_GLA), F32)
        for h in range(H_GLA):
            s0 = s0.at[:, :, h, :, h, :].set(st[:, :, h])
        s0 = s0.reshape(st.shape[0], 2, w_gla, w_gla)
        past = cache_diff_v.shape[3]
        tile = _pick_tile(x_sample.shape[1], 512)
        assert past % tile == 0, "cached context length must be a whole number of key tiles"
        cvt = cache_diff_v[:, l].astype(BF16).reshape(n_dec, H_DIFF, past // tile, tile, LANES)
        ctx = dict(k=cache_diff_k[:, l].astype(BF16), vt=jnp.swapaxes(cvt, -1, -2), s0=s0)
        xs, _ = run_layer(xs, mod[l, :n_dec], layers[l], ctx)

    return (xp, xs, jnp.stack(new_k, axis=1), jnp.stack(new_v, axis=1), jnp.stack(new_s, axis=1))
```

```python
import functools
import math

import jax
import jax.numpy as jnp
from jax import lax
from jax.experimental import pallas as pl
from jax.experimental.pallas import tpu as pltpu

F32 = jnp.float32
BF16 = jnp.bfloat16

GRID_W = 64
CONV_WIDTH = 31
H_GLA = 4
DK_GLA = 64
GLA_RANK = 16
GLA_TAU = 16.0
GLA_CHUNK = 64
H_DIFF = 4
DH_DIFF = 64
ROPE_BASE = 10000.0
LN_EPS = 1e-5
N_MOD = 9

LANES = 128
SUBLANES = 8
V7X_VMEM_BYTES = 64 * 1024 * 1024
VMEM_LIMIT = V7X_VMEM_BYTES - 8 * 1024 * 1024

CONV_HALO = 16
CONV_ROWS = 32
ATTN_STRIP = 32
ROW_STRIP = 32
LN_STRIP = 64


def _cparams(sem):
    return pltpu.CompilerParams(dimension_semantics=sem, vmem_limit_bytes=VMEM_LIMIT)


def _const_spec(shape):
    nd = len(shape)
    return pl.BlockSpec(shape, lambda *_: (0,) * nd, pipeline_mode=pl.Buffered(1))


def _dot(a, b):
    return jnp.dot(a, b, preferred_element_type=F32)


def _dot_nt(a, b):
    return lax.dot_general(a, b, (((1,), (1,)), ((), ())), preferred_element_type=F32)


def _dot_tn(a, b):
    return lax.dot_general(a, b, (((0,), (0,)), ((), ())), preferred_element_type=F32)


def _silu(x):
    return x * jax.nn.sigmoid(x)


def _layer_norm(y, g, b):
    mu = jnp.mean(y, axis=-1, keepdims=True)
    yc = y - mu
    var = jnp.mean(yc * yc, axis=-1, keepdims=True)
    return yc * lax.rsqrt(var + LN_EPS) * g + b


def _modulate(x, mod_ref, row0):
    sh = mod_ref[0, row0:row0 + 1, :]
    sc = mod_ref[0, row0 + 1:row0 + 2, :]
    g = mod_ref[0, row0 + 2:row0 + 3, :]
    return x * (1.0 + sc) + sh, g


def _ada_kernel(c_ref, w_ref, b_ref, o_ref):
    s = _silu(c_ref[...]).astype(BF16)
    o_ref[0] = _dot(s, w_ref[0].astype(BF16)) + b_ref[0]


def _ada(cvec, w_ada, b_ada):
    depth, d, n = w_ada.shape
    rows = cvec.shape[0]
    tn = d
    return pl.pallas_call(
        _ada_kernel,
        grid=(depth, n // tn),
        in_specs=[
            pl.BlockSpec((rows, d), lambda l, j: (0, 0)),
            pl.BlockSpec((1, d, tn), lambda l, j: (l, 0, j)),
            pl.BlockSpec((1, 1, tn), lambda l, j: (l, 0, j)),
        ],
        out_specs=pl.BlockSpec((1, rows, tn), lambda l, j: (l, 0, j)),
        out_shape=jax.ShapeDtypeStruct((depth, rows, n), F32),
        compiler_params=_cparams(("arbitrary", "arbitrary")),
        name="ada_mod",
    )(cvec, w_ada, b_ada.reshape(depth, 1, n))


def _strips(n_rows, fn):
    def body(i, carry):
        fn(pl.ds(pl.multiple_of(i * ROW_STRIP, ROW_STRIP), ROW_STRIP))
        return carry

    lax.fori_loop(0, n_rows // ROW_STRIP, body, 0, unroll=2)


def _layer_norm_rows(n_rows, make_y, o_ref, g, b):
    n = n_rows // LN_STRIP
    sl = lambda i: pl.ds(i * LN_STRIP if isinstance(i, int) else pl.multiple_of(i * LN_STRIP, LN_STRIP), LN_STRIP)

    def means(i):
        y = make_y(sl(i))
        o_ref[sl(i), :] = y
        return jnp.mean(y, axis=-1, keepdims=True)

    def rstd(i, mu):
        yc = o_ref[sl(i), :] - mu
        return lax.rsqrt(jnp.mean(yc * yc, axis=-1, keepdims=True) + LN_EPS)

    def normalise(i, mu, r):
        o_ref[sl(i), :] = (o_ref[sl(i), :] - mu) * r * g + b

    if n == 1:
        mu = means(0)
        normalise(0, mu, rstd(0, mu))
        return
    mu_a = means(0)
    mu_b = means(1)
    r_a = rstd(0, mu_a)

    def body(i, carry):
        mu_a, r_a, mu_b = carry
        r_b = rstd(i - 1, mu_b)
        mu_c = means(i)
        normalise(i - 2, mu_a, r_a)
        return mu_b, r_b, mu_c

    mu_a, r_a, mu_b = lax.fori_loop(2, n, body, (mu_a, r_a, mu_b))
    r_b = rstd(n - 1, mu_b)
    normalise(n - 2, mu_a, r_a)
    normalise(n - 1, mu_b, r_b)


def _ffn_kernel(x_ref, mod_ref, win_ref, wout_ref, lng_ref, lnb_ref, o_ref, h_scr, acc_ref, *, row0, alpha):
    tm = x_ref.shape[0]
    sh = mod_ref[0, row0:row0 + 1, :]
    sc1 = 1.0 + mod_ref[0, row0 + 1:row0 + 2, :]
    half_g = 0.5 * mod_ref[0, row0 + 2:row0 + 3, :]
    n_chunks, _, two_ck = win_ref.shape
    ck = two_ck // 2

    def modulate(rs):
        h_scr[rs, :] = (x_ref[rs, :] * sc1 + sh).astype(BF16)

    _strips(tm, modulate)

    def chunk(j):
        u = _dot(h_scr[...], win_ref[j])
        act = (_silu(u[:, :ck]) * u[:, ck:]).astype(BF16)
        return _dot(act, wout_ref[j])

    acc_ref[...] = chunk(0)

    def body(j, carry):
        acc_ref[...] += chunk(j)
        return carry

    lax.fori_loop(1, n_chunks, body, 0, unroll=2 if n_chunks % 2 else 1)
    _layer_norm_rows(tm, lambda rs: alpha * x_ref[rs, :] + half_g * acc_ref[rs, :], o_ref,
                     lng_ref[...], lnb_ref[...])


def _ffn(x, mod, win, wout, lng, lnb, *, row0, alpha, tm):
    n_tok, d = x.shape
    per_req = n_tok // mod.shape[0]
    kern = functools.partial(_ffn_kernel, row0=row0, alpha=alpha)
    return pl.pallas_call(
        kern,
        grid=(n_tok // tm,),
        in_specs=[
            pl.BlockSpec((tm, d), lambda ti: (ti, 0)),
            pl.BlockSpec((1, N_MOD, d), lambda ti: ((ti * tm) // per_req, 0, 0)),
            _const_spec(win.shape),
            _const_spec(wout.shape),
            _const_spec((1, d)),
            _const_spec((1, d)),
        ],
        out_specs=pl.BlockSpec((tm, d), lambda ti: (ti, 0)),
        out_shape=jax.ShapeDtypeStruct((n_tok, d), F32),
        scratch_shapes=[pltpu.VMEM((tm, d), BF16), pltpu.VMEM((tm, d), F32)],
        compiler_params=_cparams(("parallel",)),
        name="ffn",
    )(x, mod, win, wout, lng, lnb)


def _log_sigmoid(x):
    return jnp.minimum(x, 0.0) - jnp.log(1.0 + jnp.exp(-jnp.abs(x)))


def _rope(x, cos, sin_signed, first_half):
    rot = jnp.where(first_half, pltpu.roll(x, LANES - 16, 1), pltpu.roll(x, 16, 1))
    return x * cos + rot * sin_signed


def _mixin_kernel(*refs, rope, c_conv, w_gla, w_diff):
    if rope:
        (x_ref, mod_ref, w_ref, w2_ref, b2_ref, cos_ref, sin_ref,
         yglu_ref, gq_ref, gk_ref, gv_ref, gate_ref, gdec_ref, dq_ref, dk_ref, dvt_ref) = refs
    else:
        (x_ref, mod_ref, w_ref, w2_ref, b2_ref,
         yglu_ref, gq_ref, gk_ref, gv_ref, gate_ref, gdec_ref, dq_ref, dk_ref, dvt_ref, dv_ref) = refs
    x = x_ref[0]
    h, _ = _modulate(x, mod_ref, 3)
    h = h.astype(BF16)
    o_gla = 2 * c_conv
    o_diff = o_gla + 4 * w_gla
    o_lr = o_diff + 3 * w_diff

    zc = _dot(h, w_ref[:, 0:o_gla])
    yglu_ref[0] = zc[:, :c_conv] * jax.nn.sigmoid(zc[:, c_conv:])

    zg = _dot(h, w_ref[:, o_gla:o_diff])
    gq_ref[0] = zg[:, 0:w_gla] * (DK_GLA ** -0.5)
    gk_ref[0] = zg[:, w_gla:2 * w_gla]
    gv_ref[0] = zg[:, 2 * w_gla:3 * w_gla].astype(BF16)
    gate_ref[0] = zg[:, 3 * w_gla:4 * w_gla]

    zl = _dot(h, w_ref[:, o_lr:o_lr + LANES]).astype(BF16)
    pre = _dot(zl, w2_ref[...]) + b2_ref[...]
    gdec_ref[0] = _log_sigmoid(pre) * (1.0 / GLA_TAU)

    zd = _dot(h, w_ref[:, o_diff:o_lr])
    if rope:
        cos = cos_ref[...]
        sin = sin_ref[...]
        lane = lax.broadcasted_iota(jnp.int32, cos.shape, 1)
        first_half = (lane % 32) < 16
    for hd in range(H_DIFF):
        sl = slice(hd * LANES, (hd + 1) * LANES)
        q = zd[:, sl]
        k = zd[:, w_diff + hd * LANES:w_diff + (hd + 1) * LANES]
        v = zd[:, 2 * w_diff + hd * LANES:2 * w_diff + (hd + 1) * LANES]
        if rope:
            q = _rope(q, cos, sin, first_half)
            k = _rope(k, cos, sin, first_half)
        dq_ref[0, :, sl] = (q * (DH_DIFF ** -0.5 * math.log2(math.e))).astype(BF16)
        dk_ref[0, hd] = k.astype(dk_ref.dtype)
        dvt_ref[0, hd, 0] = v.T.astype(BF16)
        if not rope:
            dv_ref[0, hd] = v


def _mixin(x, mod, w, w2, b2, cos, sin, *, tm):
    b, t, d = x.shape
    rope = cos is not None
    c_conv = 256
    w_gla = H_GLA * DK_GLA
    w_diff = H_DIFF * 2 * DH_DIFF
    kv_dtype = BF16 if rope else F32
    mod_rows = mod.shape[0]
    mod_idx = (lambda bi, ti: (bi, 0, 0)) if mod_rows > 1 else (lambda bi, ti: (0, 0, 0))
    tok = lambda n: pl.BlockSpec((1, tm, n), lambda bi, ti: (bi, ti, 0))
    heads = pl.BlockSpec((1, H_DIFF, tm, LANES), lambda bi, ti: (bi, 0, ti, 0))
    in_specs = [tok(d), pl.BlockSpec((1, N_MOD, d), mod_idx),
                _const_spec(w.shape), _const_spec(w2.shape), _const_spec(b2.shape)]
    args = [x, mod, w, w2, b2]
    if rope:
        in_specs += [pl.BlockSpec((tm, LANES), lambda bi, ti: (ti, 0))] * 2
        args += [cos, sin]
    sds = jax.ShapeDtypeStruct
    out_shape = [
        sds((b, t, c_conv), F32),
        sds((b, t, w_gla), F32),
        sds((b, t, w_gla), F32),
        sds((b, t, w_gla), BF16),
        sds((b, t, w_gla), F32),
        sds((b, t, 2 * w_gla), F32),
        sds((b, t, w_diff), BF16),
        sds((b, H_DIFF, t, LANES), kv_dtype),
        sds((b, H_DIFF, t // tm, LANES, tm), BF16),
    ]
    out_specs = [tok(c_conv), tok(w_gla), tok(w_gla), tok(w_gla), tok(w_gla), tok(2 * w_gla),
                 tok(w_diff), heads,
                 pl.BlockSpec((1, H_DIFF, 1, LANES, tm), lambda bi, ti: (bi, 0, ti, 0, 0))]
    if not rope:
        out_shape.append(sds((b, H_DIFF, t, LANES), F32))
        out_specs.append(heads)
    kern = functools.partial(_mixin_kernel, rope=rope, c_conv=c_conv, w_gla=w_gla, w_diff=w_diff)
    return pl.pallas_call(
        kern,
        grid=(b, t // tm),
        in_specs=in_specs,
        out_specs=out_specs,
        out_shape=out_shape,
        compiler_params=_cparams(("parallel", "parallel")),
        name="mixer_in",
    )(*args)


def _conv_kernel(prev_ref, cur_ref, next_ref, w_ref, b_ref, g_ref, beta_ref, o_ref, win_ref, *, tc, nt):
    i = pl.program_id(1)
    pad = CONV_WIDTH // 2
    win_ref[0:CONV_HALO] = jnp.where(i > 0, prev_ref[0], 0.0)
    win_ref[CONV_HALO:CONV_HALO + tc] = cur_ref[0]
    win_ref[CONV_HALO + tc:2 * CONV_HALO + tc] = jnp.where(i < nt - 1, next_ref[0], 0.0)
    w = w_ref[...]
    for r in range(0, tc, CONV_ROWS):
        acc = jnp.zeros((CONV_ROWS, w.shape[1]), F32)
        for k in range(CONV_WIDTH):
            start = CONV_HALO - pad + r + k
            acc = acc + win_ref[start:start + CONV_ROWS, :] * w[k:k + 1, :]
        y = _layer_norm(acc + b_ref[...], g_ref[...], beta_ref[...])
        o_ref[0, r:r + CONV_ROWS, :] = _silu(y).astype(o_ref.dtype)


def _conv(yglu, w, bias, g, beta, *, tc):
    b, t, c = yglu.shape
    nt = t // tc
    hb = tc // CONV_HALO
    n_halo = t // CONV_HALO
    kern = functools.partial(_conv_kernel, tc=tc, nt=nt)
    return pl.pallas_call(
        kern,
        grid=(b, nt),
        in_specs=[
            pl.BlockSpec((1, CONV_HALO, c), lambda bi, i: (bi, jnp.maximum(i * hb - 1, 0), 0)),
            pl.BlockSpec((1, tc, c), lambda bi, i: (bi, i, 0)),
            pl.BlockSpec((1, CONV_HALO, c), lambda bi, i: (bi, jnp.minimum((i + 1) * hb, n_halo - 1), 0)),
            _const_spec(w.shape), _const_spec((1, c)), _const_spec((1, c)), _const_spec((1, c)),
        ],
        out_specs=pl.BlockSpec((1, tc, c), lambda bi, i: (bi, i, 0)),
        out_shape=jax.ShapeDtypeStruct((b, t, c), BF16),
        scratch_shapes=[pltpu.VMEM((tc + 2 * CONV_HALO, c), F32)],
        compiler_params=_cparams(("parallel", "parallel")),
        name="conv_module",
    )(yglu, yglu, yglu, w, bias, g, beta)


def _split3(x):
    hi = x.astype(BF16)
    r1 = x - hi.astype(F32)
    mid = r1.astype(BF16)
    lo = (r1 - mid.astype(F32)).astype(BF16)
    return hi, mid, lo


def _gla_kernel(*refs, nc, has_state, want_state):
    it = iter(refs)
    q_ref, k_ref, v_ref, g_ref = next(it), next(it), next(it), next(it)
    s0_ref = next(it) if has_state else None
    o_ref = next(it)
    sfin_ref = next(it) if want_state else None
    s_scr = next(it)

    d = pl.program_id(1)
    i = pl.program_id(2)
    nblk = pl.num_programs(2)
    c = GLA_CHUNK
    width = s_scr.shape[0]
    nh = width // c

    @pl.when(i == 0)
    def _():
        if has_state:
            s_scr[...] = s0_ref[0, 0]
        else:
            s_scr[...] = jnp.zeros_like(s_scr)

    fwd = d == 0
    row = lax.broadcasted_iota(jnp.int32, (c, c), 0)
    col = lax.broadcasted_iota(jnp.int32, (c, c), 1)
    sgn = jnp.where(fwd, 1, -1)
    tri = (row - col) * sgn >= 0
    tri_b = jnp.where(tri, 1.0, 0.0).astype(BF16)
    rowa = lax.broadcasted_iota(jnp.int32, (c, width), 0)
    cola = lax.broadcasted_iota(jnp.int32, (c, width), 1) % c
    tri_cat = (rowa - cola) * sgn >= 0
    rb = lax.broadcasted_iota(jnp.int32, (width, width), 0) // c
    cb = lax.broadcasted_iota(jnp.int32, (width, width), 1) // c
    bd = rb == cb

    def chunk(j, carry):
        cidx = jnp.where(fwd, j, nc - 1 - j)
        off = pl.multiple_of(cidx * c, c)
        q = q_ref[0, pl.ds(off, c), :]
        k = k_ref[0, pl.ds(off, c), :]
        v = v_ref[0, pl.ds(off, c), :]
        g = g_ref[0, pl.ds(off, c), :]
        ghi, gmid, glo = _split3(g)
        gcum = _dot(tri_b, ghi) + _dot(tri_b, gmid) + _dot(tri_b, glo)
        gtot = jnp.where(fwd, gcum[c - 1:c, :], gcum[0:1, :])
        q_t = (q * jnp.exp(gcum)).astype(BF16)
        k_t = k * jnp.exp(-gcum)
        k_hat = (k * jnp.exp(gtot - gcum)).astype(BF16)
        decay = jnp.exp(gtot)
        s_in = s_scr[...]
        k_bd = jnp.where(bd, jnp.concatenate([k_t] * nh, axis=0), 0.0).astype(BF16)
        a = _dot_nt(q_t, k_bd)
        a = jnp.where(tri_cat, a, 0.0).astype(BF16)
        v_bd = jnp.where(bd, jnp.concatenate([v] * nh, axis=0), jnp.zeros((), v.dtype))
        o = _dot(a, v_bd) + _dot_nt(q_t, s_in.astype(BF16))
        o_ref[0, 0, pl.ds(off, c), :] = o
        kv_t = _dot_tn(v, k_hat)
        s_scr[...] = s_in * decay + jnp.where(bd, kv_t, 0.0)
        return carry

    lax.fori_loop(0, nc, chunk, 0)

    if want_state:
        @pl.when(i == nblk - 1)
        def _():
            sfin_ref[0, 0] = s_scr[...]


def _gla(q, k, v, gdec, s0, *, tb, want_state):
    b, t, w = q.shape
    nblk = t // tb
    nc = tb // GLA_CHUNK
    blk = lambda d, i: i + d * (nblk - 1 - 2 * i)
    tok = pl.BlockSpec((1, tb, w), lambda bi, d, i: (bi, blk(d, i), 0))
    in_specs = [tok, tok, tok, pl.BlockSpec((1, tb, w), lambda bi, d, i: (bi, blk(d, i), d))]
    args = [q, k, v, gdec]
    has_state = s0 is not None
    if has_state:
        in_specs.append(pl.BlockSpec((1, 1, w, w), lambda bi, d, i: (bi, d, 0, 0)))
        args.append(s0)
    out_shape = [jax.ShapeDtypeStruct((2, b, t, w), F32)]
    out_specs = [pl.BlockSpec((1, 1, tb, w), lambda bi, d, i: (d, bi, blk(d, i), 0))]
    if want_state:
        out_shape.append(jax.ShapeDtypeStruct((b, 2, w, w), F32))
        out_specs.append(pl.BlockSpec((1, 1, w, w), lambda bi, d, i: (bi, d, 0, 0)))
    kern = functools.partial(_gla_kernel, nc=nc, has_state=has_state, want_state=want_state)
    return pl.pallas_call(
        kern,
        grid=(b, 2, nblk),
        in_specs=in_specs,
        out_specs=out_specs,
        out_shape=out_shape,
        scratch_shapes=[pltpu.VMEM((w, w), F32)],
        compiler_params=_cparams(("parallel", "parallel", "arbitrary")),
        name="gla",
    )(*args)


def _attn_kernel(q_ref, k_ref, vt_ref, lam_ref, g_ref, o_ref, s_scr, e_scr, l_scr, acc_scr, *, lam_init):
    p = lam_ref[...]
    lam = (jnp.exp(jnp.sum(p[0:1] * p[1:2], axis=-1, keepdims=True))
           - jnp.exp(jnp.sum(p[2:3] * p[3:4], axis=-1, keepdims=True)) + lam_init)

    q = q_ref[0]
    tq = q.shape[0]
    lane = lax.broadcasted_iota(jnp.int32, q.shape, 1)
    zero = jnp.zeros((), q.dtype)
    qs = jnp.concatenate([jnp.where(lane < DH_DIFF, q, zero), jnp.where(lane >= DH_DIFF, q, zero)], axis=0)

    n, tk = vt_ref.shape[2], vt_ref.shape[4]
    nq = 2 * tq
    l_scr[...] = jnp.zeros_like(l_scr)
    acc_scr[...] = jnp.zeros_like(acc_scr)

    def scores(j, slot):
        off = j * tk if isinstance(j, int) else pl.multiple_of(j * tk, tk)
        s = _dot_nt(k_ref[0, 0, pl.ds(off, tk), :].astype(BF16), qs)
        s_scr[slot] = s
        return jnp.max(s, axis=0, keepdims=True)

    def softmax(slot, mt, m):
        m_new = jnp.maximum(m, mt)
        alpha = jnp.exp2(m - m_new)
        lt = jnp.zeros((SUBLANES, nq), F32)
        for r in range(0, tk, ATTN_STRIP):
            e = jnp.exp2(s_scr[slot, r:r + ATTN_STRIP, :] - m_new)
            lt = lt + functools.reduce(lambda a, b: a + b,
                                       [e[i:i + SUBLANES] for i in range(0, ATTN_STRIP, SUBLANES)])
            e_scr[slot, r:r + ATTN_STRIP, :] = e.astype(BF16)
        l_scr[...] = l_scr[...] * alpha + lt
        return m_new, alpha

    def weighted(j, slot, alpha):
        acc_scr[...] = acc_scr[...] * alpha + _dot(vt_ref[0, 0, j], e_scr[slot])

    def step(j, slot, mt, m, alpha_prev, last=False):
        mt_next = mt if last else scores(j + 1, 1 - slot)
        m, alpha = softmax(slot, mt, m)
        weighted(j - 1, 1 - slot, alpha_prev)
        return mt_next, m, alpha

    m = jnp.full((1, nq), -1e30, F32)
    mt = scores(0, 0)
    if n == 1:
        m, alpha = softmax(0, mt, m)
    else:
        mt_next = scores(1, 1)
        m, alpha = softmax(0, mt, m)
        mt = mt_next
        for j in range(1, n - 1):
            mt, m, alpha = step(j, j % 2, mt, m, alpha)
        mt, m, alpha = step(n - 1, (n - 1) % 2, mt, m, alpha, last=True)
    weighted(n - 1, (n - 1) % 2, alpha)
    on = acc_scr[...] / jnp.sum(l_scr[...], axis=0, keepdims=True)
    o = on[:, :tq] - lam * on[:, tq:]
    ms = jnp.mean(o * o, axis=0, keepdims=True)
    y = o * lax.rsqrt(ms + LN_EPS) * g_ref[...] * (1.0 - lam_init)
    o_ref[0] = y.T.astype(o_ref.dtype)


def _attn(dq, dk, dvt, lam_p, g_col, *, lam_init, tq):
    b, t, _ = dq.shape
    _, _, n_tiles, dv, tk = dvt.shape
    qspec = pl.BlockSpec((1, tq, LANES), lambda bi, h, qi: (bi, qi, h))
    kspec = pl.BlockSpec((1, 1, n_tiles * tk, LANES), lambda bi, h, qi: (bi, h, 0, 0))
    vspec = pl.BlockSpec((1, 1, n_tiles, dv, tk), lambda bi, h, qi: (bi, h, 0, 0, 0))
    kern = functools.partial(_attn_kernel, lam_init=lam_init)
    return pl.pallas_call(
        kern,
        grid=(b, H_DIFF, t // tq),
        in_specs=[qspec, kspec, vspec, _const_spec(lam_p.shape), _const_spec(g_col.shape)],
        out_specs=qspec,
        out_shape=jax.ShapeDtypeStruct(dq.shape, BF16),
        scratch_shapes=[pltpu.VMEM((2, tk, 2 * tq), F32), pltpu.VMEM((2, tk, 2 * tq), BF16),
                        pltpu.VMEM((SUBLANES, 2 * tq), F32), pltpu.VMEM((dv, 2 * tq), F32)],
        compiler_params=_cparams(("parallel", "parallel", "parallel")),
        name="diff_attn",
    )(dq, dk, dvt, lam_p, g_col)


def _mixout_kernel(x_ref, mod_ref, yc_ref, of_ref, ob_ref, gate_ref, yd_ref, gn_ref, w_ref,
                   lng_ref, lnb_ref, o_ref, *, alpha):
    x = x_ref[0]
    _, g1 = _modulate(x, mod_ref, 3)
    o = of_ref[0, 0] + ob_ref[0, 0]
    width = o.shape[1]
    rb = lax.broadcasted_iota(jnp.int32, (width, width), 0) // DK_GLA
    cb = lax.broadcasted_iota(jnp.int32, (width, width), 1) // DK_GLA
    ones_bd = jnp.where(rb == cb, 1.0, 0.0).astype(BF16)
    sq = o * o
    hi = sq.astype(BF16)
    lo = (sq - hi.astype(F32)).astype(BF16)
    ms = (_dot(hi, ones_bd) + _dot(lo, ones_bd)) * (1.0 / DK_GLA)
    yg = (o * lax.rsqrt(ms + LN_EPS) * gn_ref[...] * _silu(gate_ref[0])).astype(BF16)
    c_conv = yc_ref.shape[2]
    y = (_dot(yc_ref[0], w_ref[0:c_conv, :])
         + _dot(yg, w_ref[c_conv:c_conv + width, :])
         + _dot(yd_ref[0], w_ref[c_conv + width:, :]))
    o_ref[0] = _layer_norm(alpha * x + g1 * y, lng_ref[...], lnb_ref[...])


def _mixout(x, mod, yconv, o_gla, gate, ydiff, gn, w, lng, lnb, *, alpha, tm):
    b, t, d = x.shape
    mod_rows = mod.shape[0]
    mod_idx = (lambda bi, ti: (bi, 0, 0)) if mod_rows > 1 else (lambda bi, ti: (0, 0, 0))
    tok = lambda n: pl.BlockSpec((1, tm, n), lambda bi, ti: (bi, ti, 0))
    wg = o_gla.shape[3]
    kern = functools.partial(_mixout_kernel, alpha=alpha)
    return pl.pallas_call(
        kern,
        grid=(b, t // tm),
        in_specs=[
            tok(d), pl.BlockSpec((1, N_MOD, d), mod_idx), tok(yconv.shape[2]),
            pl.BlockSpec((1, 1, tm, wg), lambda bi, ti: (0, bi, ti, 0)),
            pl.BlockSpec((1, 1, tm, wg), lambda bi, ti: (1, bi, ti, 0)),
            tok(wg), tok(ydiff.shape[2]),
            _const_spec(gn.shape), _const_spec(w.shape), _const_spec((1, d)), _const_spec((1, d)),
        ],
        out_specs=tok(d),
        out_shape=jax.ShapeDtypeStruct((b, t, d), F32),
        compiler_params=_cparams(("parallel", "parallel")),
        name="mixer_out",
    )(x, mod, yconv, o_gla, o_gla, gate, ydiff, gn, w, lng, lnb)


def _rope_tables(t):
    rows = t // GRID_W
    row = jnp.repeat(jnp.arange(rows, dtype=F32), GRID_W)
    col = jnp.tile(jnp.arange(GRID_W, dtype=F32), rows)
    seg = DH_DIFF // 2
    inv = ROPE_BASE ** (-jnp.arange(0, seg, 2, dtype=F32) / seg)
    a_r = row[:, None] * inv
    a_c = col[:, None] * inv
    ang = jnp.concatenate([a_r, a_r, a_c, a_c], axis=-1)
    ang = jnp.concatenate([ang, ang], axis=-1)
    sign = jnp.where((jnp.arange(LANES) % 32) < 16, -1.0, 1.0).astype(F32)
    return jnp.cos(ang), jnp.sin(ang) * sign


def _ffn_weights(w_in, w_out, ck):
    d, two_ff = w_in.shape
    ff = two_ff // 2
    n = ff // ck
    win = w_in.reshape(d, 2, n, ck).transpose(2, 0, 1, 3).reshape(n, d, 2 * ck).astype(BF16)
    wout = w_out.reshape(n, ck, d).astype(BF16)
    return win, wout


def _pick_tile(t, pref):
    return pref if t % pref == 0 else t


def kernel(x_prompt, x_sample, cache_diff_k, cache_diff_v, state_gla, c, c_ctx, w_ada, b_ada, w_ffn1_in,
           w_ffn1_out, w_ffn2_in, w_ffn2_out, w_in, conv_w, conv_b, conv_ln_g, conv_ln_b, gla_w_a2, gla_b_a,
           gla_norm_g, diff_lam, diff_norm_g, w_out, ln_g, ln_b):
    depth, d, _ = w_ada.shape
    alpha = (2 * depth) ** 0.25
    n_dec = c.shape[0]
    c_conv = conv_w.shape[2]
    w_gla = H_GLA * DK_GLA
    w_diff = H_DIFF * 2 * DH_DIFF
    in_conv, in_gla = 2 * c_conv, 4 * w_gla + 2 * GLA_RANK
    ff_chunk = 256

    rows = -(-(n_dec + 1) // SUBLANES) * SUBLANES
    cvec = jnp.zeros((rows, d), F32).at[:n_dec].set(c).at[n_dec].set(c_ctx)
    mod = _ada(cvec, w_ada, b_ada).reshape(depth, rows, N_MOD, d)

    cos, sin = _rope_tables(x_sample.shape[1])

    layers = []
    for l in range(depth):
        wi = w_in[l]
        w_lr = jnp.pad(wi[:, in_conv + 4 * w_gla:in_conv + in_gla], ((0, 0), (0, LANES - 2 * GLA_RANK)))
        w_all = jnp.concatenate([wi[:, :in_conv + 4 * w_gla], wi[:, in_conv + in_gla:], w_lr], axis=1).astype(BF16)
        w2 = jnp.zeros((LANES, 2 * w_gla), F32)
        w2 = w2.at[:GLA_RANK, :w_gla].set(gla_w_a2[l, 0]).at[GLA_RANK:2 * GLA_RANK, w_gla:].set(gla_w_a2[l, 1])
        layers.append(dict(
            ffn1=_ffn_weights(w_ffn1_in[l], w_ffn1_out[l], ff_chunk),
            ffn2=_ffn_weights(w_ffn2_in[l], w_ffn2_out[l], ff_chunk),
            w_all=w_all, w2=w2.astype(BF16), b2=gla_b_a[l].reshape(1, 2 * w_gla),
            conv_w=conv_w[l], conv_b=conv_b[l][None], conv_g=conv_ln_g[l][None], conv_beta=conv_ln_b[l][None],
            gn=jnp.tile(gla_norm_g[l], H_GLA)[None], lam=diff_lam[l], dg=diff_norm_g[l][:, None],
            w_out=w_out[l].astype(BF16),
            lng=[ln_g[l, i][None] for i in range(3)], lnb=[ln_b[l, i][None] for i in range(3)],
            lam_init=0.8 - 0.6 * math.exp(-0.3 * l),
        ))

    def run_layer(x, mod_l, P, ctx):
        t = x.shape[1]
        tm = _pick_tile(t, 512)
        bsz, _, d_model = x.shape
        shared_mod = mod_l.shape[0] == 1
        tf = _pick_tile(bsz * t if shared_mod else t, 1024)

        def ffn(x, which, row0, ln_idx):
            y = _ffn(x.reshape(bsz * t, d_model), mod_l, *P[which], P['lng'][ln_idx], P['lnb'][ln_idx],
                     row0=row0, alpha=alpha, tm=tf)
            return y.reshape(bsz, t, d_model)

        x = ffn(x, 'ffn1', 0, 0)
        rope = ctx is not None
        yglu, gq, gk, gv, gate, gdec, dq, dk, dvt, *dv = _mixin(
            x, mod_l, P['w_all'], P['w2'], P['b2'], cos if rope else None, sin if rope else None, tm=tm)
        yconv = _conv(yglu, P['conv_w'], P['conv_b'], P['conv_g'], P['conv_beta'], tc=_pick_tile(t, 256))
        gla_out = _gla(gq, gk, gv, gdec, ctx['s0'] if rope else None, tb=_pick_tile(t, 512), want_state=not rope)
        keys, vals_t = dk, dvt
        if rope:
            keys = jnp.concatenate([dk, ctx['k']], axis=2)
            vals_t = jnp.concatenate([dvt, ctx['vt']], axis=2)
        ydiff = _attn(dq, keys, vals_t, P['lam'], P['dg'], lam_init=P['lam_init'], tq=_pick_tile(t, 512))
        x = _mixout(x, mod_l, yconv, gla_out[0], gate, ydiff, P['gn'], P['w_out'], P['lng'][1], P['lnb'][1],
                    alpha=alpha, tm=tm)
        x = ffn(x, 'ffn2', 6, 2)
        return x, (dk, dv[0] if dv else None, gla_out[1] if not rope else None)

    xp = x_prompt
    new_k, new_v, new_s = [], [], []
    for l in range(depth):
        xp, (k_l, v_l, s_l) = run_layer(xp, mod[l, n_dec:n_dec + 1], layers[l], None)
        new_k.append(k_l)
        new_v.append(v_l)
        bsz = s_l.shape[0]
        s6 = s_l.reshape(bsz, 2, H_GLA, DK_GLA, H_GLA, DK_GLA)
        s_heads = jnp.stack([s6[:, :, h, :, h, :] for h in range(H_GLA)], axis=2)
        new_s.append(jnp.swapaxes(s_heads, -1, -2))

    xs = x_sample
    for l in range(depth):
        st = jnp.swapaxes(state_gla[:, l], -1, -2)
        s0 = jnp.zeros(st.shape[:2] + (H_GLA, DK_GLA, H_GLA, DK_GLA), F32)
        for h in range(H_GLA):
            s0 = s0.at[:, :, h, :, h, :].set(st[:, :, h])
        s0 = s0.reshape(st.shape[0], 2, w_gla, w_gla)
        past = cache_diff_v.shape[3]
        tile = _pick_tile(x_sample.shape[1], 512)
        assert past % tile == 0, "cached context length must be a whole number of key tiles"
        cvt = cache_diff_v[:, l].astype(BF16).reshape(n_dec, H_DIFF, past // tile, tile, LANES)
        ctx = dict(k=cache_diff_k[:, l].astype(BF16), vt=jnp.swapaxes(cvt, -1, -2), s0=s0)
        xs, _ = run_layer(xs, mod[l, :n_dec], layers[l], ctx)

    return (xp, xs, jnp.stack(new_k, axis=1), jnp.stack(new_v, axis=1), jnp.stack(new_s, axis=1))
```

```python
import functools
import math

import jax
import jax.numpy as jnp
from jax import lax
from jax.experimental import pallas as pl
from jax.experimental.pallas import tpu as pltpu

F32 = jnp.float32
BF16 = jnp.bfloat16

GRID_W = 64
CONV_WIDTH = 31
H_GLA = 4
DK_GLA = 64
GLA_RANK = 16
GLA_TAU = 16.0
GLA_CHUNK = 64
H_DIFF = 4
DH_DIFF = 64
ROPE_BASE = 10000.0
LN_EPS = 1e-5
N_MOD = 9

LANES = 128
SUBLANES = 8
V7X_VMEM_BYTES = 64 * 1024 * 1024
VMEM_LIMIT = V7X_VMEM_BYTES - 8 * 1024 * 1024

CONV_HALO = 16
CONV_ROWS = 32
ATTN_STRIP = 32
ROW_STRIP = 32
LN_STRIP = 64


def _cparams(sem):
    return pltpu.CompilerParams(dimension_semantics=sem, vmem_limit_bytes=VMEM_LIMIT)


def _const_spec(shape):
    nd = len(shape)
    return pl.BlockSpec(shape, lambda *_: (0,) * nd, pipeline_mode=pl.Buffered(1))


def _dot(a, b):
    return jnp.dot(a, b, preferred_element_type=F32)


def _dot_nt(a, b):
    return lax.dot_general(a, b, (((1,), (1,)), ((), ())), preferred_element_type=F32)


def _dot_tn(a, b):
    return lax.dot_general(a, b, (((0,), (0,)), ((), ())), preferred_element_type=F32)


def _silu(x):
    return x * jax.nn.sigmoid(x)


def _layer_norm(y, g, b):
    mu = jnp.mean(y, axis=-1, keepdims=True)
    yc = y - mu
    var = jnp.mean(yc * yc, axis=-1, keepdims=True)
    return yc * lax.rsqrt(var + LN_EPS) * g + b


def _modulate(x, mod_ref, row0):
    sh = mod_ref[0, row0:row0 + 1, :]
    sc = mod_ref[0, row0 + 1:row0 + 2, :]
    g = mod_ref[0, row0 + 2:row0 + 3, :]
    return x * (1.0 + sc) + sh, g


def _ada_kernel(c_ref, w_ref, b_ref, o_ref):
    s = _silu(c_ref[...]).astype(BF16)
    o_ref[0] = _dot(s, w_ref[0].astype(BF16)) + b_ref[0]


def _ada(cvec, w_ada, b_ada):
    depth, d, n = w_ada.shape
    rows = cvec.shape[0]
    tn = d
    return pl.pallas_call(
        _ada_kernel,
        grid=(depth, n // tn),
        in_specs=[
            pl.BlockSpec((rows, d), lambda l, j: (0, 0)),
            pl.BlockSpec((1, d, tn), lambda l, j: (l, 0, j)),
            pl.BlockSpec((1, 1, tn), lambda l, j: (l, 0, j)),
        ],
        out_specs=pl.BlockSpec((1, rows, tn), lambda l, j: (l, 0, j)),
        out_shape=jax.ShapeDtypeStruct((depth, rows, n), F32),
        compiler_params=_cparams(("arbitrary", "arbitrary")),
        name="ada_mod",
    )(cvec, w_ada, b_ada.reshape(depth, 1, n))


def _strips(n_rows, fn):
    def body(i, carry):
        fn(pl.ds(pl.multiple_of(i * ROW_STRIP, ROW_STRIP), ROW_STRIP))
        return carry

    lax.fori_loop(0, n_rows // ROW_STRIP, body, 0, unroll=2)


def _layer_norm_rows(n_rows, make_y, o_ref, g, b):
    n = n_rows // LN_STRIP
    sl = lambda i: pl.ds(i * LN_STRIP if isinstance(i, int) else pl.multiple_of(i * LN_STRIP, LN_STRIP), LN_STRIP)

    def means(i):
        y = make_y(sl(i))
        o_ref[sl(i), :] = y
        return jnp.mean(y, axis=-1, keepdims=True)

    def rstd(i, mu):
        yc = o_ref[sl(i), :] - mu
        return lax.rsqrt(jnp.mean(yc * yc, axis=-1, keepdims=True) + LN_EPS)

    def normalise(i, mu, r):
        o_ref[sl(i), :] = (o_ref[sl(i), :] - mu) * r * g + b

    if n == 1:
        mu = means(0)
        normalise(0, mu, rstd(0, mu))
        return
    mu_a = means(0)
    mu_b = means(1)
    r_a = rstd(0, mu_a)

    def body(i, carry):
        mu_a, r_a, mu_b = carry
        r_b = rstd(i - 1, mu_b)
        mu_c = means(i)
        normalise(i - 2, mu_a, r_a)
        return mu_b, r_b, mu_c

    mu_a, r_a, mu_b = lax.fori_loop(2, n, body, (mu_a, r_a, mu_b))
    r_b = rstd(n - 1, mu_b)
    normalise(n - 2, mu_a, r_a)
    normalise(n - 1, mu_b, r_b)


def _ffn_kernel(x_ref, mod_ref, win_ref, wout_ref, lng_ref, lnb_ref, o_ref, h_scr, acc_ref, *, row0, alpha):
    tm = x_ref.shape[0]
    sh = mod_ref[0, row0:row0 + 1, :]
    sc1 = 1.0 + mod_ref[0, row0 + 1:row0 + 2, :]
    half_g = 0.5 * mod_ref[0, row0 + 2:row0 + 3, :]
    n_chunks, _, two_ck = win_ref.shape
    ck = two_ck // 2

    def modulate(rs):
        h_scr[rs, :] = (x_ref[rs, :] * sc1 + sh).astype(BF16)

    _strips(tm, modulate)

    def chunk(j):
        u = _dot(h_scr[...], win_ref[j])
        act = (_silu(u[:, :ck]) * u[:, ck:]).astype(BF16)
        return _dot(act, wout_ref[j])

    acc_ref[...] = chunk(0)

    def body(j, carry):
        acc_ref[...] += chunk(j)
        return carry

    lax.fori_loop(1, n_chunks, body, 0, unroll=2 if n_chunks % 2 else 1)
    _layer_norm_rows(tm, lambda rs: alpha * x_ref[rs, :] + half_g * acc_ref[rs, :], o_ref,
                     lng_ref[...], lnb_ref[...])


def _ffn(x, mod, win, wout, lng, lnb, *, row0, alpha, tm):
    n_tok, d = x.shape
    per_req = n_tok // mod.shape[0]
    kern = functools.partial(_ffn_kernel, row0=row0, alpha=alpha)
    return pl.pallas_call(
        kern,
        grid=(n_tok // tm,),
        in_specs=[
            pl.BlockSpec((tm, d), lambda ti: (ti, 0)),
            pl.BlockSpec((1, N_MOD, d), lambda ti: ((ti * tm) // per_req, 0, 0)),
            _const_spec(win.shape),
            _const_spec(wout.shape),
            _const_spec((1, d)),
            _const_spec((1, d)),
        ],
        out_specs=pl.BlockSpec((tm, d), lambda ti: (ti, 0)),
        out_shape=jax.ShapeDtypeStruct((n_tok, d), F32),
        scratch_shapes=[pltpu.VMEM((tm, d), BF16), pltpu.VMEM((tm, d), F32)],
        compiler_params=_cparams(("parallel",)),
        name="ffn",
    )(x, mod, win, wout, lng, lnb)


def _log_sigmoid(x):
    return jnp.minimum(x, 0.0) - jnp.log(1.0 + jnp.exp(-jnp.abs(x)))


def _rope(x, cos, sin_signed, first_half):
    rot = jnp.where(first_half, pltpu.roll(x, LANES - 16, 1), pltpu.roll(x, 16, 1))
    return x * cos + rot * sin_signed


def _mixin_kernel(*refs, rope, c_conv, w_gla, w_diff):
    if rope:
        (x_ref, mod_ref, w_ref, w2_ref, b2_ref, cos_ref, sin_ref,
         yglu_ref, gq_ref, gk_ref, gv_ref, gate_ref, gdec_ref, dq_ref, dk_ref, dvt_ref) = refs
    else:
        (x_ref, mod_ref, w_ref, w2_ref, b2_ref,
         yglu_ref, gq_ref, gk_ref, gv_ref, gate_ref, gdec_ref, dq_ref, dk_ref, dvt_ref, dv_ref) = refs
    x = x_ref[0]
    h, _ = _modulate(x, mod_ref, 3)
    h = h.astype(BF16)
    o_gla = 2 * c_conv
    o_diff = o_gla + 4 * w_gla
    o_lr = o_diff + 3 * w_diff

    zc = _dot(h, w_ref[:, 0:o_gla])
    yglu_ref[0] = zc[:, :c_conv] * jax.nn.sigmoid(zc[:, c_conv:])

    zg = _dot(h, w_ref[:, o_gla:o_diff])
    gq_ref[0] = zg[:, 0:w_gla] * (DK_GLA ** -0.5)
    gk_ref[0] = zg[:, w_gla:2 * w_gla]
    gv_ref[0] = zg[:, 2 * w_gla:3 * w_gla].astype(BF16)
    gate_ref[0] = zg[:, 3 * w_gla:4 * w_gla]

    zl = _dot(h, w_ref[:, o_lr:o_lr + LANES]).astype(BF16)
    pre = _dot(zl, w2_ref[...]) + b2_ref[...]
    gdec_ref[0] = _log_sigmoid(pre) * (1.0 / GLA_TAU)

    zd = _dot(h, w_ref[:, o_diff:o_lr])
    if rope:
        cos = cos_ref[...]
        sin = sin_ref[...]
        lane = lax.broadcasted_iota(jnp.int32, cos.shape, 1)
        first_half = (lane % 32) < 16
    for hd in range(H_DIFF):
        sl = slice(hd * LANES, (hd + 1) * LANES)
        q = zd[:, sl]
        k = zd[:, w_diff + hd * LANES:w_diff + (hd + 1) * LANES]
        v = zd[:, 2 * w_diff + hd * LANES:2 * w_diff + (hd + 1) * LANES]
        if rope:
            q = _rope(q, cos, sin, first_half)
            k = _rope(k, cos, sin, first_half)
        dq_ref[0, :, sl] = (q * (DH_DIFF ** -0.5 * math.log2(math.e))).astype(BF16)
        dk_ref[0, hd] = k.astype(dk_ref.dtype)
        dvt_ref[0, hd, 0] = v.T.astype(BF16)
        if not rope:
            dv_ref[0, hd] = v


def _mixin(x, mod, w, w2, b2, cos, sin, *, tm):
    b, t, d = x.shape
    rope = cos is not None
    c_conv = 256
    w_gla = H_GLA * DK_GLA
    w_diff = H_DIFF * 2 * DH_DIFF
    kv_dtype = BF16 if rope else F32
    mod_rows = mod.shape[0]
    mod_idx = (lambda bi, ti: (bi, 0, 0)) if mod_rows > 1 else (lambda bi, ti: (0, 0, 0))
    tok = lambda n: pl.BlockSpec((1, tm, n), lambda bi, ti: (bi, ti, 0))
    heads = pl.BlockSpec((1, H_DIFF, tm, LANES), lambda bi, ti: (bi, 0, ti, 0))
    in_specs = [tok(d), pl.BlockSpec((1, N_MOD, d), mod_idx),
                _const_spec(w.shape), _const_spec(w2.shape), _const_spec(b2.shape)]
    args = [x, mod, w, w2, b2]
    if rope:
        in_specs += [pl.BlockSpec((tm, LANES), lambda bi, ti: (ti, 0))] * 2
        args += [cos, sin]
    sds = jax.ShapeDtypeStruct
    out_shape = [
        sds((b, t, c_conv), F32),
        sds((b, t, w_gla), F32),
        sds((b, t, w_gla), F32),
        sds((b, t, w_gla), BF16),
        sds((b, t, w_gla), F32),
        sds((b, t, 2 * w_gla), F32),
        sds((b, t, w_diff), BF16),
        sds((b, H_DIFF, t, LANES), kv_dtype),
        sds((b, H_DIFF, t // tm, LANES, tm), BF16),
    ]
    out_specs = [tok(c_conv), tok(w_gla), tok(w_gla), tok(w_gla), tok(w_gla), tok(2 * w_gla),
                 tok(w_diff), heads,
                 pl.BlockSpec((1, H_DIFF, 1, LANES, tm), lambda bi, ti: (bi, 0, ti, 0, 0))]
    if not rope:
        out_shape.append(sds((b, H_DIFF, t, LANES), F32))
        out_specs.append(heads)
    kern = functools.partial(_mixin_kernel, rope=rope, c_conv=c_conv, w_gla=w_gla, w_diff=w_diff)
    return pl.pallas_call(
        kern,
        grid=(b, t // tm),
        in_specs=in_specs,
        out_specs=out_specs,
        out_shape=out_shape,
        compiler_params=_cparams(("parallel", "parallel")),
        name="mixer_in",
    )(*args)


def _conv_kernel(prev_ref, cur_ref, next_ref, w_ref, b_ref, g_ref, beta_ref, o_ref, win_ref, *, tc, nt):
    i = pl.program_id(1)
    pad = CONV_WIDTH // 2
    win_ref[0:CONV_HALO] = jnp.where(i > 0, prev_ref[0], 0.0)
    win_ref[CONV_HALO:CONV_HALO + tc] = cur_ref[0]
    win_ref[CONV_HALO + tc:2 * CONV_HALO + tc] = jnp.where(i < nt - 1, next_ref[0], 0.0)
    w = w_ref[...]
    for r in range(0, tc, CONV_ROWS):
        acc = jnp.zeros((CONV_ROWS, w.shape[1]), F32)
        for k in range(CONV_WIDTH):
            start = CONV_HALO - pad + r + k
            acc = acc + win_ref[start:start + CONV_ROWS, :] * w[k:k + 1, :]
        y = _layer_norm(acc + b_ref[...], g_ref[...], beta_ref[...])
        o_ref[0, r:r + CONV_ROWS, :] = _silu(y).astype(o_ref.dtype)


def _conv(yglu, w, bias, g, beta, *, tc):
    b, t, c = yglu.shape
    nt = t // tc
    hb = tc // CONV_HALO
    n_halo = t // CONV_HALO
    kern = functools.partial(_conv_kernel, tc=tc, nt=nt)
    return pl.pallas_call(
        kern,
        grid=(b, nt),
        in_specs=[
            pl.BlockSpec((1, CONV_HALO, c), lambda bi, i: (bi, jnp.maximum(i * hb - 1, 0), 0)),
            pl.BlockSpec((1, tc, c), lambda bi, i: (bi, i, 0)),
            pl.BlockSpec((1, CONV_HALO, c), lambda bi, i: (bi, jnp.minimum((i + 1) * hb, n_halo - 1), 0)),
            _const_spec(w.shape), _const_spec((1, c)), _const_spec((1, c)), _const_spec((1, c)),
        ],
        out_specs=pl.BlockSpec((1, tc, c), lambda bi, i: (bi, i, 0)),
        out_shape=jax.ShapeDtypeStruct((b, t, c), BF16),
        scratch_shapes=[pltpu.VMEM((tc + 2 * CONV_HALO, c), F32)],
        compiler_params=_cparams(("parallel", "parallel")),
        name="conv_module",
    )(yglu, yglu, yglu, w, bias, g, beta)


def _split3(x):
    hi = x.astype(BF16)
    r1 = x - hi.astype(F32)
    mid = r1.astype(BF16)
    lo = (r1 - mid.astype(F32)).astype(BF16)
    return hi, mid, lo


def _gla_kernel(*refs, nc, has_state, want_state):
    it = iter(refs)
    q_ref, k_ref, v_ref, g_ref = next(it), next(it), next(it), next(it)
    s0_ref = next(it) if has_state else None
    o_ref = next(it)
    sfin_ref = next(it) if want_state else None
    s_scr, qt_scr, kv_scr, dec_scr, sin_scr = next(it), next(it), next(it), next(it), next(it)

    d = pl.program_id(1)
    i = pl.program_id(2)
    nblk = pl.num_programs(2)
    c = GLA_CHUNK
    width = s_scr.shape[0]
    nh = width // c

    @pl.when(i == 0)
    def _():
        if has_state:
            s_scr[...] = s0_ref[0, 0]
        else:
            s_scr[...] = jnp.zeros_like(s_scr)

    fwd = d == 0
    row = lax.broadcasted_iota(jnp.int32, (c, c), 0)
    col = lax.broadcasted_iota(jnp.int32, (c, c), 1)
    sgn = jnp.where(fwd, 1, -1)
    tri = (row - col) * sgn >= 0
    tri_b = jnp.where(tri, 1.0, 0.0).astype(BF16)
    rowa = lax.broadcasted_iota(jnp.int32, (c, width), 0)
    cola = lax.broadcasted_iota(jnp.int32, (c, width), 1) % c
    tri_cat = (rowa - cola) * sgn >= 0
    rb = lax.broadcasted_iota(jnp.int32, (width, width), 0) // c
    cb = lax.broadcasted_iota(jnp.int32, (width, width), 1) // c
    bd = rb == cb

    rows = [slice(ci * c, (ci + 1) * c) for ci in range(nc)]
    gcums = []
    for rs in rows:
        ghi, gmid, glo = _split3(g_ref[0, rs, :])
        gcums.append(_dot(tri_b, ghi) + _dot(tri_b, gmid) + _dot(tri_b, glo))
    scores = []
    for ci, (rs, gcum) in enumerate(zip(rows, gcums)):
        q = q_ref[0, rs, :]
        k = k_ref[0, rs, :]
        gtot = jnp.where(fwd, gcum[c - 1:c, :], gcum[0:1, :])
        q_t = (q * jnp.exp(gcum)).astype(BF16)
        k_t = k * jnp.exp(-gcum)
        k_hat = (k * jnp.exp(gtot - gcum)).astype(BF16)
        k_bd = jnp.where(bd, jnp.concatenate([k_t] * nh, axis=0), 0.0).astype(BF16)
        scores.append(_dot_nt(q_t, k_bd))
        qt_scr[rs, :] = q_t
        kv_scr[ci] = jnp.where(bd, _dot_tn(v_ref[0, rs, :], k_hat), 0.0)
        dec_scr[ci] = jnp.broadcast_to(jnp.exp(gtot), dec_scr.shape[1:])
    for rs, a in zip(rows, scores):
        v = v_ref[0, rs, :]
        a = jnp.where(tri_cat, a, 0.0).astype(BF16)
        v_bd = jnp.where(bd, jnp.concatenate([v] * nh, axis=0), jnp.zeros((), v.dtype))
        o_ref[0, 0, rs, :] = _dot(a, v_bd)

    order = [jnp.where(fwd, j, nc - 1 - j) for j in range(nc)]
    for j, cidx in enumerate(order):
        s_in = s_scr[...]
        sin_scr[j] = s_in.astype(BF16)
        s_scr[...] = s_in * dec_scr[cidx, 0:1, :] + kv_scr[cidx]
    for j, cidx in enumerate(order):
        rs = pl.ds(pl.multiple_of(cidx * c, c), c)
        o_ref[0, 0, rs, :] += _dot_nt(qt_scr[rs, :], sin_scr[j])

    if want_state:
        @pl.when(i == nblk - 1)
        def _():
            sfin_ref[0, 0] = s_scr[...]


def _gla(q, k, v, gdec, s0, *, tb, want_state):
    b, t, w = q.shape
    nblk = t // tb
    nc = tb // GLA_CHUNK
    blk = lambda d, i: i + d * (nblk - 1 - 2 * i)
    tok = pl.BlockSpec((1, tb, w), lambda bi, d, i: (bi, blk(d, i), 0))
    in_specs = [tok, tok, tok, pl.BlockSpec((1, tb, w), lambda bi, d, i: (bi, blk(d, i), d))]
    args = [q, k, v, gdec]
    has_state = s0 is not None
    if has_state:
        in_specs.append(pl.BlockSpec((1, 1, w, w), lambda bi, d, i: (bi, d, 0, 0)))
        args.append(s0)
    out_shape = [jax.ShapeDtypeStruct((2, b, t, w), F32)]
    out_specs = [pl.BlockSpec((1, 1, tb, w), lambda bi, d, i: (d, bi, blk(d, i), 0))]
    if want_state:
        out_shape.append(jax.ShapeDtypeStruct((b, 2, w, w), F32))
        out_specs.append(pl.BlockSpec((1, 1, w, w), lambda bi, d, i: (bi, d, 0, 0)))
    kern = functools.partial(_gla_kernel, nc=nc, has_state=has_state, want_state=want_state)
    return pl.pallas_call(
        kern,
        grid=(b, 2, nblk),
        in_specs=in_specs,
        out_specs=out_specs,
        out_shape=out_shape,
        scratch_shapes=[pltpu.VMEM((w, w), F32), pltpu.VMEM((tb, w), BF16),
                        pltpu.VMEM((nc, w, w), F32), pltpu.VMEM((nc, SUBLANES, w), F32),
                        pltpu.VMEM((nc, w, w), BF16)],
        compiler_params=_cparams(("parallel", "parallel", "arbitrary")),
        name="gla",
    )(*args)


def _attn_kernel(*refs, has_cache, lam_init):
    it = iter(refs)
    q_ref, k_ref, vt_ref = next(it), next(it), next(it)
    kc_ref, vtc_ref = (next(it), next(it)) if has_cache else (None, None)
    lam_ref, g_ref, o_ref, s_scr, e_scr, l_scr, acc_scr = (next(it) for _ in range(7))
    p = lam_ref[...]
    lam = (jnp.exp(jnp.sum(p[0:1] * p[1:2], axis=-1, keepdims=True))
           - jnp.exp(jnp.sum(p[2:3] * p[3:4], axis=-1, keepdims=True)) + lam_init)

    q = q_ref[0]
    tq = q.shape[0]
    lane = lax.broadcasted_iota(jnp.int32, q.shape, 1)
    zero = jnp.zeros((), q.dtype)
    qs = jnp.concatenate([jnp.where(lane < DH_DIFF, q, zero), jnp.where(lane >= DH_DIFF, q, zero)], axis=0)

    n_new, tk = vt_ref.shape[2], vt_ref.shape[4]
    n = n_new + (vtc_ref.shape[2] if has_cache else 0)
    nq = 2 * tq
    l_scr[...] = jnp.zeros_like(l_scr)
    acc_scr[...] = jnp.zeros_like(acc_scr)

    def key_tile(j):
        ref, jj = (k_ref, j) if j < n_new else (kc_ref, j - n_new)
        return ref[0, 0, jj * tk:(jj + 1) * tk, :].astype(BF16)

    def value_tile(j):
        return vt_ref[0, 0, j] if j < n_new else vtc_ref[0, 0, j - n_new]

    def scores(j, slot):
        s = _dot_nt(key_tile(j), qs)
        s_scr[slot] = s
        return jnp.max(s, axis=0, keepdims=True)

    def softmax(slot, mt, m):
        m_new = jnp.maximum(m, mt)
        alpha = jnp.exp2(m - m_new)
        lt = jnp.zeros((SUBLANES, nq), F32)
        for r in range(0, tk, ATTN_STRIP):
            e = jnp.exp2(s_scr[slot, r:r + ATTN_STRIP, :] - m_new)
            lt = lt + functools.reduce(lambda a, b: a + b,
                                       [e[i:i + SUBLANES] for i in range(0, ATTN_STRIP, SUBLANES)])
            e_scr[slot, r:r + ATTN_STRIP, :] = e.astype(BF16)
        l_scr[...] = l_scr[...] * alpha + lt
        return m_new, alpha

    def weighted(j, slot, alpha):
        acc_scr[...] = acc_scr[...] * alpha + _dot(value_tile(j), e_scr[slot])

    def step(j, slot, mt, m, alpha_prev, last=False):
        mt_next = mt if last else scores(j + 1, 1 - slot)
        m, alpha = softmax(slot, mt, m)
        weighted(j - 1, 1 - slot, alpha_prev)
        return mt_next, m, alpha

    m = jnp.full((1, nq), -1e30, F32)
    mt = scores(0, 0)
    if n == 1:
        m, alpha = softmax(0, mt, m)
    else:
        mt_next = scores(1, 1)
        m, alpha = softmax(0, mt, m)
        mt = mt_next
        for j in range(1, n - 1):
            mt, m, alpha = step(j, j % 2, mt, m, alpha)
        mt, m, alpha = step(n - 1, (n - 1) % 2, mt, m, alpha, last=True)
    weighted(n - 1, (n - 1) % 2, alpha)
    on = acc_scr[...] / jnp.sum(l_scr[...], axis=0, keepdims=True)
    o = on[:, :tq] - lam * on[:, tq:]
    ms = jnp.mean(o * o, axis=0, keepdims=True)
    y = o * lax.rsqrt(ms + LN_EPS) * g_ref[...] * (1.0 - lam_init)
    o_ref[0] = y.T.astype(o_ref.dtype)


def _attn(dq, dk, dvt, ck, cvt, lam_p, g_col, *, lam_init, tq):
    b, t, _ = dq.shape
    dv, tk = dvt.shape[3:]
    qspec = pl.BlockSpec((1, tq, LANES), lambda bi, h, qi: (bi, qi, h))
    kspec = lambda a: pl.BlockSpec((1, 1) + a.shape[2:], lambda bi, h, qi: (bi, h, 0, 0))
    vspec = lambda a: pl.BlockSpec((1, 1) + a.shape[2:], lambda bi, h, qi: (bi, h, 0, 0, 0))
    has_cache = ck is not None
    kv_args = [dk, dvt] + ([ck, cvt] if has_cache else [])
    kv_specs = [kspec(dk), vspec(dvt)] + ([kspec(ck), vspec(cvt)] if has_cache else [])
    kern = functools.partial(_attn_kernel, has_cache=has_cache, lam_init=lam_init)
    return pl.pallas_call(
        kern,
        grid=(b, H_DIFF, t // tq),
        in_specs=[qspec] + kv_specs + [_const_spec(lam_p.shape), _const_spec(g_col.shape)],
        out_specs=qspec,
        out_shape=jax.ShapeDtypeStruct(dq.shape, BF16),
        scratch_shapes=[pltpu.VMEM((2, tk, 2 * tq), F32), pltpu.VMEM((2, tk, 2 * tq), BF16),
                        pltpu.VMEM((SUBLANES, 2 * tq), F32), pltpu.VMEM((dv, 2 * tq), F32)],
        compiler_params=_cparams(("parallel", "parallel", "parallel")),
        name="diff_attn",
    )(dq, *kv_args, lam_p, g_col)


def _mixout_kernel(x_ref, mod_ref, yc_ref, of_ref, ob_ref, gate_ref, yd_ref, gn_ref, w_ref,
                   lng_ref, lnb_ref, o_ref, *, alpha):
    x = x_ref[0]
    _, g1 = _modulate(x, mod_ref, 3)
    o = of_ref[0, 0] + ob_ref[0, 0]
    width = o.shape[1]
    rb = lax.broadcasted_iota(jnp.int32, (width, width), 0) // DK_GLA
    cb = lax.broadcasted_iota(jnp.int32, (width, width), 1) // DK_GLA
    ones_bd = jnp.where(rb == cb, 1.0, 0.0).astype(BF16)
    sq = o * o
    hi = sq.astype(BF16)
    lo = (sq - hi.astype(F32)).astype(BF16)
    ms = (_dot(hi, ones_bd) + _dot(lo, ones_bd)) * (1.0 / DK_GLA)
    yg = (o * lax.rsqrt(ms + LN_EPS) * gn_ref[...] * _silu(gate_ref[0])).astype(BF16)
    c_conv = yc_ref.shape[2]
    y = (_dot(yc_ref[0], w_ref[0:c_conv, :])
         + _dot(yg, w_ref[c_conv:c_conv + width, :])
         + _dot(yd_ref[0], w_ref[c_conv + width:, :]))
    o_ref[0] = _layer_norm(alpha * x + g1 * y, lng_ref[...], lnb_ref[...])


def _mixout(x, mod, yconv, o_gla, gate, ydiff, gn, w, lng, lnb, *, alpha, tm):
    b, t, d = x.shape
    mod_rows = mod.shape[0]
    mod_idx = (lambda bi, ti: (bi, 0, 0)) if mod_rows > 1 else (lambda bi, ti: (0, 0, 0))
    tok = lambda n: pl.BlockSpec((1, tm, n), lambda bi, ti: (bi, ti, 0))
    wg = o_gla.shape[3]
    kern = functools.partial(_mixout_kernel, alpha=alpha)
    return pl.pallas_call(
        kern,
        grid=(b, t // tm),
        in_specs=[
            tok(d), pl.BlockSpec((1, N_MOD, d), mod_idx), tok(yconv.shape[2]),
            pl.BlockSpec((1, 1, tm, wg), lambda bi, ti: (0, bi, ti, 0)),
            pl.BlockSpec((1, 1, tm, wg), lambda bi, ti: (1, bi, ti, 0)),
            tok(wg), tok(ydiff.shape[2]),
            _const_spec(gn.shape), _const_spec(w.shape), _const_spec((1, d)), _const_spec((1, d)),
        ],
        out_specs=tok(d),
        out_shape=jax.ShapeDtypeStruct((b, t, d), F32),
        compiler_params=_cparams(("parallel", "parallel")),
        name="mixer_out",
    )(x, mod, yconv, o_gla, o_gla, gate, ydiff, gn, w, lng, lnb)


def _rope_tables(t):
    rows = t // GRID_W
    row = jnp.repeat(jnp.arange(rows, dtype=F32), GRID_W)
    col = jnp.tile(jnp.arange(GRID_W, dtype=F32), rows)
    seg = DH_DIFF // 2
    inv = ROPE_BASE ** (-jnp.arange(0, seg, 2, dtype=F32) / seg)
    a_r = row[:, None] * inv
    a_c = col[:, None] * inv
    ang = jnp.concatenate([a_r, a_r, a_c, a_c], axis=-1)
    ang = jnp.concatenate([ang, ang], axis=-1)
    sign = jnp.where((jnp.arange(LANES) % 32) < 16, -1.0, 1.0).astype(F32)
    return jnp.cos(ang), jnp.sin(ang) * sign


def _ffn_weights(w_in, w_out, ck):
    d, two_ff = w_in.shape
    ff = two_ff // 2
    n = ff // ck
    win = w_in.reshape(d, 2, n, ck).transpose(2, 0, 1, 3).reshape(n, d, 2 * ck).astype(BF16)
    wout = w_out.reshape(n, ck, d).astype(BF16)
    return win, wout


def _pick_tile(t, pref):
    return pref if t % pref == 0 else t


def kernel(x_prompt, x_sample, cache_diff_k, cache_diff_v, state_gla, c, c_ctx, w_ada, b_ada, w_ffn1_in,
           w_ffn1_out, w_ffn2_in, w_ffn2_out, w_in, conv_w, conv_b, conv_ln_g, conv_ln_b, gla_w_a2, gla_b_a,
           gla_norm_g, diff_lam, diff_norm_g, w_out, ln_g, ln_b):
    depth, d, _ = w_ada.shape
    alpha = (2 * depth) ** 0.25
    n_dec = c.shape[0]
    c_conv = conv_w.shape[2]
    w_gla = H_GLA * DK_GLA
    w_diff = H_DIFF * 2 * DH_DIFF
    in_conv, in_gla = 2 * c_conv, 4 * w_gla + 2 * GLA_RANK
    ff_chunk = 256

    rows = -(-(n_dec + 1) // SUBLANES) * SUBLANES
    cvec = jnp.zeros((rows, d), F32).at[:n_dec].set(c).at[n_dec].set(c_ctx)
    mod = _ada(cvec, w_ada, b_ada).reshape(depth, rows, N_MOD, d)

    cos, sin = _rope_tables(x_sample.shape[1])

    layers = []
    for l in range(depth):
        wi = w_in[l]
        w_lr = jnp.pad(wi[:, in_conv + 4 * w_gla:in_conv + in_gla], ((0, 0), (0, LANES - 2 * GLA_RANK)))
        w_all = jnp.concatenate([wi[:, :in_conv + 4 * w_gla], wi[:, in_conv + in_gla:], w_lr], axis=1).astype(BF16)
        w2 = jnp.zeros((LANES, 2 * w_gla), F32)
        w2 = w2.at[:GLA_RANK, :w_gla].set(gla_w_a2[l, 0]).at[GLA_RANK:2 * GLA_RANK, w_gla:].set(gla_w_a2[l, 1])
        layers.append(dict(
            ffn1=_ffn_weights(w_ffn1_in[l], w_ffn1_out[l], ff_chunk),
            ffn2=_ffn_weights(w_ffn2_in[l], w_ffn2_out[l], ff_chunk),
            w_all=w_all, w2=w2.astype(BF16), b2=gla_b_a[l].reshape(1, 2 * w_gla),
            conv_w=conv_w[l], conv_b=conv_b[l][None], conv_g=conv_ln_g[l][None], conv_beta=conv_ln_b[l][None],
            gn=jnp.tile(gla_norm_g[l], H_GLA)[None], lam=diff_lam[l], dg=diff_norm_g[l][:, None],
            w_out=w_out[l].astype(BF16),
            lng=[ln_g[l, i][None] for i in range(3)], lnb=[ln_b[l, i][None] for i in range(3)],
            lam_init=0.8 - 0.6 * math.exp(-0.3 * l),
        ))

    def run_layer(x, mod_l, P, ctx):
        t = x.shape[1]
        tm = _pick_tile(t, 512)
        bsz, _, d_model = x.shape
        shared_mod = mod_l.shape[0] == 1
        tf = _pick_tile(bsz * t if shared_mod else t, 1024)

        def ffn(x, which, row0, ln_idx):
            y = _ffn(x.reshape(bsz * t, d_model), mod_l, *P[which], P['lng'][ln_idx], P['lnb'][ln_idx],
                     row0=row0, alpha=alpha, tm=tf)
            return y.reshape(bsz, t, d_model)

        x = ffn(x, 'ffn1', 0, 0)
        rope = ctx is not None
        yglu, gq, gk, gv, gate, gdec, dq, dk, dvt, *dv = _mixin(
            x, mod_l, P['w_all'], P['w2'], P['b2'], cos if rope else None, sin if rope else None, tm=tm)
        yconv = _conv(yglu, P['conv_w'], P['conv_b'], P['conv_g'], P['conv_beta'], tc=_pick_tile(t, 256))
        gla_out = _gla(gq, gk, gv, gdec, ctx['s0'] if rope else None, tb=_pick_tile(t, 512), want_state=not rope)
        ydiff = _attn(dq, dk, dvt, ctx['k'] if rope else None, ctx['vt'] if rope else None, P['lam'], P['dg'],
                      lam_init=P['lam_init'], tq=_pick_tile(t, 512))
        x = _mixout(x, mod_l, yconv, gla_out[0], gate, ydiff, P['gn'], P['w_out'], P['lng'][1], P['lnb'][1],
                    alpha=alpha, tm=tm)
        x = ffn(x, 'ffn2', 6, 2)
        return x, (dk, dv[0] if dv else None, gla_out[1] if not rope else None)

    xp = x_prompt
    new_k, new_v, new_s = [], [], []
    for l in range(depth):
        xp, (k_l, v_l, s_l) = run_layer(xp, mod[l, n_dec:n_dec + 1], layers[l], None)
        new_k.append(k_l)
        new_v.append(v_l)
        bsz = s_l.shape[0]
        s6 = s_l.reshape(bsz, 2, H_GLA, DK_GLA, H_GLA, DK_GLA)
        s_heads = jnp.stack([s6[:, :, h, :, h, :] for h in range(H_GLA)], axis=2)
        new_s.append(jnp.swapaxes(s_heads, -1, -2))

    xs = x_sample
    for l in range(depth):
        st = jnp.swapaxes(state_gla[:, l], -1, -2)
        s0 = jnp.zeros(st.shape[:2] + (H_GLA, DK_GLA, H_GLA, DK_GLA), F32)
        for h in range(H_GLA):
            s0 = s0.at[:, :, h, :, h, :].set(st[:, :, h])
        s0 = s0.reshape(st.shape[0], 2, w_gla, w_gla)
        past = cache_diff_v.shape[3]
        tile = _pick_tile(x_sample.shape[1], 512)
        assert past % tile == 0, "cached context length must be a whole number of key tiles"
        cvt = cache_diff_v[:, l].astype(BF16).reshape(n_dec, H_DIFF, past // tile, tile, LANES)
        ctx = dict(k=cache_diff_k[:, l].astype(BF16), vt=jnp.swapaxes(cvt, -1, -2), s0=s0)
        xs, _ = run_layer(xs, mod[l, :n_dec], layers[l], ctx)

    return (xp, xs, jnp.stack(new_k, axis=1), jnp.stack(new_v, axis=1), jnp.stack(new_s, axis=1))
```

```python
import functools
import math

import jax
import jax.numpy as jnp
from jax import lax
from jax.experimental import pallas as pl
from jax.experimental.pallas import tpu as pltpu

F32 = jnp.float32
BF16 = jnp.bfloat16

GRID_W = 64
CONV_WIDTH = 31
H_GLA = 4
DK_GLA = 64
GLA_RANK = 16
GLA_TAU = 16.0
GLA_CHUNK = 64
H_DIFF = 4
DH_DIFF = 64
ROPE_BASE = 10000.0
LN_EPS = 1e-5
N_MOD = 9

LANES = 128
SUBLANES = 8
V7X_VMEM_BYTES = 64 * 1024 * 1024
VMEM_LIMIT = V7X_VMEM_BYTES - 8 * 1024 * 1024

CONV_HALO = 16
CONV_ROWS = 32
ATTN_STRIP = 32
ATTN_SLOTS = 3
ONES_ROWS = 16
ROW_STRIP = 32
LN_STRIP = 64


def _cparams(sem):
    return pltpu.CompilerParams(dimension_semantics=sem, vmem_limit_bytes=VMEM_LIMIT)


def _const_spec(shape):
    nd = len(shape)
    return pl.BlockSpec(shape, lambda *_: (0,) * nd, pipeline_mode=pl.Buffered(1))


def _dot(a, b):
    return jnp.dot(a, b, preferred_element_type=F32)


def _dot_nt(a, b):
    return lax.dot_general(a, b, (((1,), (1,)), ((), ())), preferred_element_type=F32)


def _dot_tn(a, b):
    return lax.dot_general(a, b, (((0,), (0,)), ((), ())), preferred_element_type=F32)


def _silu(x):
    return x * jax.nn.sigmoid(x)


def _layer_norm(y, g, b):
    mu = jnp.mean(y, axis=-1, keepdims=True)
    yc = y - mu
    var = jnp.mean(yc * yc, axis=-1, keepdims=True)
    return yc * lax.rsqrt(var + LN_EPS) * g + b


def _modulate(x, mod_ref, row0):
    sh = mod_ref[0, row0:row0 + 1, :]
    sc = mod_ref[0, row0 + 1:row0 + 2, :]
    g = mod_ref[0, row0 + 2:row0 + 3, :]
    return x * (1.0 + sc) + sh, g


def _ada_kernel(c_ref, w_ref, b_ref, o_ref):
    s = _silu(c_ref[...]).astype(BF16)
    o_ref[0] = _dot(s, w_ref[0].astype(BF16)) + b_ref[0]


def _ada(cvec, w_ada, b_ada):
    depth, d, n = w_ada.shape
    rows = cvec.shape[0]
    tn = d
    return pl.pallas_call(
        _ada_kernel,
        grid=(depth, n // tn),
        in_specs=[
            pl.BlockSpec((rows, d), lambda l, j: (0, 0)),
            pl.BlockSpec((1, d, tn), lambda l, j: (l, 0, j)),
            pl.BlockSpec((1, 1, tn), lambda l, j: (l, 0, j)),
        ],
        out_specs=pl.BlockSpec((1, rows, tn), lambda l, j: (l, 0, j)),
        out_shape=jax.ShapeDtypeStruct((depth, rows, n), F32),
        compiler_params=_cparams(("arbitrary", "arbitrary")),
        name="ada_mod",
    )(cvec, w_ada, b_ada.reshape(depth, 1, n))


def _strips(n_rows, fn):
    def body(i, carry):
        fn(pl.ds(pl.multiple_of(i * ROW_STRIP, ROW_STRIP), ROW_STRIP))
        return carry

    lax.fori_loop(0, n_rows // ROW_STRIP, body, 0, unroll=2)


def _layer_norm_rows(n_rows, make_y, o_ref, g, b):
    n = n_rows // LN_STRIP
    sl = lambda i: pl.ds(i * LN_STRIP if isinstance(i, int) else pl.multiple_of(i * LN_STRIP, LN_STRIP), LN_STRIP)

    def means(i):
        y = make_y(sl(i))
        o_ref[sl(i), :] = y
        return jnp.mean(y, axis=-1, keepdims=True)

    def rstd(i, mu):
        yc = o_ref[sl(i), :] - mu
        return lax.rsqrt(jnp.mean(yc * yc, axis=-1, keepdims=True) + LN_EPS)

    def normalise(i, mu, r):
        o_ref[sl(i), :] = (o_ref[sl(i), :] - mu) * r * g + b

    if n == 1:
        mu = means(0)
        normalise(0, mu, rstd(0, mu))
        return
    mu_a = means(0)
    mu_b = means(1)
    r_a = rstd(0, mu_a)

    def body(i, carry):
        mu_a, r_a, mu_b = carry
        r_b = rstd(i - 1, mu_b)
        mu_c = means(i)
        normalise(i - 2, mu_a, r_a)
        return mu_b, r_b, mu_c

    mu_a, r_a, mu_b = lax.fori_loop(2, n, body, (mu_a, r_a, mu_b))
    r_b = rstd(n - 1, mu_b)
    normalise(n - 2, mu_a, r_a)
    normalise(n - 1, mu_b, r_b)


def _ffn_kernel(x_ref, mod_ref, win_ref, wout_ref, lng_ref, lnb_ref, o_ref, h_scr, acc_ref, *, row0, alpha):
    tm = x_ref.shape[0]
    sh = mod_ref[0, row0:row0 + 1, :]
    sc1 = 1.0 + mod_ref[0, row0 + 1:row0 + 2, :]
    half_g = 0.5 * mod_ref[0, row0 + 2:row0 + 3, :]
    n_chunks, _, two_ck = win_ref.shape
    ck = two_ck // 2

    def modulate(rs):
        h_scr[rs, :] = (x_ref[rs, :] * sc1 + sh).astype(BF16)

    _strips(tm, modulate)

    def chunk(j):
        u = _dot(h_scr[...], win_ref[j])
        act = (_silu(u[:, :ck]) * u[:, ck:]).astype(BF16)
        return _dot(act, wout_ref[j])

    acc_ref[...] = chunk(0)

    def body(j, carry):
        acc_ref[...] += chunk(j)
        return carry

    lax.fori_loop(1, n_chunks, body, 0, unroll=2 if n_chunks % 2 else 1)
    _layer_norm_rows(tm, lambda rs: alpha * x_ref[rs, :] + half_g * acc_ref[rs, :], o_ref,
                     lng_ref[...], lnb_ref[...])


def _ffn(x, mod, win, wout, lng, lnb, *, row0, alpha, tm):
    n_tok, d = x.shape
    per_req = n_tok // mod.shape[0]
    kern = functools.partial(_ffn_kernel, row0=row0, alpha=alpha)
    return pl.pallas_call(
        kern,
        grid=(n_tok // tm,),
        in_specs=[
            pl.BlockSpec((tm, d), lambda ti: (ti, 0)),
            pl.BlockSpec((1, N_MOD, d), lambda ti: ((ti * tm) // per_req, 0, 0)),
            _const_spec(win.shape),
            _const_spec(wout.shape),
            _const_spec((1, d)),
            _const_spec((1, d)),
        ],
        out_specs=pl.BlockSpec((tm, d), lambda ti: (ti, 0)),
        out_shape=jax.ShapeDtypeStruct((n_tok, d), F32),
        scratch_shapes=[pltpu.VMEM((tm, d), BF16), pltpu.VMEM((tm, d), F32)],
        compiler_params=_cparams(("parallel",)),
        name="ffn",
    )(x, mod, win, wout, lng, lnb)


def _log_sigmoid(x):
    return jnp.minimum(x, 0.0) - jnp.log(1.0 + jnp.exp(-jnp.abs(x)))


def _rope(x, cos, sin_signed, first_half):
    rot = jnp.where(first_half, pltpu.roll(x, LANES - 16, 1), pltpu.roll(x, 16, 1))
    return x * cos + rot * sin_signed


def _mixin_kernel(*refs, rope, c_conv, w_gla, w_diff):
    if rope:
        (x_ref, mod_ref, w_ref, w2_ref, b2_ref, cos_ref, sin_ref,
         yglu_ref, gq_ref, gk_ref, gv_ref, gate_ref, gdec_ref, dq_ref, dk_ref, dvt_ref) = refs
    else:
        (x_ref, mod_ref, w_ref, w2_ref, b2_ref,
         yglu_ref, gq_ref, gk_ref, gv_ref, gate_ref, gdec_ref, dq_ref, dk_ref, dvt_ref, dv_ref) = refs
    x = x_ref[0]
    h, _ = _modulate(x, mod_ref, 3)
    h = h.astype(BF16)
    o_gla = 2 * c_conv
    o_diff = o_gla + 4 * w_gla
    o_lr = o_diff + 3 * w_diff

    zc = _dot(h, w_ref[:, 0:o_gla])
    yglu_ref[0] = zc[:, :c_conv] * jax.nn.sigmoid(zc[:, c_conv:])

    zg = _dot(h, w_ref[:, o_gla:o_diff])
    gq_ref[0] = zg[:, 0:w_gla] * (DK_GLA ** -0.5)
    gk_ref[0] = zg[:, w_gla:2 * w_gla]
    gv_ref[0] = zg[:, 2 * w_gla:3 * w_gla].astype(BF16)
    gate_ref[0] = zg[:, 3 * w_gla:4 * w_gla]

    zl = _dot(h, w_ref[:, o_lr:o_lr + LANES]).astype(BF16)
    pre = _dot(zl, w2_ref[...]) + b2_ref[...]
    gdec_ref[0] = _log_sigmoid(pre) * (1.0 / GLA_TAU)

    zd = _dot(h, w_ref[:, o_diff:o_lr])
    if rope:
        cos = cos_ref[...]
        sin = sin_ref[...]
        lane = lax.broadcasted_iota(jnp.int32, cos.shape, 1)
        first_half = (lane % 32) < 16
    for hd in range(H_DIFF):
        sl = slice(hd * LANES, (hd + 1) * LANES)
        q = zd[:, sl]
        k = zd[:, w_diff + hd * LANES:w_diff + (hd + 1) * LANES]
        v = zd[:, 2 * w_diff + hd * LANES:2 * w_diff + (hd + 1) * LANES]
        if rope:
            q = _rope(q, cos, sin, first_half)
            k = _rope(k, cos, sin, first_half)
        dq_ref[0, :, sl] = (q * (DH_DIFF ** -0.5 * math.log2(math.e))).astype(BF16)
        dk_ref[0, hd] = k.astype(dk_ref.dtype)
        dvt_ref[0, hd, 0, 0:LANES, :] = v.T.astype(BF16)
        dvt_ref[0, hd, 0, LANES:, :] = jnp.ones((ONES_ROWS, v.shape[0]), BF16)
        if not rope:
            dv_ref[0, hd] = v


def _mixin(x, mod, w, w2, b2, cos, sin, *, tm):
    b, t, d = x.shape
    rope = cos is not None
    c_conv = 256
    w_gla = H_GLA * DK_GLA
    w_diff = H_DIFF * 2 * DH_DIFF
    kv_dtype = BF16 if rope else F32
    mod_rows = mod.shape[0]
    mod_idx = (lambda bi, ti: (bi, 0, 0)) if mod_rows > 1 else (lambda bi, ti: (0, 0, 0))
    tok = lambda n: pl.BlockSpec((1, tm, n), lambda bi, ti: (bi, ti, 0))
    heads = pl.BlockSpec((1, H_DIFF, tm, LANES), lambda bi, ti: (bi, 0, ti, 0))
    in_specs = [tok(d), pl.BlockSpec((1, N_MOD, d), mod_idx),
                _const_spec(w.shape), _const_spec(w2.shape), _const_spec(b2.shape)]
    args = [x, mod, w, w2, b2]
    if rope:
        in_specs += [pl.BlockSpec((tm, LANES), lambda bi, ti: (ti, 0))] * 2
        args += [cos, sin]
    sds = jax.ShapeDtypeStruct
    out_shape = [
        sds((b, t, c_conv), F32),
        sds((b, t, w_gla), F32),
        sds((b, t, w_gla), F32),
        sds((b, t, w_gla), BF16),
        sds((b, t, w_gla), F32),
        sds((b, t, 2 * w_gla), F32),
        sds((b, t, w_diff), BF16),
        sds((b, H_DIFF, t, LANES), kv_dtype),
        sds((b, H_DIFF, t // tm, LANES + ONES_ROWS, tm), BF16),
    ]
    out_specs = [tok(c_conv), tok(w_gla), tok(w_gla), tok(w_gla), tok(w_gla), tok(2 * w_gla),
                 tok(w_diff), heads,
                 pl.BlockSpec((1, H_DIFF, 1, LANES + ONES_ROWS, tm), lambda bi, ti: (bi, 0, ti, 0, 0))]
    if not rope:
        out_shape.append(sds((b, H_DIFF, t, LANES), F32))
        out_specs.append(heads)
    kern = functools.partial(_mixin_kernel, rope=rope, c_conv=c_conv, w_gla=w_gla, w_diff=w_diff)
    return pl.pallas_call(
        kern,
        grid=(b, t // tm),
        in_specs=in_specs,
        out_specs=out_specs,
        out_shape=out_shape,
        compiler_params=_cparams(("parallel", "parallel")),
        name="mixer_in",
    )(*args)


def _conv_kernel(prev_ref, cur_ref, next_ref, w_ref, b_ref, g_ref, beta_ref, o_ref, win_ref, *, tc, nt):
    i = pl.program_id(1)
    pad = CONV_WIDTH // 2
    win_ref[0:CONV_HALO] = jnp.where(i > 0, prev_ref[0], 0.0)
    win_ref[CONV_HALO:CONV_HALO + tc] = cur_ref[0]
    win_ref[CONV_HALO + tc:2 * CONV_HALO + tc] = jnp.where(i < nt - 1, next_ref[0], 0.0)
    w = w_ref[...]
    for r in range(0, tc, CONV_ROWS):
        acc = jnp.zeros((CONV_ROWS, w.shape[1]), F32)
        for k in range(CONV_WIDTH):
            start = CONV_HALO - pad + r + k
            acc = acc + win_ref[start:start + CONV_ROWS, :] * w[k:k + 1, :]
        y = _layer_norm(acc + b_ref[...], g_ref[...], beta_ref[...])
        o_ref[0, r:r + CONV_ROWS, :] = _silu(y).astype(o_ref.dtype)


def _conv(yglu, w, bias, g, beta, *, tc):
    b, t, c = yglu.shape
    nt = t // tc
    hb = tc // CONV_HALO
    n_halo = t // CONV_HALO
    kern = functools.partial(_conv_kernel, tc=tc, nt=nt)
    return pl.pallas_call(
        kern,
        grid=(b, nt),
        in_specs=[
            pl.BlockSpec((1, CONV_HALO, c), lambda bi, i: (bi, jnp.maximum(i * hb - 1, 0), 0)),
            pl.BlockSpec((1, tc, c), lambda bi, i: (bi, i, 0)),
            pl.BlockSpec((1, CONV_HALO, c), lambda bi, i: (bi, jnp.minimum((i + 1) * hb, n_halo - 1), 0)),
            _const_spec(w.shape), _const_spec((1, c)), _const_spec((1, c)), _const_spec((1, c)),
        ],
        out_specs=pl.BlockSpec((1, tc, c), lambda bi, i: (bi, i, 0)),
        out_shape=jax.ShapeDtypeStruct((b, t, c), BF16),
        scratch_shapes=[pltpu.VMEM((tc + 2 * CONV_HALO, c), F32)],
        compiler_params=_cparams(("parallel", "parallel")),
        name="conv_module",
    )(yglu, yglu, yglu, w, bias, g, beta)


def _split3(x):
    hi = x.astype(BF16)
    r1 = x - hi.astype(F32)
    mid = r1.astype(BF16)
    lo = (r1 - mid.astype(F32)).astype(BF16)
    return hi, mid, lo


def _gla_kernel(*refs, nc, has_state, want_state):
    it = iter(refs)
    q_ref, k_ref, v_ref, g_ref = next(it), next(it), next(it), next(it)
    s0_ref = next(it) if has_state else None
    o_ref = next(it)
    sfin_ref = next(it) if want_state else None
    s_scr, qt_scr, kv_scr, dec_scr, sin_scr = next(it), next(it), next(it), next(it), next(it)

    d = pl.program_id(1)
    i = pl.program_id(2)
    nblk = pl.num_programs(2)
    c = GLA_CHUNK
    width = s_scr.shape[0]
    nh = width // c

    @pl.when(i == 0)
    def _():
        if has_state:
            s_scr[...] = s0_ref[0, 0]
        else:
            s_scr[...] = jnp.zeros_like(s_scr)

    fwd = d == 0
    row = lax.broadcasted_iota(jnp.int32, (c, c), 0)
    col = lax.broadcasted_iota(jnp.int32, (c, c), 1)
    sgn = jnp.where(fwd, 1, -1)
    tri = (row - col) * sgn >= 0
    tri_b = jnp.where(tri, 1.0, 0.0).astype(BF16)
    rowa = lax.broadcasted_iota(jnp.int32, (c, width), 0)
    cola = lax.broadcasted_iota(jnp.int32, (c, width), 1) % c
    tri_cat = (rowa - cola) * sgn >= 0
    rb = lax.broadcasted_iota(jnp.int32, (width, width), 0) // c
    cb = lax.broadcasted_iota(jnp.int32, (width, width), 1) // c
    bd = rb == cb

    rows = [slice(ci * c, (ci + 1) * c) for ci in range(nc)]
    gcums = []
    for rs in rows:
        ghi, gmid, glo = _split3(g_ref[0, rs, :])
        gcums.append(_dot(tri_b, ghi) + _dot(tri_b, gmid) + _dot(tri_b, glo))
    scores = []
    for ci, (rs, gcum) in enumerate(zip(rows, gcums)):
        q = q_ref[0, rs, :]
        k = k_ref[0, rs, :]
        gtot = jnp.where(fwd, gcum[c - 1:c, :], gcum[0:1, :])
        q_t = (q * jnp.exp(gcum)).astype(BF16)
        k_t = k * jnp.exp(-gcum)
        k_hat = (k * jnp.exp(gtot - gcum)).astype(BF16)
        k_bd = jnp.where(bd, jnp.concatenate([k_t] * nh, axis=0), 0.0).astype(BF16)
        scores.append(_dot_nt(q_t, k_bd))
        qt_scr[rs, :] = q_t
        kv_scr[ci] = jnp.where(bd, _dot_tn(v_ref[0, rs, :], k_hat), 0.0)
        dec_scr[ci] = jnp.broadcast_to(jnp.exp(gtot), dec_scr.shape[1:])
    for rs, a in zip(rows, scores):
        v = v_ref[0, rs, :]
        a = jnp.where(tri_cat, a, 0.0).astype(BF16)
        v_bd = jnp.where(bd, jnp.concatenate([v] * nh, axis=0), jnp.zeros((), v.dtype))
        o_ref[0, 0, rs, :] = _dot(a, v_bd)

    order = [jnp.where(fwd, j, nc - 1 - j) for j in range(nc)]
    for j, cidx in enumerate(order):
        s_in = s_scr[...]
        sin_scr[j] = s_in.astype(BF16)
        s_scr[...] = s_in * dec_scr[cidx, 0:1, :] + kv_scr[cidx]
    for j, cidx in enumerate(order):
        rs = pl.ds(pl.multiple_of(cidx * c, c), c)
        o_ref[0, 0, rs, :] += _dot_nt(qt_scr[rs, :], sin_scr[j])

    if want_state:
        @pl.when(i == nblk - 1)
        def _():
            sfin_ref[0, 0] = s_scr[...]


def _gla(q, k, v, gdec, s0, *, tb, want_state):
    b, t, w = q.shape
    nblk = t // tb
    nc = tb // GLA_CHUNK
    blk = lambda d, i: i + d * (nblk - 1 - 2 * i)
    tok = pl.BlockSpec((1, tb, w), lambda bi, d, i: (bi, blk(d, i), 0))
    in_specs = [tok, tok, tok, pl.BlockSpec((1, tb, w), lambda bi, d, i: (bi, blk(d, i), d))]
    args = [q, k, v, gdec]
    has_state = s0 is not None
    if has_state:
        in_specs.append(pl.BlockSpec((1, 1, w, w), lambda bi, d, i: (bi, d, 0, 0)))
        args.append(s0)
    out_shape = [jax.ShapeDtypeStruct((2, b, t, w), F32)]
    out_specs = [pl.BlockSpec((1, 1, tb, w), lambda bi, d, i: (d, bi, blk(d, i), 0))]
    if want_state:
        out_shape.append(jax.ShapeDtypeStruct((b, 2, w, w), F32))
        out_specs.append(pl.BlockSpec((1, 1, w, w), lambda bi, d, i: (bi, d, 0, 0)))
    kern = functools.partial(_gla_kernel, nc=nc, has_state=has_state, want_state=want_state)
    return pl.pallas_call(
        kern,
        grid=(b, 2, nblk),
        in_specs=in_specs,
        out_specs=out_specs,
        out_shape=out_shape,
        scratch_shapes=[pltpu.VMEM((w, w), F32), pltpu.VMEM((tb, w), BF16),
                        pltpu.VMEM((nc, w, w), F32), pltpu.VMEM((nc, SUBLANES, w), F32),
                        pltpu.VMEM((nc, w, w), BF16)],
        compiler_params=_cparams(("parallel", "parallel", "arbitrary")),
        name="gla",
    )(*args)


def _attn_kernel(*refs, has_cache, lam_init, n_slots):
    it = iter(refs)
    q_ref, k_ref, vt_ref = next(it), next(it), next(it)
    kc_ref, vtc_ref = (next(it), next(it)) if has_cache else (None, None)
    lam_ref, g_ref, o_ref = next(it), next(it), next(it)
    s_scr = [next(it) for _ in range(n_slots)]
    e_scr = [next(it) for _ in range(n_slots)]
    acc_scr = next(it)
    p = lam_ref[...]
    lam = (jnp.exp(jnp.sum(p[0:1] * p[1:2], axis=-1, keepdims=True))
           - jnp.exp(jnp.sum(p[2:3] * p[3:4], axis=-1, keepdims=True)) + lam_init)

    q = q_ref[0]
    tq = q.shape[0]
    lane = lax.broadcasted_iota(jnp.int32, q.shape, 1)
    zero = jnp.zeros((), q.dtype)
    qs = jnp.concatenate([jnp.where(lane < DH_DIFF, q, zero), jnp.where(lane >= DH_DIFF, q, zero)], axis=0)

    n_new, tk = vt_ref.shape[2], vt_ref.shape[4]
    n = n_new + (vtc_ref.shape[2] if has_cache else 0)
    nq = 2 * tq
    acc_scr[...] = jnp.zeros_like(acc_scr)

    def key_tile(j):
        ref, jj = (k_ref, j) if j < n_new else (kc_ref, j - n_new)
        return ref[0, 0, jj * tk:(jj + 1) * tk, :].astype(BF16)

    def value_tile(j):
        return vt_ref[0, 0, j] if j < n_new else vtc_ref[0, 0, j - n_new]

    def scores(j, slot):
        s_scr[slot][...] = _dot_nt(key_tile(j), qs)
        strips = [s_scr[slot][r:r + ATTN_STRIP, :] for r in range(0, tk, ATTN_STRIP)]
        return jnp.max(functools.reduce(jnp.maximum, strips), axis=0, keepdims=True)

    def softmax(slot, mt, m):
        m_new = jnp.maximum(m, mt)
        alpha = jnp.exp2(m - m_new)
        for r in range(0, tk, ATTN_STRIP):
            e_scr[slot][r:r + ATTN_STRIP, :] = jnp.exp2(s_scr[slot][r:r + ATTN_STRIP, :] - m_new).astype(BF16)
        return m_new, alpha

    def weighted(j, slot, alpha):
        acc_scr[...] = acc_scr[...] * alpha + _dot(value_tile(j), e_scr[slot][...])

    slot = lambda j: j % n_slots

    def step(j, mt, m, alpha_prev, last=False):
        mt_next = mt if last else scores(j + 1, slot(j + 1))
        m, alpha = softmax(slot(j), mt, m)
        weighted(j - 1, slot(j - 1), alpha_prev)
        return mt_next, m, alpha

    m = jnp.full((1, nq), -1e30, F32)
    mt = scores(0, slot(0))
    if n == 1:
        m, alpha = softmax(slot(0), mt, m)
    else:
        mt_next = scores(1, slot(1))
        m, alpha = softmax(slot(0), mt, m)
        mt = mt_next
        for j in range(1, n - 1):
            mt, m, alpha = step(j, mt, m, alpha)
        mt, m, alpha = step(n - 1, mt, m, alpha, last=True)
    weighted(n - 1, slot(n - 1), alpha)
    dv = o_ref.shape[2]
    on = acc_scr[0:dv, :] / acc_scr[dv:dv + 1, :]
    o = on[:, :tq] - lam * on[:, tq:]
    ms = jnp.mean(o * o, axis=0, keepdims=True)
    y = o * lax.rsqrt(ms + LN_EPS) * g_ref[...] * (1.0 - lam_init)
    o_ref[0] = y.T.astype(o_ref.dtype)


def _attn(dq, dk, dvt, ck, cvt, lam_p, g_col, *, lam_init, tq):
    b, t, _ = dq.shape
    dv, tk = dvt.shape[3:]
    qspec = pl.BlockSpec((1, tq, LANES), lambda bi, h, qi: (bi, qi, h))
    kspec = lambda a: pl.BlockSpec((1, 1) + a.shape[2:], lambda bi, h, qi: (bi, h, 0, 0))
    vspec = lambda a: pl.BlockSpec((1, 1) + a.shape[2:], lambda bi, h, qi: (bi, h, 0, 0, 0))
    has_cache = ck is not None
    n_tiles = dvt.shape[2] + (cvt.shape[2] if has_cache else 0)
    n_slots = min(n_tiles, ATTN_SLOTS)
    kv_args = [dk, dvt] + ([ck, cvt] if has_cache else [])
    kv_specs = [kspec(dk), vspec(dvt)] + ([kspec(ck), vspec(cvt)] if has_cache else [])
    kern = functools.partial(_attn_kernel, has_cache=has_cache, lam_init=lam_init, n_slots=n_slots)
    return pl.pallas_call(
        kern,
        grid=(b, H_DIFF, t // tq),
        in_specs=[qspec] + kv_specs + [_const_spec(lam_p.shape), _const_spec(g_col.shape)],
        out_specs=qspec,
        out_shape=jax.ShapeDtypeStruct(dq.shape, BF16),
        scratch_shapes=([pltpu.VMEM((tk, 2 * tq), F32)] * n_slots + [pltpu.VMEM((tk, 2 * tq), BF16)] * n_slots
                        + [pltpu.VMEM((dv, 2 * tq), F32)]),
        compiler_params=_cparams(("parallel", "parallel", "parallel")),
        name="diff_attn",
    )(dq, *kv_args, lam_p, g_col)


def _mixout_kernel(x_ref, mod_ref, yc_ref, of_ref, ob_ref, gate_ref, yd_ref, gn_ref, w_ref,
                   lng_ref, lnb_ref, o_ref, *, alpha):
    x = x_ref[0]
    g1 = mod_ref[0, 5:6, :]
    o = of_ref[0, 0] + ob_ref[0, 0]
    width = o.shape[1]
    rb = lax.broadcasted_iota(jnp.int32, (width, width), 0) // DK_GLA
    cb = lax.broadcasted_iota(jnp.int32, (width, width), 1) // DK_GLA
    ones_bd = jnp.where(rb == cb, 1.0, 0.0).astype(BF16)
    sq = o * o
    hi = sq.astype(BF16)
    lo = (sq - hi.astype(F32)).astype(BF16)
    ms = (_dot(hi, ones_bd) + _dot(lo, ones_bd)) * (1.0 / DK_GLA)
    yg = (o * lax.rsqrt(ms + LN_EPS) * gn_ref[...] * _silu(gate_ref[0])).astype(BF16)
    c_conv = yc_ref.shape[2]
    y = (_dot(yc_ref[0], w_ref[0:c_conv, :])
         + _dot(yg, w_ref[c_conv:c_conv + width, :])
         + _dot(yd_ref[0], w_ref[c_conv + width:, :]))
    o_ref[0] = _layer_norm(alpha * x + g1 * y, lng_ref[...], lnb_ref[...])


def _mixout(x, mod, yconv, o_gla, gate, ydiff, gn, w, lng, lnb, *, alpha, tm):
    b, t, d = x.shape
    mod_rows = mod.shape[0]
    mod_idx = (lambda bi, ti: (bi, 0, 0)) if mod_rows > 1 else (lambda bi, ti: (0, 0, 0))
    tok = lambda n: pl.BlockSpec((1, tm, n), lambda bi, ti: (bi, ti, 0))
    wg = o_gla.shape[3]
    kern = functools.partial(_mixout_kernel, alpha=alpha)
    return pl.pallas_call(
        kern,
        grid=(b, t // tm),
        in_specs=[
            tok(d), pl.BlockSpec((1, N_MOD, d), mod_idx), tok(yconv.shape[2]),
            pl.BlockSpec((1, 1, tm, wg), lambda bi, ti: (0, bi, ti, 0)),
            pl.BlockSpec((1, 1, tm, wg), lambda bi, ti: (1, bi, ti, 0)),
            tok(wg), tok(ydiff.shape[2]),
            _const_spec(gn.shape), _const_spec(w.shape), _const_spec((1, d)), _const_spec((1, d)),
        ],
        out_specs=tok(d),
        out_shape=jax.ShapeDtypeStruct((b, t, d), F32),
        compiler_params=_cparams(("parallel", "parallel")),
        name="mixer_out",
    )(x, mod, yconv, o_gla, o_gla, gate, ydiff, gn, w, lng, lnb)


def _rope_tables(t):
    rows = t // GRID_W
    row = jnp.repeat(jnp.arange(rows, dtype=F32), GRID_W)
    col = jnp.tile(jnp.arange(GRID_W, dtype=F32), rows)
    seg = DH_DIFF // 2
    inv = ROPE_BASE ** (-jnp.arange(0, seg, 2, dtype=F32) / seg)
    a_r = row[:, None] * inv
    a_c = col[:, None] * inv
    ang = jnp.concatenate([a_r, a_r, a_c, a_c], axis=-1)
    ang = jnp.concatenate([ang, ang], axis=-1)
    sign = jnp.where((jnp.arange(LANES) % 32) < 16, -1.0, 1.0).astype(F32)
    return jnp.cos(ang), jnp.sin(ang) * sign


def _ffn_weights(w_in, w_out, ck):
    d, two_ff = w_in.shape
    ff = two_ff // 2
    n = ff // ck
    win = w_in.reshape(d, 2, n, ck).transpose(2, 0, 1, 3).reshape(n, d, 2 * ck).astype(BF16)
    wout = w_out.reshape(n, ck, d).astype(BF16)
    return win, wout


def _pick_tile(t, pref):
    return pref if t % pref == 0 else t


def kernel(x_prompt, x_sample, cache_diff_k, cache_diff_v, state_gla, c, c_ctx, w_ada, b_ada, w_ffn1_in,
           w_ffn1_out, w_ffn2_in, w_ffn2_out, w_in, conv_w, conv_b, conv_ln_g, conv_ln_b, gla_w_a2, gla_b_a,
           gla_norm_g, diff_lam, diff_norm_g, w_out, ln_g, ln_b):
    depth, d, _ = w_ada.shape
    alpha = (2 * depth) ** 0.25
    n_dec = c.shape[0]
    c_conv = conv_w.shape[2]
    w_gla = H_GLA * DK_GLA
    w_diff = H_DIFF * 2 * DH_DIFF
    in_conv, in_gla = 2 * c_conv, 4 * w_gla + 2 * GLA_RANK
    ff_chunk = 256

    rows = -(-(n_dec + 1) // SUBLANES) * SUBLANES
    cvec = jnp.zeros((rows, d), F32).at[:n_dec].set(c).at[n_dec].set(c_ctx)
    mod = _ada(cvec, w_ada, b_ada).reshape(depth, rows, N_MOD, d)

    cos, sin = _rope_tables(x_sample.shape[1])

    layers = []
    for l in range(depth):
        wi = w_in[l]
        w_lr = jnp.pad(wi[:, in_conv + 4 * w_gla:in_conv + in_gla], ((0, 0), (0, LANES - 2 * GLA_RANK)))
        w_all = jnp.concatenate([wi[:, :in_conv + 4 * w_gla], wi[:, in_conv + in_gla:], w_lr], axis=1).astype(BF16)
        w2 = jnp.zeros((LANES, 2 * w_gla), F32)
        w2 = w2.at[:GLA_RANK, :w_gla].set(gla_w_a2[l, 0]).at[GLA_RANK:2 * GLA_RANK, w_gla:].set(gla_w_a2[l, 1])
        layers.append(dict(
            ffn1=_ffn_weights(w_ffn1_in[l], w_ffn1_out[l], ff_chunk),
            ffn2=_ffn_weights(w_ffn2_in[l], w_ffn2_out[l], ff_chunk),
            w_all=w_all, w2=w2.astype(BF16), b2=gla_b_a[l].reshape(1, 2 * w_gla),
            conv_w=conv_w[l], conv_b=conv_b[l][None], conv_g=conv_ln_g[l][None], conv_beta=conv_ln_b[l][None],
            gn=jnp.tile(gla_norm_g[l], H_GLA)[None], lam=diff_lam[l], dg=diff_norm_g[l][:, None],
            w_out=w_out[l].astype(BF16),
            lng=[ln_g[l, i][None] for i in range(3)], lnb=[ln_b[l, i][None] for i in range(3)],
            lam_init=0.8 - 0.6 * math.exp(-0.3 * l),
        ))

    def run_layer(x, mod_l, P, ctx):
        t = x.shape[1]
        tm = _pick_tile(t, 512)
        bsz, _, d_model = x.shape
        shared_mod = mod_l.shape[0] == 1
        tf = _pick_tile(bsz * t if shared_mod else t, 1024)

        def ffn(x, which, row0, ln_idx):
            y = _ffn(x.reshape(bsz * t, d_model), mod_l, *P[which], P['lng'][ln_idx], P['lnb'][ln_idx],
                     row0=row0, alpha=alpha, tm=tf)
            return y.reshape(bsz, t, d_model)

        x = ffn(x, 'ffn1', 0, 0)
        rope = ctx is not None
        yglu, gq, gk, gv, gate, gdec, dq, dk, dvt, *dv = _mixin(
            x, mod_l, P['w_all'], P['w2'], P['b2'], cos if rope else None, sin if rope else None, tm=tm)
        yconv = _conv(yglu, P['conv_w'], P['conv_b'], P['conv_g'], P['conv_beta'], tc=_pick_tile(t, 256))
        gla_out = _gla(gq, gk, gv, gdec, ctx['s0'] if rope else None, tb=_pick_tile(t, 512), want_state=not rope)
        ydiff = _attn(dq, dk, dvt, ctx['k'] if rope else None, ctx['vt'] if rope else None, P['lam'], P['dg'],
                      lam_init=P['lam_init'], tq=_pick_tile(t, 512))
        x = _mixout(x, mod_l, yconv, gla_out[0], gate, ydiff, P['gn'], P['w_out'], P['lng'][1], P['lnb'][1],
                    alpha=alpha, tm=tm)
        x = ffn(x, 'ffn2', 6, 2)
        return x, (dk, dv[0] if dv else None, gla_out[1] if not rope else None)

    xp = x_prompt
    new_k, new_v, new_s = [], [], []
    for l in range(depth):
        xp, (k_l, v_l, s_l) = run_layer(xp, mod[l, n_dec:n_dec + 1], layers[l], None)
        new_k.append(k_l)
        new_v.append(v_l)
        bsz = s_l.shape[0]
        s6 = s_l.reshape(bsz, 2, H_GLA, DK_GLA, H_GLA, DK_GLA)
        s_heads = jnp.stack([s6[:, :, h, :, h, :] for h in range(H_GLA)], axis=2)
        new_s.append(jnp.swapaxes(s_heads, -1, -2))

    xs = x_sample
    for l in range(depth):
        st = jnp.swapaxes(state_gla[:, l], -1, -2)
        s0 = jnp.zeros(st.shape[:2] + (H_GLA, DK_GLA, H_GLA, DK_GLA), F32)
        for h in range(H_GLA):
            s0 = s0.at[:, :, h, :, h, :].set(st[:, :, h])
        s0 = s0.reshape(st.shape[0], 2, w_gla, w_gla)
        past = cache_diff_v.shape[3]
        tile = _pick_tile(x_sample.shape[1], 512)
        assert past % tile == 0, "cached context length must be a whole number of key tiles"
        cvt = cache_diff_v[:, l].astype(BF16).reshape(n_dec, H_DIFF, past // tile, tile, LANES)
        cvt = jnp.concatenate([jnp.swapaxes(cvt, -1, -2), jnp.ones(cvt.shape[:3] + (ONES_ROWS, tile), BF16)], axis=3)
        ctx = dict(k=cache_diff_k[:, l].astype(BF16), vt=cvt, s0=s0)
        xs, _ = run_layer(xs, mod[l, :n_dec], layers[l], ctx)

    return (xp, xs, jnp.stack(new_k, axis=1), jnp.stack(new_v, axis=1), jnp.stack(new_s, axis=1))
```

```python
import functools
import math

import jax
import jax.numpy as jnp
from jax import lax
from jax.experimental import pallas as pl
from jax.experimental.pallas import tpu as pltpu

F32 = jnp.float32
BF16 = jnp.bfloat16

GRID_W = 64
CONV_WIDTH = 31
H_GLA = 4
DK_GLA = 64
GLA_RANK = 16
GLA_TAU = 16.0
GLA_CHUNK = 64
H_DIFF = 4
DH_DIFF = 64
ROPE_BASE = 10000.0
LN_EPS = 1e-5
N_MOD = 9

LANES = 128
SUBLANES = 8
V7X_VMEM_BYTES = 64 * 1024 * 1024
VMEM_LIMIT = V7X_VMEM_BYTES - 8 * 1024 * 1024

CONV_HALO = 16
CONV_ROWS = 32
ATTN_STRIP = 32
ATTN_SLOTS = 3
ONES_ROWS = 16
ROW_STRIP = 32
LN_STRIP = 64


def _cparams(sem):
    return pltpu.CompilerParams(dimension_semantics=sem, vmem_limit_bytes=VMEM_LIMIT)


def _const_spec(shape):
    nd = len(shape)
    return pl.BlockSpec(shape, lambda *_: (0,) * nd, pipeline_mode=pl.Buffered(1))


def _dot(a, b):
    return jnp.dot(a, b, preferred_element_type=F32)


def _dot_nt(a, b):
    return lax.dot_general(a, b, (((1,), (1,)), ((), ())), preferred_element_type=F32)


def _dot_tn(a, b):
    return lax.dot_general(a, b, (((0,), (0,)), ((), ())), preferred_element_type=F32)


def _silu(x):
    return x * jax.nn.sigmoid(x)


def _layer_norm(y, g, b):
    mu = jnp.mean(y, axis=-1, keepdims=True)
    yc = y - mu
    var = jnp.mean(yc * yc, axis=-1, keepdims=True)
    return yc * lax.rsqrt(var + LN_EPS) * g + b


def _modulate(x, mod_ref, row0):
    sh = mod_ref[0, row0:row0 + 1, :]
    sc = mod_ref[0, row0 + 1:row0 + 2, :]
    g = mod_ref[0, row0 + 2:row0 + 3, :]
    return x * (1.0 + sc) + sh, g


def _ada_kernel(c_ref, w_ref, b_ref, o_ref):
    s = _silu(c_ref[...]).astype(BF16)
    o_ref[0] = _dot(s, w_ref[0].astype(BF16)) + b_ref[0]


def _ada(cvec, w_ada, b_ada):
    depth, d, n = w_ada.shape
    rows = cvec.shape[0]
    tn = d
    return pl.pallas_call(
        _ada_kernel,
        grid=(depth, n // tn),
        in_specs=[
            pl.BlockSpec((rows, d), lambda l, j: (0, 0)),
            pl.BlockSpec((1, d, tn), lambda l, j: (l, 0, j)),
            pl.BlockSpec((1, 1, tn), lambda l, j: (l, 0, j)),
        ],
        out_specs=pl.BlockSpec((1, rows, tn), lambda l, j: (l, 0, j)),
        out_shape=jax.ShapeDtypeStruct((depth, rows, n), F32),
        compiler_params=_cparams(("arbitrary", "arbitrary")),
        name="ada_mod",
    )(cvec, w_ada, b_ada.reshape(depth, 1, n))


def _strips(n_rows, fn):
    def body(i, carry):
        fn(pl.ds(pl.multiple_of(i * ROW_STRIP, ROW_STRIP), ROW_STRIP))
        return carry

    lax.fori_loop(0, n_rows // ROW_STRIP, body, 0, unroll=2)


def _layer_norm_rows(n_rows, make_y, o_ref, g, b):
    n = n_rows // LN_STRIP
    sl = lambda i: pl.ds(i * LN_STRIP if isinstance(i, int) else pl.multiple_of(i * LN_STRIP, LN_STRIP), LN_STRIP)

    def means(i):
        y = make_y(sl(i))
        o_ref[sl(i), :] = y
        return jnp.mean(y, axis=-1, keepdims=True)

    def rstd(i, mu):
        yc = o_ref[sl(i), :] - mu
        return lax.rsqrt(jnp.mean(yc * yc, axis=-1, keepdims=True) + LN_EPS)

    def normalise(i, mu, r):
        o_ref[sl(i), :] = (o_ref[sl(i), :] - mu) * r * g + b

    if n == 1:
        mu = means(0)
        normalise(0, mu, rstd(0, mu))
        return
    mu_a = means(0)
    mu_b = means(1)
    r_a = rstd(0, mu_a)

    def body(i, carry):
        mu_a, r_a, mu_b = carry
        r_b = rstd(i - 1, mu_b)
        mu_c = means(i)
        normalise(i - 2, mu_a, r_a)
        return mu_b, r_b, mu_c

    mu_a, r_a, mu_b = lax.fori_loop(2, n, body, (mu_a, r_a, mu_b))
    r_b = rstd(n - 1, mu_b)
    normalise(n - 2, mu_a, r_a)
    normalise(n - 1, mu_b, r_b)


def _ffn_kernel(x_ref, mod_ref, win_ref, wout_ref, lng_ref, lnb_ref, o_ref, h_scr, acc_ref, y_scr, *, row0, alpha):
    i = pl.program_id(0)
    n_tiles = pl.num_programs(0) - 1
    tm = x_ref.shape[0]
    n_chunks, _, two_ck = win_ref.shape
    ck = two_ck // 2
    lng = lng_ref[...]
    lnb = lnb_ref[...]
    n_strips = tm // LN_STRIP
    unroll = 2 if n_chunks % 2 else 1
    n_iters = (n_chunks - 1) // unroll
    per_iter = -(-n_strips // (n_iters + 1))

    def norm_strip(idx):
        idx = jnp.minimum(idx, n_strips - 1)
        rs = pl.ds(pl.multiple_of(idx * LN_STRIP, LN_STRIP), LN_STRIP)
        o_ref[rs, :] = _layer_norm(y_scr[rs, :], lng, lnb)

    @pl.when(i == 0)
    def _():
        y_scr[...] = jnp.zeros_like(y_scr)

    @pl.when(i < n_tiles)
    def _():
        sh = mod_ref[0, row0:row0 + 1, :]
        sc1 = 1.0 + mod_ref[0, row0 + 1:row0 + 2, :]
        half_g = 0.5 * mod_ref[0, row0 + 2:row0 + 3, :]

        def modulate(rs):
            h_scr[rs, :] = (x_ref[rs, :] * sc1 + sh).astype(BF16)

        _strips(tm, modulate)

        def chunk(j):
            u = _dot(h_scr[...], win_ref[j])
            act = (_silu(u[:, :ck]) * u[:, ck:]).astype(BF16)
            return _dot(act, wout_ref[j])

        acc_ref[...] = chunk(0)
        for s in range(per_iter):
            norm_strip(s)

        def body(k, carry):
            for u in range(unroll):
                acc_ref[...] += chunk(1 + k * unroll + u)
            for s in range(per_iter):
                norm_strip((k + 1) * per_iter + s)
            return carry

        lax.fori_loop(0, n_iters, body, 0)

        def residual(rs):
            y_scr[rs, :] = alpha * x_ref[rs, :] + half_g * acc_ref[rs, :]

        _strips(tm, residual)

    @pl.when(i == n_tiles)
    def _():
        _layer_norm_rows(tm, lambda rs: y_scr[rs, :], o_ref, lng, lnb)


def _ffn(x, mod, win, wout, lng, lnb, *, row0, alpha, tm):
    n_tok, d = x.shape
    n_tiles = n_tok // tm
    per_req = n_tok // mod.shape[0]
    cur = lambda ti: jnp.minimum(ti, n_tiles - 1)
    kern = functools.partial(_ffn_kernel, row0=row0, alpha=alpha)
    return pl.pallas_call(
        kern,
        grid=(n_tiles + 1,),
        in_specs=[
            pl.BlockSpec((tm, d), lambda ti: (cur(ti), 0)),
            pl.BlockSpec((1, N_MOD, d), lambda ti: ((cur(ti) * tm) // per_req, 0, 0)),
            _const_spec(win.shape),
            _const_spec(wout.shape),
            _const_spec((1, d)),
            _const_spec((1, d)),
        ],
        out_specs=pl.BlockSpec((tm, d), lambda ti: (jnp.maximum(ti - 1, 0), 0)),
        out_shape=jax.ShapeDtypeStruct((n_tok, d), F32),
        scratch_shapes=[pltpu.VMEM((tm, d), BF16), pltpu.VMEM((tm, d), F32), pltpu.VMEM((tm, d), F32)],
        compiler_params=_cparams(("arbitrary",)),
        name="ffn",
    )(x, mod, win, wout, lng, lnb)


def _log_sigmoid(x):
    return jnp.minimum(x, 0.0) - jnp.log(1.0 + jnp.exp(-jnp.abs(x)))


def _rope(x, cos, sin_signed, first_half):
    rot = jnp.where(first_half, pltpu.roll(x, LANES - 16, 1), pltpu.roll(x, 16, 1))
    return x * cos + rot * sin_signed


def _mixin_kernel(*refs, rope, c_conv, w_gla, w_diff):
    if rope:
        (x_ref, mod_ref, w_ref, w2_ref, b2_ref, cos_ref, sin_ref,
         yglu_ref, gq_ref, gk_ref, gv_ref, gate_ref, gdec_ref, dq_ref, dk_ref, dvt_ref) = refs
    else:
        (x_ref, mod_ref, w_ref, w2_ref, b2_ref,
         yglu_ref, gq_ref, gk_ref, gv_ref, gate_ref, gdec_ref, dq_ref, dk_ref, dvt_ref, dv_ref) = refs
    x = x_ref[0]
    h, _ = _modulate(x, mod_ref, 3)
    h = h.astype(BF16)
    o_gla = 2 * c_conv
    o_diff = o_gla + 4 * w_gla
    o_lr = o_diff + 3 * w_diff

    zc = _dot(h, w_ref[:, 0:o_gla])
    yglu_ref[0] = zc[:, :c_conv] * jax.nn.sigmoid(zc[:, c_conv:])

    zg = _dot(h, w_ref[:, o_gla:o_diff])
    gq_ref[0] = zg[:, 0:w_gla] * (DK_GLA ** -0.5)
    gk_ref[0] = zg[:, w_gla:2 * w_gla]
    gv_ref[0] = zg[:, 2 * w_gla:3 * w_gla].astype(BF16)
    gate_ref[0] = zg[:, 3 * w_gla:4 * w_gla]

    zl = _dot(h, w_ref[:, o_lr:o_lr + LANES]).astype(BF16)
    pre = _dot(zl, w2_ref[...]) + b2_ref[...]
    gdec_ref[0] = _log_sigmoid(pre) * (1.0 / GLA_TAU)

    zd = _dot(h, w_ref[:, o_diff:o_lr])
    if rope:
        cos = cos_ref[...]
        sin = sin_ref[...]
        lane = lax.broadcasted_iota(jnp.int32, cos.shape, 1)
        first_half = (lane % 32) < 16
    for hd in range(H_DIFF):
        sl = slice(hd * LANES, (hd + 1) * LANES)
        q = zd[:, sl]
        k = zd[:, w_diff + hd * LANES:w_diff + (hd + 1) * LANES]
        v = zd[:, 2 * w_diff + hd * LANES:2 * w_diff + (hd + 1) * LANES]
        if rope:
            q = _rope(q, cos, sin, first_half)
            k = _rope(k, cos, sin, first_half)
        dq_ref[0, :, sl] = (q * (DH_DIFF ** -0.5 * math.log2(math.e))).astype(BF16)
        dk_ref[0, hd] = k.astype(dk_ref.dtype)
        dvt_ref[0, hd, 0, 0:LANES, :] = v.T.astype(BF16)
        dvt_ref[0, hd, 0, LANES:, :] = jnp.ones((ONES_ROWS, v.shape[0]), BF16)
        if not rope:
            dv_ref[0, hd] = v


def _mixin(x, mod, w, w2, b2, cos, sin, *, tm):
    b, t, d = x.shape
    rope = cos is not None
    c_conv = 256
    w_gla = H_GLA * DK_GLA
    w_diff = H_DIFF * 2 * DH_DIFF
    kv_dtype = BF16 if rope else F32
    mod_rows = mod.shape[0]
    mod_idx = (lambda bi, ti: (bi, 0, 0)) if mod_rows > 1 else (lambda bi, ti: (0, 0, 0))
    tok = lambda n: pl.BlockSpec((1, tm, n), lambda bi, ti: (bi, ti, 0))
    heads = pl.BlockSpec((1, H_DIFF, tm, LANES), lambda bi, ti: (bi, 0, ti, 0))
    in_specs = [tok(d), pl.BlockSpec((1, N_MOD, d), mod_idx),
                _const_spec(w.shape), _const_spec(w2.shape), _const_spec(b2.shape)]
    args = [x, mod, w, w2, b2]
    if rope:
        in_specs += [pl.BlockSpec((tm, LANES), lambda bi, ti: (ti, 0))] * 2
        args += [cos, sin]
    sds = jax.ShapeDtypeStruct
    out_shape = [
        sds((b, t, c_conv), F32),
        sds((b, t, w_gla), F32),
        sds((b, t, w_gla), F32),
        sds((b, t, w_gla), BF16),
        sds((b, t, w_gla), F32),
        sds((b, t, 2 * w_gla), F32),
        sds((b, t, w_diff), BF16),
        sds((b, H_DIFF, t, LANES), kv_dtype),
        sds((b, H_DIFF, t // tm, LANES + ONES_ROWS, tm), BF16),
    ]
    out_specs = [tok(c_conv), tok(w_gla), tok(w_gla), tok(w_gla), tok(w_gla), tok(2 * w_gla),
                 tok(w_diff), heads,
                 pl.BlockSpec((1, H_DIFF, 1, LANES + ONES_ROWS, tm), lambda bi, ti: (bi, 0, ti, 0, 0))]
    if not rope:
        out_shape.append(sds((b, H_DIFF, t, LANES), F32))
        out_specs.append(heads)
    kern = functools.partial(_mixin_kernel, rope=rope, c_conv=c_conv, w_gla=w_gla, w_diff=w_diff)
    return pl.pallas_call(
        kern,
        grid=(b, t // tm),
        in_specs=in_specs,
        out_specs=out_specs,
        out_shape=out_shape,
        compiler_params=_cparams(("parallel", "parallel")),
        name="mixer_in",
    )(*args)


def _conv_kernel(prev_ref, cur_ref, next_ref, w_ref, b_ref, g_ref, beta_ref, o_ref, win_ref, *, tc, nt):
    i = pl.program_id(1)
    pad = CONV_WIDTH // 2
    win_ref[0:CONV_HALO] = jnp.where(i > 0, prev_ref[0], 0.0)
    win_ref[CONV_HALO:CONV_HALO + tc] = cur_ref[0]
    win_ref[CONV_HALO + tc:2 * CONV_HALO + tc] = jnp.where(i < nt - 1, next_ref[0], 0.0)
    w = w_ref[...]
    first = CONV_HALO - pad
    n_groups = -(-(first + CONV_WIDTH) // SUBLANES)
    for r in range(0, tc, CONV_ROWS):
        wide = win_ref[r:r + CONV_ROWS + n_groups * SUBLANES, :]
        acc = None
        for s in range(SUBLANES):
            z = None
            for a in range(n_groups):
                k = a * SUBLANES + s - first
                if 0 <= k < CONV_WIDTH:
                    term = wide[a * SUBLANES:a * SUBLANES + CONV_ROWS + SUBLANES, :] * w[k:k + 1, :]
                    z = term if z is None else z + term
            if z is not None:
                acc = z[s:s + CONV_ROWS, :] if acc is None else acc + z[s:s + CONV_ROWS, :]
        y = _layer_norm(acc + b_ref[...], g_ref[...], beta_ref[...])
        o_ref[0, r:r + CONV_ROWS, :] = _silu(y).astype(o_ref.dtype)


def _conv(yglu, w, bias, g, beta, *, tc):
    b, t, c = yglu.shape
    nt = t // tc
    hb = tc // CONV_HALO
    n_halo = t // CONV_HALO
    kern = functools.partial(_conv_kernel, tc=tc, nt=nt)
    return pl.pallas_call(
        kern,
        grid=(b, nt),
        in_specs=[
            pl.BlockSpec((1, CONV_HALO, c), lambda bi, i: (bi, jnp.maximum(i * hb - 1, 0), 0)),
            pl.BlockSpec((1, tc, c), lambda bi, i: (bi, i, 0)),
            pl.BlockSpec((1, CONV_HALO, c), lambda bi, i: (bi, jnp.minimum((i + 1) * hb, n_halo - 1), 0)),
            _const_spec(w.shape), _const_spec((1, c)), _const_spec((1, c)), _const_spec((1, c)),
        ],
        out_specs=pl.BlockSpec((1, tc, c), lambda bi, i: (bi, i, 0)),
        out_shape=jax.ShapeDtypeStruct((b, t, c), BF16),
        scratch_shapes=[pltpu.VMEM((tc + 2 * CONV_HALO, c), F32)],
        compiler_params=_cparams(("parallel", "parallel")),
        name="conv_module",
    )(yglu, yglu, yglu, w, bias, g, beta)


def _split3(x):
    hi = x.astype(BF16)
    r1 = x - hi.astype(F32)
    mid = r1.astype(BF16)
    lo = (r1 - mid.astype(F32)).astype(BF16)
    return hi, mid, lo


def _gla_kernel(*refs, nc, has_state, want_state):
    it = iter(refs)
    q_ref, k_ref, v_ref, g_ref = next(it), next(it), next(it), next(it)
    s0_ref = next(it) if has_state else None
    o_ref = next(it)
    sfin_ref = next(it) if want_state else None
    s_scr, qt_scr, kv_scr, dec_scr, sin_scr = next(it), next(it), next(it), next(it), next(it)

    d = pl.program_id(1)
    i = pl.program_id(2)
    nblk = pl.num_programs(2)
    c = GLA_CHUNK
    width = s_scr.shape[0]
    nh = width // c

    @pl.when(i == 0)
    def _():
        if has_state:
            s_scr[...] = s0_ref[0, 0]
        else:
            s_scr[...] = jnp.zeros_like(s_scr)

    fwd = d == 0
    row = lax.broadcasted_iota(jnp.int32, (c, c), 0)
    col = lax.broadcasted_iota(jnp.int32, (c, c), 1)
    sgn = jnp.where(fwd, 1, -1)
    tri = (row - col) * sgn >= 0
    tri_b = jnp.where(tri, 1.0, 0.0).astype(BF16)
    rowa = lax.broadcasted_iota(jnp.int32, (c, width), 0)
    cola = lax.broadcasted_iota(jnp.int32, (c, width), 1) % c
    tri_cat = (rowa - cola) * sgn >= 0
    rb = lax.broadcasted_iota(jnp.int32, (width, width), 0) // c
    cb = lax.broadcasted_iota(jnp.int32, (width, width), 1) // c
    bd = rb == cb

    rows = [slice(ci * c, (ci + 1) * c) for ci in range(nc)]
    gcums = []
    for rs in rows:
        ghi, gmid, glo = _split3(g_ref[0, rs, :])
        gcums.append(_dot(tri_b, ghi) + _dot(tri_b, gmid) + _dot(tri_b, glo))
    scores = []
    for ci, (rs, gcum) in enumerate(zip(rows, gcums)):
        q = q_ref[0, rs, :]
        k = k_ref[0, rs, :]
        gtot = jnp.where(fwd, gcum[c - 1:c, :], gcum[0:1, :])
        q_t = (q * jnp.exp(gcum)).astype(BF16)
        k_t = k * jnp.exp(-gcum)
        k_hat = (k * jnp.exp(gtot - gcum)).astype(BF16)
        k_bd = jnp.where(bd, jnp.concatenate([k_t] * nh, axis=0), 0.0).astype(BF16)
        scores.append(_dot_nt(q_t, k_bd))
        qt_scr[rs, :] = q_t
        kv_scr[ci] = jnp.where(bd, _dot_tn(v_ref[0, rs, :], k_hat), 0.0)
        dec_scr[ci] = jnp.broadcast_to(jnp.exp(gtot), dec_scr.shape[1:])
    for rs, a in zip(rows, scores):
        v = v_ref[0, rs, :]
        a = jnp.where(tri_cat, a, 0.0).astype(BF16)
        v_bd = jnp.where(bd, jnp.concatenate([v] * nh, axis=0), jnp.zeros((), v.dtype))
        o_ref[0, 0, rs, :] = _dot(a, v_bd)

    order = [jnp.where(fwd, j, nc - 1 - j) for j in range(nc)]
    for j, cidx in enumerate(order):
        s_in = s_scr[...]
        sin_scr[j] = s_in.astype(BF16)
        s_scr[...] = s_in * dec_scr[cidx, 0:1, :] + kv_scr[cidx]
    for j, cidx in enumerate(order):
        rs = pl.ds(pl.multiple_of(cidx * c, c), c)
        o_ref[0, 0, rs, :] += _dot_nt(qt_scr[rs, :], sin_scr[j])

    if want_state:
        @pl.when(i == nblk - 1)
        def _():
            sfin_ref[0, 0] = s_scr[...]


def _gla(q, k, v, gdec, s0, *, tb, want_state):
    b, t, w = q.shape
    nblk = t // tb
    nc = tb // GLA_CHUNK
    blk = lambda d, i: i + d * (nblk - 1 - 2 * i)
    tok = pl.BlockSpec((1, tb, w), lambda bi, d, i: (bi, blk(d, i), 0))
    in_specs = [tok, tok, tok, pl.BlockSpec((1, tb, w), lambda bi, d, i: (bi, blk(d, i), d))]
    args = [q, k, v, gdec]
    has_state = s0 is not None
    if has_state:
        in_specs.append(pl.BlockSpec((1, 1, w, w), lambda bi, d, i: (bi, d, 0, 0)))
        args.append(s0)
    out_shape = [jax.ShapeDtypeStruct((2, b, t, w), F32)]
    out_specs = [pl.BlockSpec((1, 1, tb, w), lambda bi, d, i: (d, bi, blk(d, i), 0))]
    if want_state:
        out_shape.append(jax.ShapeDtypeStruct((b, 2, w, w), F32))
        out_specs.append(pl.BlockSpec((1, 1, w, w), lambda bi, d, i: (bi, d, 0, 0)))
    kern = functools.partial(_gla_kernel, nc=nc, has_state=has_state, want_state=want_state)
    return pl.pallas_call(
        kern,
        grid=(b, 2, nblk),
        in_specs=in_specs,
        out_specs=out_specs,
        out_shape=out_shape,
        scratch_shapes=[pltpu.VMEM((w, w), F32), pltpu.VMEM((tb, w), BF16),
                        pltpu.VMEM((nc, w, w), F32), pltpu.VMEM((nc, SUBLANES, w), F32),
                        pltpu.VMEM((nc, w, w), BF16)],
        compiler_params=_cparams(("parallel", "parallel", "arbitrary")),
        name="gla",
    )(*args)


def _attn_kernel(*refs, has_cache, lam_init, n_slots):
    it = iter(refs)
    q_ref, k_ref, vt_ref = next(it), next(it), next(it)
    kc_ref, vtc_ref = (next(it), next(it)) if has_cache else (None, None)
    lam_ref, g_ref, o_ref = next(it), next(it), next(it)
    s_scr = [next(it) for _ in range(n_slots)]
    e_scr = [next(it) for _ in range(n_slots)]
    acc_scr = next(it)
    p = lam_ref[...]
    lam = (jnp.exp(jnp.sum(p[0:1] * p[1:2], axis=-1, keepdims=True))
           - jnp.exp(jnp.sum(p[2:3] * p[3:4], axis=-1, keepdims=True)) + lam_init)

    q = q_ref[0]
    tq = q.shape[0]
    lane = lax.broadcasted_iota(jnp.int32, q.shape, 1)
    zero = jnp.zeros((), q.dtype)
    qs = jnp.concatenate([jnp.where(lane < DH_DIFF, q, zero), jnp.where(lane >= DH_DIFF, q, zero)], axis=0)

    n_new, tk = vt_ref.shape[2], vt_ref.shape[4]
    n = n_new + (vtc_ref.shape[2] if has_cache else 0)
    nq = 2 * tq
    acc_scr[...] = jnp.zeros_like(acc_scr)

    def key_tile(j):
        ref, jj = (k_ref, j) if j < n_new else (kc_ref, j - n_new)
        return ref[0, 0, jj * tk:(jj + 1) * tk, :].astype(BF16)

    def value_tile(j):
        return vt_ref[0, 0, j] if j < n_new else vtc_ref[0, 0, j - n_new]

    def scores(j, slot):
        s_scr[slot][...] = _dot_nt(key_tile(j), qs)
        strips = [s_scr[slot][r:r + ATTN_STRIP, :] for r in range(0, tk, ATTN_STRIP)]
        return jnp.max(functools.reduce(jnp.maximum, strips), axis=0, keepdims=True)

    def softmax(slot, mt, m):
        m_new = jnp.maximum(m, mt)
        alpha = jnp.exp2(m - m_new)
        for r in range(0, tk, ATTN_STRIP):
            e_scr[slot][r:r + ATTN_STRIP, :] = jnp.exp2(s_scr[slot][r:r + ATTN_STRIP, :] - m_new).astype(BF16)
        return m_new, alpha

    def weighted(j, slot, alpha):
        acc_scr[...] = acc_scr[...] * alpha + _dot(value_tile(j), e_scr[slot][...])

    slot = lambda j: j % n_slots

    def step(j, mt, m, alpha_prev, last=False):
        mt_next = mt if last else scores(j + 1, slot(j + 1))
        m, alpha = softmax(slot(j), mt, m)
        weighted(j - 1, slot(j - 1), alpha_prev)
        return mt_next, m, alpha

    m = jnp.full((1, nq), -1e30, F32)
    mt = scores(0, slot(0))
    if n == 1:
        m, alpha = softmax(slot(0), mt, m)
    else:
        mt_next = scores(1, slot(1))
        m, alpha = softmax(slot(0), mt, m)
        mt = mt_next
        for j in range(1, n - 1):
            mt, m, alpha = step(j, mt, m, alpha)
        mt, m, alpha = step(n - 1, mt, m, alpha, last=True)
    weighted(n - 1, slot(n - 1), alpha)
    dv = o_ref.shape[2]
    on = acc_scr[0:dv, :] / acc_scr[dv:dv + 1, :]
    o = on[:, :tq] - lam * on[:, tq:]
    ms = jnp.mean(o * o, axis=0, keepdims=True)
    y = o * lax.rsqrt(ms + LN_EPS) * g_ref[...] * (1.0 - lam_init)
    o_ref[0] = y.T.astype(o_ref.dtype)


def _attn(dq, dk, dvt, ck, cvt, lam_p, g_col, *, lam_init, tq):
    b, t, _ = dq.shape
    dv, tk = dvt.shape[3:]
    qspec = pl.BlockSpec((1, tq, LANES), lambda bi, h, qi: (bi, qi, h))
    kspec = lambda a: pl.BlockSpec((1, 1) + a.shape[2:], lambda bi, h, qi: (bi, h, 0, 0))
    vspec = lambda a: pl.BlockSpec((1, 1) + a.shape[2:], lambda bi, h, qi: (bi, h, 0, 0, 0))
    has_cache = ck is not None
    n_tiles = dvt.shape[2] + (cvt.shape[2] if has_cache else 0)
    n_slots = min(n_tiles, ATTN_SLOTS)
    kv_args = [dk, dvt] + ([ck, cvt] if has_cache else [])
    kv_specs = [kspec(dk), vspec(dvt)] + ([kspec(ck), vspec(cvt)] if has_cache else [])
    kern = functools.partial(_attn_kernel, has_cache=has_cache, lam_init=lam_init, n_slots=n_slots)
    return pl.pallas_call(
        kern,
        grid=(b, H_DIFF, t // tq),
        in_specs=[qspec] + kv_specs + [_const_spec(lam_p.shape), _const_spec(g_col.shape)],
        out_specs=qspec,
        out_shape=jax.ShapeDtypeStruct(dq.shape, BF16),
        scratch_shapes=([pltpu.VMEM((tk, 2 * tq), F32)] * n_slots + [pltpu.VMEM((tk, 2 * tq), BF16)] * n_slots
                        + [pltpu.VMEM((dv, 2 * tq), F32)]),
        compiler_params=_cparams(("parallel", "parallel", "parallel")),
        name="diff_attn",
    )(dq, *kv_args, lam_p, g_col)


def _mixout_kernel(x_ref, mod_ref, yc_ref, of_ref, ob_ref, gate_ref, yd_ref, gn_ref, w_ref,
                   lng_ref, lnb_ref, o_ref, *, alpha):
    x = x_ref[0]
    g1 = mod_ref[0, 5:6, :]
    o = of_ref[0, 0] + ob_ref[0, 0]
    width = o.shape[1]
    rb = lax.broadcasted_iota(jnp.int32, (width, width), 0) // DK_GLA
    cb = lax.broadcasted_iota(jnp.int32, (width, width), 1) // DK_GLA
    ones_bd = jnp.where(rb == cb, 1.0, 0.0).astype(BF16)
    sq = o * o
    hi = sq.astype(BF16)
    lo = (sq - hi.astype(F32)).astype(BF16)
    ms = (_dot(hi, ones_bd) + _dot(lo, ones_bd)) * (1.0 / DK_GLA)
    yg = (o * lax.rsqrt(ms + LN_EPS) * gn_ref[...] * _silu(gate_ref[0])).astype(BF16)
    c_conv = yc_ref.shape[2]
    y = (_dot(yc_ref[0], w_ref[0:c_conv, :])
         + _dot(yg, w_ref[c_conv:c_conv + width, :])
         + _dot(yd_ref[0], w_ref[c_conv + width:, :]))
    o_ref[0] = _layer_norm(alpha * x + g1 * y, lng_ref[...], lnb_ref[...])


def _mixout(x, mod, yconv, o_gla, gate, ydiff, gn, w, lng, lnb, *, alpha, tm):
    b, t, d = x.shape
    mod_rows = mod.shape[0]
    mod_idx = (lambda bi, ti: (bi, 0, 0)) if mod_rows > 1 else (lambda bi, ti: (0, 0, 0))
    tok = lambda n: pl.BlockSpec((1, tm, n), lambda bi, ti: (bi, ti, 0))
    wg = o_gla.shape[3]
    kern = functools.partial(_mixout_kernel, alpha=alpha)
    return pl.pallas_call(
        kern,
        grid=(b, t // tm),
        in_specs=[
            tok(d), pl.BlockSpec((1, N_MOD, d), mod_idx), tok(yconv.shape[2]),
            pl.BlockSpec((1, 1, tm, wg), lambda bi, ti: (0, bi, ti, 0)),
            pl.BlockSpec((1, 1, tm, wg), lambda bi, ti: (1, bi, ti, 0)),
            tok(wg), tok(ydiff.shape[2]),
            _const_spec(gn.shape), _const_spec(w.shape), _const_spec((1, d)), _const_spec((1, d)),
        ],
        out_specs=tok(d),
        out_shape=jax.ShapeDtypeStruct((b, t, d), F32),
        compiler_params=_cparams(("parallel", "parallel")),
        name="mixer_out",
    )(x, mod, yconv, o_gla, o_gla, gate, ydiff, gn, w, lng, lnb)


def _rope_tables(t):
    rows = t // GRID_W
    row = jnp.repeat(jnp.arange(rows, dtype=F32), GRID_W)
    col = jnp.tile(jnp.arange(GRID_W, dtype=F32), rows)
    seg = DH_DIFF // 2
    inv = ROPE_BASE ** (-jnp.arange(0, seg, 2, dtype=F32) / seg)
    a_r = row[:, None] * inv
    a_c = col[:, None] * inv
    ang = jnp.concatenate([a_r, a_r, a_c, a_c], axis=-1)
    ang = jnp.concatenate([ang, ang], axis=-1)
    sign = jnp.where((jnp.arange(LANES) % 32) < 16, -1.0, 1.0).astype(F32)
    return jnp.cos(ang), jnp.sin(ang) * sign


def _ffn_weights(w_in, w_out, ck):
    d, two_ff = w_in.shape
    ff = two_ff // 2
    n = ff // ck
    win = w_in.reshape(d, 2, n, ck).transpose(2, 0, 1, 3).reshape(n, d, 2 * ck).astype(BF16)
    wout = w_out.reshape(n, ck, d).astype(BF16)
    return win, wout


def _pick_tile(t, pref):
    return pref if t % pref == 0 else t


def kernel(x_prompt, x_sample, cache_diff_k, cache_diff_v, state_gla, c, c_ctx, w_ada, b_ada, w_ffn1_in,
           w_ffn1_out, w_ffn2_in, w_ffn2_out, w_in, conv_w, conv_b, conv_ln_g, conv_ln_b, gla_w_a2, gla_b_a,
           gla_norm_g, diff_lam, diff_norm_g, w_out, ln_g, ln_b):
    depth, d, _ = w_ada.shape
    alpha = (2 * depth) ** 0.25
    n_dec = c.shape[0]
    c_conv = conv_w.shape[2]
    w_gla = H_GLA * DK_GLA
    w_diff = H_DIFF * 2 * DH_DIFF
    in_conv, in_gla = 2 * c_conv, 4 * w_gla + 2 * GLA_RANK
    ff_chunk = 256

    rows = -(-(n_dec + 1) // SUBLANES) * SUBLANES
    cvec = jnp.zeros((rows, d), F32).at[:n_dec].set(c).at[n_dec].set(c_ctx)
    mod = _ada(cvec, w_ada, b_ada).reshape(depth, rows, N_MOD, d)

    cos, sin = _rope_tables(x_sample.shape[1])

    layers = []
    for l in range(depth):
        wi = w_in[l]
        w_lr = jnp.pad(wi[:, in_conv + 4 * w_gla:in_conv + in_gla], ((0, 0), (0, LANES - 2 * GLA_RANK)))
        w_all = jnp.concatenate([wi[:, :in_conv + 4 * w_gla], wi[:, in_conv + in_gla:], w_lr], axis=1).astype(BF16)
        w2 = jnp.zeros((LANES, 2 * w_gla), F32)
        w2 = w2.at[:GLA_RANK, :w_gla].set(gla_w_a2[l, 0]).at[GLA_RANK:2 * GLA_RANK, w_gla:].set(gla_w_a2[l, 1])
        layers.append(dict(
            ffn1=_ffn_weights(w_ffn1_in[l], w_ffn1_out[l], ff_chunk),
            ffn2=_ffn_weights(w_ffn2_in[l], w_ffn2_out[l], ff_chunk),
            w_all=w_all, w2=w2.astype(BF16), b2=gla_b_a[l].reshape(1, 2 * w_gla),
            conv_w=conv_w[l], conv_b=conv_b[l][None], conv_g=conv_ln_g[l][None], conv_beta=conv_ln_b[l][None],
            gn=jnp.tile(gla_norm_g[l], H_GLA)[None], lam=diff_lam[l], dg=diff_norm_g[l][:, None],
            w_out=w_out[l].astype(BF16),
            lng=[ln_g[l, i][None] for i in range(3)], lnb=[ln_b[l, i][None] for i in range(3)],
            lam_init=0.8 - 0.6 * math.exp(-0.3 * l),
        ))

    def run_layer(x, mod_l, P, ctx):
        t = x.shape[1]
        tm = _pick_tile(t, 512)
        bsz, _, d_model = x.shape
        shared_mod = mod_l.shape[0] == 1
        tf = _pick_tile(bsz * t if shared_mod else t, 1024)

        def ffn(x, which, row0, ln_idx):
            y = _ffn(x.reshape(bsz * t, d_model), mod_l, *P[which], P['lng'][ln_idx], P['lnb'][ln_idx],
                     row0=row0, alpha=alpha, tm=tf)
            return y.reshape(bsz, t, d_model)

        x = ffn(x, 'ffn1', 0, 0)
        rope = ctx is not None
        yglu, gq, gk, gv, gate, gdec, dq, dk, dvt, *dv = _mixin(
            x, mod_l, P['w_all'], P['w2'], P['b2'], cos if rope else None, sin if rope else None, tm=tm)
        yconv = _conv(yglu, P['conv_w'], P['conv_b'], P['conv_g'], P['conv_beta'], tc=_pick_tile(t, 256))
        gla_out = _gla(gq, gk, gv, gdec, ctx['s0'] if rope else None, tb=_pick_tile(t, 512), want_state=not rope)
        ydiff = _attn(dq, dk, dvt, ctx['k'] if rope else None, ctx['vt'] if rope else None, P['lam'], P['dg'],
                      lam_init=P['lam_init'], tq=_pick_tile(t, 512))
        x = _mixout(x, mod_l, yconv, gla_out[0], gate, ydiff, P['gn'], P['w_out'], P['lng'][1], P['lnb'][1],
                    alpha=alpha, tm=tm)
        x = ffn(x, 'ffn2', 6, 2)
        return x, (dk, dv[0] if dv else None, gla_out[1] if not rope else None)

    xp = x_prompt
    new_k, new_v, new_s = [], [], []
    for l in range(depth):
        xp, (k_l, v_l, s_l) = run_layer(xp, mod[l, n_dec:n_dec + 1], layers[l], None)
        new_k.append(k_l)
        new_v.append(v_l)
        bsz = s_l.shape[0]
        s6 = s_l.reshape(bsz, 2, H_GLA, DK_GLA, H_GLA, DK_GLA)
        s_heads = jnp.stack([s6[:, :, h, :, h, :] for h in range(H_GLA)], axis=2)
        new_s.append(jnp.swapaxes(s_heads, -1, -2))

    xs = x_sample
    for l in range(depth):
        st = jnp.swapaxes(state_gla[:, l], -1, -2)
        s0 = jnp.zeros(st.shape[:2] + (H_GLA, DK_GLA, H_GLA, DK_GLA), F32)
        for h in range(H_GLA):
            s0 = s0.at[:, :, h, :, h, :].set(st[:, :, h])
        s0 = s0.reshape(st.shape[0], 2, w_gla, w_gla)
        past = cache_diff_v.shape[3]
        tile = _pick_tile(x_sample.shape[1], 512)
        assert past % tile == 0, "cached context length must be a whole number of key tiles"
        cvt = cache_diff_v[:, l].astype(BF16).reshape(n_dec, H_DIFF, past // tile, tile, LANES)
        cvt = jnp.concatenate([jnp.swapaxes(cvt, -1, -2), jnp.ones(cvt.shape[:3] + (ONES_ROWS, tile), BF16)], axis=3)
        ctx = dict(k=cache_diff_k[:, l].astype(BF16), vt=cvt, s0=s0)
        xs, _ = run_layer(xs, mod[l, :n_dec], layers[l], ctx)

    return (xp, xs, jnp.stack(new_k, axis=1), jnp.stack(new_v, axis=1), jnp.stack(new_s, axis=1))
```

```python
import functools
import math

import jax
import jax.numpy as jnp
from jax import lax
from jax.experimental import pallas as pl
from jax.experimental.pallas import tpu as pltpu

F32 = jnp.float32
BF16 = jnp.bfloat16

GRID_W = 64
CONV_WIDTH = 31
H_GLA = 4
DK_GLA = 64
GLA_RANK = 16
GLA_TAU = 16.0
GLA_CHUNK = 64
H_DIFF = 4
DH_DIFF = 64
ROPE_BASE = 10000.0
LN_EPS = 1e-5
N_MOD = 9

LANES = 128
SUBLANES = 8
V7X_VMEM_BYTES = 64 * 1024 * 1024
VMEM_LIMIT = V7X_VMEM_BYTES - 8 * 1024 * 1024

CONV_HALO = 16
CONV_ROWS = 32
ATTN_STRIP = 32
ATTN_LAG = 1
ATTN_SLOTS = 3
ONES_ROWS = 16
ROW_STRIP = 32
LN_STRIP = 64


def _cparams(sem):
    return pltpu.CompilerParams(dimension_semantics=sem, vmem_limit_bytes=VMEM_LIMIT)


def _const_spec(shape):
    nd = len(shape)
    return pl.BlockSpec(shape, lambda *_: (0,) * nd, pipeline_mode=pl.Buffered(1))


def _dot(a, b):
    return jnp.dot(a, b, preferred_element_type=F32)


def _dot_nt(a, b):
    return lax.dot_general(a, b, (((1,), (1,)), ((), ())), preferred_element_type=F32)


def _dot_tn(a, b):
    return lax.dot_general(a, b, (((0,), (0,)), ((), ())), preferred_element_type=F32)


def _silu(x):
    return x * jax.nn.sigmoid(x)


def _layer_norm(y, g, b):
    mu = jnp.mean(y, axis=-1, keepdims=True)
    yc = y - mu
    var = jnp.mean(yc * yc, axis=-1, keepdims=True)
    return yc * lax.rsqrt(var + LN_EPS) * g + b


def _modulate(x, mod_ref, row0):
    sh = mod_ref[0, row0:row0 + 1, :]
    sc = mod_ref[0, row0 + 1:row0 + 2, :]
    g = mod_ref[0, row0 + 2:row0 + 3, :]
    return x * (1.0 + sc) + sh, g


def _ada_kernel(c_ref, w_ref, b_ref, o_ref):
    s = _silu(c_ref[...]).astype(BF16)
    o_ref[0] = _dot(s, w_ref[0].astype(BF16)) + b_ref[0]


def _ada(cvec, w_ada, b_ada):
    depth, d, n = w_ada.shape
    rows = cvec.shape[0]
    tn = d
    return pl.pallas_call(
        _ada_kernel,
        grid=(depth, n // tn),
        in_specs=[
            pl.BlockSpec((rows, d), lambda l, j: (0, 0)),
            pl.BlockSpec((1, d, tn), lambda l, j: (l, 0, j)),
            pl.BlockSpec((1, 1, tn), lambda l, j: (l, 0, j)),
        ],
        out_specs=pl.BlockSpec((1, rows, tn), lambda l, j: (l, 0, j)),
        out_shape=jax.ShapeDtypeStruct((depth, rows, n), F32),
        compiler_params=_cparams(("arbitrary", "arbitrary")),
        name="ada_mod",
    )(cvec, w_ada, b_ada.reshape(depth, 1, n))


def _strips(n_rows, fn):
    def body(i, carry):
        fn(pl.ds(pl.multiple_of(i * ROW_STRIP, ROW_STRIP), ROW_STRIP))
        return carry

    lax.fori_loop(0, n_rows // ROW_STRIP, body, 0, unroll=2)


def _layer_norm_rows(n_rows, make_y, o_ref, g, b):
    n = n_rows // LN_STRIP
    sl = lambda i: pl.ds(i * LN_STRIP if isinstance(i, int) else pl.multiple_of(i * LN_STRIP, LN_STRIP), LN_STRIP)

    def means(i):
        y = make_y(sl(i))
        o_ref[sl(i), :] = y
        return jnp.mean(y, axis=-1, keepdims=True)

    def rstd(i, mu):
        yc = o_ref[sl(i), :] - mu
        return lax.rsqrt(jnp.mean(yc * yc, axis=-1, keepdims=True) + LN_EPS)

    def normalise(i, mu, r):
        o_ref[sl(i), :] = (o_ref[sl(i), :] - mu) * r * g + b

    if n == 1:
        mu = means(0)
        normalise(0, mu, rstd(0, mu))
        return
    mu_a = means(0)
    mu_b = means(1)
    r_a = rstd(0, mu_a)

    def body(i, carry):
        mu_a, r_a, mu_b = carry
        r_b = rstd(i - 1, mu_b)
        mu_c = means(i)
        normalise(i - 2, mu_a, r_a)
        return mu_b, r_b, mu_c

    mu_a, r_a, mu_b = lax.fori_loop(2, n, body, (mu_a, r_a, mu_b))
    r_b = rstd(n - 1, mu_b)
    normalise(n - 2, mu_a, r_a)
    normalise(n - 1, mu_b, r_b)


def _ffn_kernel(x_ref, mod_ref, win_ref, wout_ref, lng_ref, lnb_ref, o_ref, h_scr, acc_ref, y_scr, *, row0, alpha):
    i = pl.program_id(0)
    n_tiles = pl.num_programs(0) - 1
    tm = x_ref.shape[0]
    n_chunks, ck, _ = wout_ref.shape
    lng = lng_ref[...]
    lnb = lnb_ref[...]
    n_strips = tm // LN_STRIP
    unroll = 2 if n_chunks % 2 else 1
    n_iters = (n_chunks - 1) // unroll
    per_iter = -(-n_strips // (n_iters + 1))

    def norm_strip(idx):
        idx = jnp.minimum(idx, n_strips - 1)
        rs = pl.ds(pl.multiple_of(idx * LN_STRIP, LN_STRIP), LN_STRIP)
        o_ref[rs, :] = _layer_norm(y_scr[rs, :], lng, lnb)

    @pl.when(i == 0)
    def _():
        y_scr[...] = jnp.zeros_like(y_scr)

    @pl.when(i < n_tiles)
    def _():
        sh = mod_ref[0, row0:row0 + 1, :]
        sc1 = 1.0 + mod_ref[0, row0 + 1:row0 + 2, :]
        half_g = 0.5 * mod_ref[0, row0 + 2:row0 + 3, :]

        def modulate(rs):
            h_scr[rs, :] = (x_ref[rs, :] * sc1 + sh).astype(BF16)

        _strips(tm, modulate)

        def chunk(j):
            cols = lambda start: pl.ds(pl.multiple_of(start, ck), ck)
            h = h_scr[...]
            a = _dot(h, win_ref[:, cols(j * ck)])
            b = _dot(h, win_ref[:, cols(n_chunks * ck + j * ck)])
            return _dot((_silu(a) * b).astype(BF16), wout_ref[j])

        acc_ref[...] = chunk(0)
        for s in range(per_iter):
            norm_strip(s)

        def body(k, carry):
            for u in range(unroll):
                acc_ref[...] += chunk(1 + k * unroll + u)
            for s in range(per_iter):
                norm_strip((k + 1) * per_iter + s)
            return carry

        lax.fori_loop(0, n_iters, body, 0)

        def residual(rs):
            y_scr[rs, :] = alpha * x_ref[rs, :] + half_g * acc_ref[rs, :]

        _strips(tm, residual)

    @pl.when(i == n_tiles)
    def _():
        _layer_norm_rows(tm, lambda rs: y_scr[rs, :], o_ref, lng, lnb)


def _ffn(x, mod, win, wout, lng, lnb, *, row0, alpha, tm):
    n_tok, d = x.shape
    n_tiles = n_tok // tm
    per_req = n_tok // mod.shape[0]
    cur = lambda ti: jnp.minimum(ti, n_tiles - 1)
    kern = functools.partial(_ffn_kernel, row0=row0, alpha=alpha)
    return pl.pallas_call(
        kern,
        grid=(n_tiles + 1,),
        in_specs=[
            pl.BlockSpec((tm, d), lambda ti: (cur(ti), 0)),
            pl.BlockSpec((1, N_MOD, d), lambda ti: ((cur(ti) * tm) // per_req, 0, 0)),
            _const_spec(win.shape),
            _const_spec(wout.shape),
            _const_spec((1, d)),
            _const_spec((1, d)),
        ],
        out_specs=pl.BlockSpec((tm, d), lambda ti: (jnp.maximum(ti - 1, 0), 0)),
        out_shape=jax.ShapeDtypeStruct((n_tok, d), F32),
        scratch_shapes=[pltpu.VMEM((tm, d), BF16), pltpu.VMEM((tm, d), F32), pltpu.VMEM((tm, d), F32)],
        compiler_params=_cparams(("arbitrary",)),
        name="ffn",
    )(x, mod, win, wout, lng, lnb)


def _log_sigmoid(x):
    return jnp.minimum(x, 0.0) - jnp.log(1.0 + jnp.exp(-jnp.abs(x)))


def _rope(x, cos, sin_signed, first_half):
    rot = jnp.where(first_half, pltpu.roll(x, LANES - 16, 1), pltpu.roll(x, 16, 1))
    return x * cos + rot * sin_signed


def _mixin_kernel(*refs, rope, c_conv, w_gla, w_diff):
    if rope:
        (x_ref, mod_ref, w_ref, w2_ref, b2_ref, cos_ref, sin_ref,
         yglu_ref, gq_ref, gk_ref, gv_ref, gate_ref, gdec_ref, dq_ref, dk_ref, dvt_ref) = refs
    else:
        (x_ref, mod_ref, w_ref, w2_ref, b2_ref,
         yglu_ref, gq_ref, gk_ref, gv_ref, gate_ref, gdec_ref, dq_ref, dk_ref, dvt_ref, dv_ref) = refs
    x = x_ref[0]
    h, _ = _modulate(x, mod_ref, 3)
    h = h.astype(BF16)
    o_gla = 2 * c_conv
    o_diff = o_gla + 4 * w_gla
    o_lr = o_diff + 3 * w_diff

    tm = x.shape[0]
    halves = [slice(0, tm // 2), slice(tm // 2, tm)] if tm % (2 * LANES) == 0 else [slice(0, tm)]
    zs = []
    for rs in halves:
        hh = h[rs]
        zs.append((_dot(hh, w_ref[:, 0:o_gla]), _dot(hh, w_ref[:, o_gla:o_diff]),
                   _dot(hh, w_ref[:, o_lr:o_lr + LANES]).astype(BF16), _dot(hh, w_ref[:, o_diff:o_lr])))
    pres = [_dot(z[2], w2_ref[...]) + b2_ref[...] for z in zs]

    for rs, (zc, zg, _, zd), pre in zip(halves, zs, pres):
        yglu_ref[0, rs, :] = zc[:, :c_conv] * jax.nn.sigmoid(zc[:, c_conv:])
        gq_ref[0, rs, :] = zg[:, 0:w_gla] * (DK_GLA ** -0.5)
        gk_ref[0, rs, :] = zg[:, w_gla:2 * w_gla]
        gv_ref[0, rs, :] = zg[:, 2 * w_gla:3 * w_gla].astype(BF16)
        gate_ref[0, rs, :] = zg[:, 3 * w_gla:4 * w_gla]
        gdec_ref[0, rs, :] = _log_sigmoid(pre) * (1.0 / GLA_TAU)
        if rope:
            cos = cos_ref[rs, :]
            sin = sin_ref[rs, :]
            lane = lax.broadcasted_iota(jnp.int32, cos.shape, 1)
            first_half = (lane % 32) < 16
        for hd in range(H_DIFF):
            sl = slice(hd * LANES, (hd + 1) * LANES)
            q = zd[:, sl]
            k = zd[:, w_diff + hd * LANES:w_diff + (hd + 1) * LANES]
            v = zd[:, 2 * w_diff + hd * LANES:2 * w_diff + (hd + 1) * LANES]
            if rope:
                q = _rope(q, cos, sin, first_half)
                k = _rope(k, cos, sin, first_half)
            dq_ref[0, rs, sl] = (q * (DH_DIFF ** -0.5 * math.log2(math.e))).astype(BF16)
            dk_ref[0, hd, rs, :] = k.astype(dk_ref.dtype)
            dvt_ref[0, hd, 0, 0:LANES, rs] = v.T.astype(BF16)
            if not rope:
                dv_ref[0, hd, rs, :] = v
    for hd in range(H_DIFF):
        dvt_ref[0, hd, 0, LANES:, :] = jnp.ones((ONES_ROWS, tm), BF16)


def _mixin(x, mod, w, w2, b2, cos, sin, *, tm):
    b, t, d = x.shape
    rope = cos is not None
    c_conv = 256
    w_gla = H_GLA * DK_GLA
    w_diff = H_DIFF * 2 * DH_DIFF
    kv_dtype = BF16 if rope else F32
    mod_rows = mod.shape[0]
    mod_idx = (lambda bi, ti: (bi, 0, 0)) if mod_rows > 1 else (lambda bi, ti: (0, 0, 0))
    tok = lambda n: pl.BlockSpec((1, tm, n), lambda bi, ti: (bi, ti, 0))
    heads = pl.BlockSpec((1, H_DIFF, tm, LANES), lambda bi, ti: (bi, 0, ti, 0))
    in_specs = [tok(d), pl.BlockSpec((1, N_MOD, d), mod_idx),
                _const_spec(w.shape), _const_spec(w2.shape), _const_spec(b2.shape)]
    args = [x, mod, w, w2, b2]
    if rope:
        in_specs += [pl.BlockSpec((tm, LANES), lambda bi, ti: (ti, 0))] * 2
        args += [cos, sin]
    sds = jax.ShapeDtypeStruct
    out_shape = [
        sds((b, t, c_conv), F32),
        sds((b, t, w_gla), F32),
        sds((b, t, w_gla), F32),
        sds((b, t, w_gla), BF16),
        sds((b, t, w_gla), F32),
        sds((b, t, 2 * w_gla), F32),
        sds((b, t, w_diff), BF16),
        sds((b, H_DIFF, t, LANES), kv_dtype),
        sds((b, H_DIFF, t // tm, LANES + ONES_ROWS, tm), BF16),
    ]
    out_specs = [tok(c_conv), tok(w_gla), tok(w_gla), tok(w_gla), tok(w_gla), tok(2 * w_gla),
                 tok(w_diff), heads,
                 pl.BlockSpec((1, H_DIFF, 1, LANES + ONES_ROWS, tm), lambda bi, ti: (bi, 0, ti, 0, 0))]
    if not rope:
        out_shape.append(sds((b, H_DIFF, t, LANES), F32))
        out_specs.append(heads)
    kern = functools.partial(_mixin_kernel, rope=rope, c_conv=c_conv, w_gla=w_gla, w_diff=w_diff)
    return pl.pallas_call(
        kern,
        grid=(b, t // tm),
        in_specs=in_specs,
        out_specs=out_specs,
        out_shape=out_shape,
        compiler_params=_cparams(("parallel", "parallel")),
        name="mixer_in",
    )(*args)


def _conv_kernel(prev_ref, cur_ref, next_ref, w_ref, b_ref, g_ref, beta_ref, o_ref, win_ref, *, tc, nt):
    i = pl.program_id(1)
    pad = CONV_WIDTH // 2
    win_ref[0:CONV_HALO] = jnp.where(i > 0, prev_ref[0], 0.0)
    win_ref[CONV_HALO:CONV_HALO + tc] = cur_ref[0]
    win_ref[CONV_HALO + tc:2 * CONV_HALO + tc] = jnp.where(i < nt - 1, next_ref[0], 0.0)
    w = w_ref[...]
    first = CONV_HALO - pad
    n_groups = -(-(first + CONV_WIDTH) // SUBLANES)
    for r in range(0, tc, CONV_ROWS):
        wide = win_ref[r:r + CONV_ROWS + n_groups * SUBLANES, :]
        acc = None
        for s in range(SUBLANES):
            z = None
            for a in range(n_groups):
                k = a * SUBLANES + s - first
                if 0 <= k < CONV_WIDTH:
                    term = wide[a * SUBLANES:a * SUBLANES + CONV_ROWS + SUBLANES, :] * w[k:k + 1, :]
                    z = term if z is None else z + term
            if z is not None:
                acc = z[s:s + CONV_ROWS, :] if acc is None else acc + z[s:s + CONV_ROWS, :]
        y = _layer_norm(acc + b_ref[...], g_ref[...], beta_ref[...])
        o_ref[0, r:r + CONV_ROWS, :] = _silu(y).astype(o_ref.dtype)


def _conv(yglu, w, bias, g, beta, *, tc):
    b, t, c = yglu.shape
    nt = t // tc
    hb = tc // CONV_HALO
    n_halo = t // CONV_HALO
    kern = functools.partial(_conv_kernel, tc=tc, nt=nt)
    return pl.pallas_call(
        kern,
        grid=(b, nt),
        in_specs=[
            pl.BlockSpec((1, CONV_HALO, c), lambda bi, i: (bi, jnp.maximum(i * hb - 1, 0), 0)),
            pl.BlockSpec((1, tc, c), lambda bi, i: (bi, i, 0)),
            pl.BlockSpec((1, CONV_HALO, c), lambda bi, i: (bi, jnp.minimum((i + 1) * hb, n_halo - 1), 0)),
            _const_spec(w.shape), _const_spec((1, c)), _const_spec((1, c)), _const_spec((1, c)),
        ],
        out_specs=pl.BlockSpec((1, tc, c), lambda bi, i: (bi, i, 0)),
        out_shape=jax.ShapeDtypeStruct((b, t, c), BF16),
        scratch_shapes=[pltpu.VMEM((tc + 2 * CONV_HALO, c), F32)],
        compiler_params=_cparams(("parallel", "parallel")),
        name="conv_module",
    )(yglu, yglu, yglu, w, bias, g, beta)


def _split3(x):
    hi = x.astype(BF16)
    r1 = x - hi.astype(F32)
    mid = r1.astype(BF16)
    lo = (r1 - mid.astype(F32)).astype(BF16)
    return hi, mid, lo


def _gla_kernel(*refs, nc, has_state, want_state):
    it = iter(refs)
    q_ref, k_ref, v_ref, g_ref = next(it), next(it), next(it), next(it)
    s0_ref = next(it) if has_state else None
    o_ref = next(it)
    sfin_ref = next(it) if want_state else None
    s_scr, qt_scr, kv_scr, dec_scr, sin_scr = next(it), next(it), next(it), next(it), next(it)

    d = pl.program_id(1)
    i = pl.program_id(2)
    nblk = pl.num_programs(2)
    c = GLA_CHUNK
    width = s_scr.shape[0]
    nh = width // c

    @pl.when(i == 0)
    def _():
        if has_state:
            s_scr[...] = s0_ref[0, 0]
        else:
            s_scr[...] = jnp.zeros_like(s_scr)

    fwd = d == 0
    row = lax.broadcasted_iota(jnp.int32, (c, c), 0)
    col = lax.broadcasted_iota(jnp.int32, (c, c), 1)
    sgn = jnp.where(fwd, 1, -1)
    tri = (row - col) * sgn >= 0
    tri_b = jnp.where(tri, 1.0, 0.0).astype(BF16)
    rowa = lax.broadcasted_iota(jnp.int32, (c, width), 0)
    cola = lax.broadcasted_iota(jnp.int32, (c, width), 1) % c
    tri_cat = (rowa - cola) * sgn >= 0
    rb = lax.broadcasted_iota(jnp.int32, (width, width), 0) // c
    cb = lax.broadcasted_iota(jnp.int32, (width, width), 1) // c
    bd = rb == cb

    rows = [slice(ci * c, (ci + 1) * c) for ci in range(nc)]
    gcums = []
    for rs in rows:
        ghi, gmid, glo = _split3(g_ref[0, rs, :])
        gcums.append(_dot(tri_b, ghi) + _dot(tri_b, gmid) + _dot(tri_b, glo))
    scores = []
    for ci, (rs, gcum) in enumerate(zip(rows, gcums)):
        q = q_ref[0, rs, :]
        k = k_ref[0, rs, :]
        gtot = jnp.where(fwd, gcum[c - 1:c, :], gcum[0:1, :])
        q_t = (q * jnp.exp(gcum)).astype(BF16)
        k_t = k * jnp.exp(-gcum)
        k_hat = (k * jnp.exp(gtot - gcum)).astype(BF16)
        k_bd = jnp.where(bd, jnp.concatenate([k_t] * nh, axis=0), 0.0).astype(BF16)
        scores.append(_dot_nt(q_t, k_bd))
        qt_scr[rs, :] = q_t
        kv_scr[ci] = jnp.where(bd, _dot_tn(v_ref[0, rs, :], k_hat), 0.0)
        dec_scr[ci] = jnp.broadcast_to(jnp.exp(gtot), dec_scr.shape[1:])
    for rs, a in zip(rows, scores):
        v = v_ref[0, rs, :]
        a = jnp.where(tri_cat, a, 0.0).astype(BF16)
        v_bd = jnp.where(bd, jnp.concatenate([v] * nh, axis=0), jnp.zeros((), v.dtype))
        o_ref[0, 0, rs, :] = _dot(a, v_bd)

    order = [jnp.where(fwd, j, nc - 1 - j) for j in range(nc)]
    for j, cidx in enumerate(order):
        s_in = s_scr[...]
        sin_scr[j] = s_in.astype(BF16)
        s_scr[...] = s_in * dec_scr[cidx, 0:1, :] + kv_scr[cidx]
    for j, cidx in enumerate(order):
        rs = pl.ds(pl.multiple_of(cidx * c, c), c)
        o_ref[0, 0, rs, :] += _dot_nt(qt_scr[rs, :], sin_scr[j])

    if want_state:
        @pl.when(i == nblk - 1)
        def _():
            sfin_ref[0, 0] = s_scr[...]


def _gla(q, k, v, gdec, s0, *, tb, want_state):
    b, t, w = q.shape
    nblk = t // tb
    nc = tb // GLA_CHUNK
    blk = lambda d, i: i + d * (nblk - 1 - 2 * i)
    tok = pl.BlockSpec((1, tb, w), lambda bi, d, i: (bi, blk(d, i), 0))
    in_specs = [tok, tok, tok, pl.BlockSpec((1, tb, w), lambda bi, d, i: (bi, blk(d, i), d))]
    args = [q, k, v, gdec]
    has_state = s0 is not None
    if has_state:
        in_specs.append(pl.BlockSpec((1, 1, w, w), lambda bi, d, i: (bi, d, 0, 0)))
        args.append(s0)
    out_shape = [jax.ShapeDtypeStruct((2, b, t, w), F32)]
    out_specs = [pl.BlockSpec((1, 1, tb, w), lambda bi, d, i: (d, bi, blk(d, i), 0))]
    if want_state:
        out_shape.append(jax.ShapeDtypeStruct((b, 2, w, w), F32))
        out_specs.append(pl.BlockSpec((1, 1, w, w), lambda bi, d, i: (bi, d, 0, 0)))
    kern = functools.partial(_gla_kernel, nc=nc, has_state=has_state, want_state=want_state)
    return pl.pallas_call(
        kern,
        grid=(b, 2, nblk),
        in_specs=in_specs,
        out_specs=out_specs,
        out_shape=out_shape,
        scratch_shapes=[pltpu.VMEM((w, w), F32), pltpu.VMEM((tb, w), BF16),
                        pltpu.VMEM((nc, w, w), F32), pltpu.VMEM((nc, SUBLANES, w), F32),
                        pltpu.VMEM((nc, w, w), BF16)],
        compiler_params=_cparams(("parallel", "parallel", "arbitrary")),
        name="gla",
    )(*args)


def _attn_kernel(*refs, has_cache, lam_init, n_slots):
    it = iter(refs)
    q_ref, k_ref, vt_ref = next(it), next(it), next(it)
    kc_ref, vtc_ref = (next(it), next(it)) if has_cache else (None, None)
    lam_ref, g_ref, o_ref = next(it), next(it), next(it)
    s_scr = [next(it) for _ in range(n_slots)]
    e_scr = [next(it) for _ in range(n_slots)]
    acc_scr = next(it)
    p = lam_ref[...]
    lam = (jnp.exp(jnp.sum(p[0:1] * p[1:2], axis=-1, keepdims=True))
           - jnp.exp(jnp.sum(p[2:3] * p[3:4], axis=-1, keepdims=True)) + lam_init)

    q = q_ref[0]
    tq = q.shape[0]
    lane = lax.broadcasted_iota(jnp.int32, q.shape, 1)
    zero = jnp.zeros((), q.dtype)
    qs = jnp.concatenate([jnp.where(lane < DH_DIFF, q, zero), jnp.where(lane >= DH_DIFF, q, zero)], axis=0)

    n_new, tk = vt_ref.shape[2], vt_ref.shape[4]
    n = n_new + (vtc_ref.shape[2] if has_cache else 0)
    nq = 2 * tq
    acc_scr[...] = jnp.zeros_like(acc_scr)

    def key_tile(j):
        ref, jj = (k_ref, j) if j < n_new else (kc_ref, j - n_new)
        return ref[0, 0, jj * tk:(jj + 1) * tk, :].astype(BF16)

    def value_tile(j):
        return vt_ref[0, 0, j] if j < n_new else vtc_ref[0, 0, j - n_new]

    def scores(j, slot):
        s_scr[slot][...] = _dot_nt(key_tile(j), qs)
        strips = [s_scr[slot][r:r + ATTN_STRIP, :] for r in range(0, tk, ATTN_STRIP)]
        return jnp.max(functools.reduce(jnp.maximum, strips), axis=0, keepdims=True)

    def softmax(slot, mt, m):
        m_new = jnp.maximum(m, mt)
        alpha = jnp.exp2(m - m_new)
        for r in range(0, tk, ATTN_STRIP):
            e_scr[slot][r:r + ATTN_STRIP, :] = jnp.exp2(s_scr[slot][r:r + ATTN_STRIP, :] - m_new).astype(BF16)
        return m_new, alpha

    def weighted(j, slot, alpha):
        acc_scr[...] = acc_scr[...] * alpha + _dot(value_tile(j), e_scr[slot][...])

    slot = lambda j: j % n_slots

    lag = min(ATTN_LAG, n_slots - 1)
    m = jnp.full((1, nq), -1e30, F32)
    mt = scores(0, slot(0))
    alphas = {}
    for j in range(n + lag):
        if j < n:
            mt_next = scores(j + 1, slot(j + 1)) if j + 1 < n else None
            m, alphas[j] = softmax(slot(j), mt, m)
            mt = mt_next
        if 0 <= j - lag < n:
            weighted(j - lag, slot(j - lag), alphas.pop(j - lag))
    dv = o_ref.shape[2]
    on = acc_scr[0:dv, :] / acc_scr[dv:dv + 1, :]
    o = on[:, :tq] - lam * on[:, tq:]
    ms = jnp.mean(o * o, axis=0, keepdims=True)
    y = o * lax.rsqrt(ms + LN_EPS) * g_ref[...] * (1.0 - lam_init)
    o_ref[0] = y.T.astype(o_ref.dtype)


def _attn(dq, dk, dvt, ck, cvt, lam_p, g_col, *, lam_init, tq):
    b, t, _ = dq.shape
    dv, tk = dvt.shape[3:]
    qspec = pl.BlockSpec((1, tq, LANES), lambda bi, h, qi: (bi, qi, h))
    kspec = lambda a: pl.BlockSpec((1, 1) + a.shape[2:], lambda bi, h, qi: (bi, h, 0, 0))
    vspec = lambda a: pl.BlockSpec((1, 1) + a.shape[2:], lambda bi, h, qi: (bi, h, 0, 0, 0))
    has_cache = ck is not None
    n_tiles = dvt.shape[2] + (cvt.shape[2] if has_cache else 0)
    n_slots = min(n_tiles, ATTN_SLOTS)
    kv_args = [dk, dvt] + ([ck, cvt] if has_cache else [])
    kv_specs = [kspec(dk), vspec(dvt)] + ([kspec(ck), vspec(cvt)] if has_cache else [])
    kern = functools.partial(_attn_kernel, has_cache=has_cache, lam_init=lam_init, n_slots=n_slots)
    return pl.pallas_call(
        kern,
        grid=(b, H_DIFF, t // tq),
        in_specs=[qspec] + kv_specs + [_const_spec(lam_p.shape), _const_spec(g_col.shape)],
        out_specs=qspec,
        out_shape=jax.ShapeDtypeStruct(dq.shape, BF16),
        scratch_shapes=([pltpu.VMEM((tk, 2 * tq), F32)] * n_slots + [pltpu.VMEM((tk, 2 * tq), BF16)] * n_slots
                        + [pltpu.VMEM((dv, 2 * tq), F32)]),
        compiler_params=_cparams(("parallel", "parallel", "parallel")),
        name="diff_attn",
    )(dq, *kv_args, lam_p, g_col)


def _mixout_kernel(x_ref, mod_ref, yc_ref, of_ref, ob_ref, gate_ref, yd_ref, gn_ref, w_ref,
                   lng_ref, lnb_ref, o_ref, *, alpha):
    x = x_ref[0]
    g1 = mod_ref[0, 5:6, :]
    o = of_ref[0, 0] + ob_ref[0, 0]
    width = o.shape[1]
    rb = lax.broadcasted_iota(jnp.int32, (width, width), 0) // DK_GLA
    cb = lax.broadcasted_iota(jnp.int32, (width, width), 1) // DK_GLA
    ones_bd = jnp.where(rb == cb, 1.0, 0.0).astype(BF16)
    sq = o * o
    hi = sq.astype(BF16)
    lo = (sq - hi.astype(F32)).astype(BF16)
    ms = (_dot(hi, ones_bd) + _dot(lo, ones_bd)) * (1.0 / DK_GLA)
    yg = (o * lax.rsqrt(ms + LN_EPS) * gn_ref[...] * _silu(gate_ref[0])).astype(BF16)
    c_conv = yc_ref.shape[2]
    y = (_dot(yc_ref[0], w_ref[0:c_conv, :])
         + _dot(yg, w_ref[c_conv:c_conv + width, :])
         + _dot(yd_ref[0], w_ref[c_conv + width:, :]))
    o_ref[0] = _layer_norm(alpha * x + g1 * y, lng_ref[...], lnb_ref[...])


def _mixout(x, mod, yconv, o_gla, gate, ydiff, gn, w, lng, lnb, *, alpha, tm):
    b, t, d = x.shape
    mod_rows = mod.shape[0]
    mod_idx = (lambda bi, ti: (bi, 0, 0)) if mod_rows > 1 else (lambda bi, ti: (0, 0, 0))
    tok = lambda n: pl.BlockSpec((1, tm, n), lambda bi, ti: (bi, ti, 0))
    wg = o_gla.shape[3]
    kern = functools.partial(_mixout_kernel, alpha=alpha)
    return pl.pallas_call(
        kern,
        grid=(b, t // tm),
        in_specs=[
            tok(d), pl.BlockSpec((1, N_MOD, d), mod_idx), tok(yconv.shape[2]),
            pl.BlockSpec((1, 1, tm, wg), lambda bi, ti: (0, bi, ti, 0)),
            pl.BlockSpec((1, 1, tm, wg), lambda bi, ti: (1, bi, ti, 0)),
            tok(wg), tok(ydiff.shape[2]),
            _const_spec(gn.shape), _const_spec(w.shape), _const_spec((1, d)), _const_spec((1, d)),
        ],
        out_specs=tok(d),
        out_shape=jax.ShapeDtypeStruct((b, t, d), F32),
        compiler_params=_cparams(("parallel", "parallel")),
        name="mixer_out",
    )(x, mod, yconv, o_gla, o_gla, gate, ydiff, gn, w, lng, lnb)


def _rope_tables(t):
    rows = t // GRID_W
    row = jnp.repeat(jnp.arange(rows, dtype=F32), GRID_W)
    col = jnp.tile(jnp.arange(GRID_W, dtype=F32), rows)
    seg = DH_DIFF // 2
    inv = ROPE_BASE ** (-jnp.arange(0, seg, 2, dtype=F32) / seg)
    a_r = row[:, None] * inv
    a_c = col[:, None] * inv
    ang = jnp.concatenate([a_r, a_r, a_c, a_c], axis=-1)
    ang = jnp.concatenate([ang, ang], axis=-1)
    sign = jnp.where((jnp.arange(LANES) % 32) < 16, -1.0, 1.0).astype(F32)
    return jnp.cos(ang), jnp.sin(ang) * sign


def _ffn_weights(w_in, w_out, ck):
    d, two_ff = w_in.shape
    ff = two_ff // 2
    n = ff // ck
    wout = w_out.reshape(n, ck, d).astype(BF16)
    return w_in.astype(BF16), wout


def _pick_tile(t, pref):
    return pref if t % pref == 0 else t


def kernel(x_prompt, x_sample, cache_diff_k, cache_diff_v, state_gla, c, c_ctx, w_ada, b_ada, w_ffn1_in,
           w_ffn1_out, w_ffn2_in, w_ffn2_out, w_in, conv_w, conv_b, conv_ln_g, conv_ln_b, gla_w_a2, gla_b_a,
           gla_norm_g, diff_lam, diff_norm_g, w_out, ln_g, ln_b):
    depth, d, _ = w_ada.shape
    alpha = (2 * depth) ** 0.25
    n_dec = c.shape[0]
    c_conv = conv_w.shape[2]
    w_gla = H_GLA * DK_GLA
    w_diff = H_DIFF * 2 * DH_DIFF
    in_conv, in_gla = 2 * c_conv, 4 * w_gla + 2 * GLA_RANK
    ff_chunk = 256

    rows = -(-(n_dec + 1) // SUBLANES) * SUBLANES
    cvec = jnp.zeros((rows, d), F32).at[:n_dec].set(c).at[n_dec].set(c_ctx)
    mod = _ada(cvec, w_ada, b_ada).reshape(depth, rows, N_MOD, d)

    cos, sin = _rope_tables(x_sample.shape[1])

    layers = []
    for l in range(depth):
        wi = w_in[l]
        w_lr = jnp.pad(wi[:, in_conv + 4 * w_gla:in_conv + in_gla], ((0, 0), (0, LANES - 2 * GLA_RANK)))
        w_all = jnp.concatenate([wi[:, :in_conv + 4 * w_gla], wi[:, in_conv + in_gla:], w_lr], axis=1).astype(BF16)
        w2 = jnp.zeros((LANES, 2 * w_gla), F32)
        w2 = w2.at[:GLA_RANK, :w_gla].set(gla_w_a2[l, 0]).at[GLA_RANK:2 * GLA_RANK, w_gla:].set(gla_w_a2[l, 1])
        layers.append(dict(
            ffn1=_ffn_weights(w_ffn1_in[l], w_ffn1_out[l], ff_chunk),
            ffn2=_ffn_weights(w_ffn2_in[l], w_ffn2_out[l], ff_chunk),
            w_all=w_all, w2=w2.astype(BF16), b2=gla_b_a[l].reshape(1, 2 * w_gla),
            conv_w=conv_w[l], conv_b=conv_b[l][None], conv_g=conv_ln_g[l][None], conv_beta=conv_ln_b[l][None],
            gn=jnp.tile(gla_norm_g[l], H_GLA)[None], lam=diff_lam[l], dg=diff_norm_g[l][:, None],
            w_out=w_out[l].astype(BF16),
            lng=[ln_g[l, i][None] for i in range(3)], lnb=[ln_b[l, i][None] for i in range(3)],
            lam_init=0.8 - 0.6 * math.exp(-0.3 * l),
        ))

    def run_layer(x, mod_l, P, ctx):
        t = x.shape[1]
        tm = _pick_tile(t, 512)
        bsz, _, d_model = x.shape
        shared_mod = mod_l.shape[0] == 1
        tf = _pick_tile(bsz * t if shared_mod else t, 1024)

        def ffn(x, which, row0, ln_idx):
            y = _ffn(x.reshape(bsz * t, d_model), mod_l, *P[which], P['lng'][ln_idx], P['lnb'][ln_idx],
                     row0=row0, alpha=alpha, tm=tf)
            return y.reshape(bsz, t, d_model)

        x = ffn(x, 'ffn1', 0, 0)
        rope = ctx is not None
        yglu, gq, gk, gv, gate, gdec, dq, dk, dvt, *dv = _mixin(
            x, mod_l, P['w_all'], P['w2'], P['b2'], cos if rope else None, sin if rope else None, tm=tm)
        yconv = _conv(yglu, P['conv_w'], P['conv_b'], P['conv_g'], P['conv_beta'], tc=_pick_tile(t, 256))
        gla_out = _gla(gq, gk, gv, gdec, ctx['s0'] if rope else None, tb=_pick_tile(t, 512), want_state=not rope)
        ydiff = _attn(dq, dk, dvt, ctx['k'] if rope else None, ctx['vt'] if rope else None, P['lam'], P['dg'],
                      lam_init=P['lam_init'], tq=_pick_tile(t, 512))
        x = _mixout(x, mod_l, yconv, gla_out[0], gate, ydiff, P['gn'], P['w_out'], P['lng'][1], P['lnb'][1],
                    alpha=alpha, tm=tm)
        x = ffn(x, 'ffn2', 6, 2)
        return x, (dk, dv[0] if dv else None, gla_out[1] if not rope else None)

    xp = x_prompt
    new_k, new_v, new_s = [], [], []
    for l in range(depth):
        xp, (k_l, v_l, s_l) = run_layer(xp, mod[l, n_dec:n_dec + 1], layers[l], None)
        new_k.append(k_l)
        new_v.append(v_l)
        bsz = s_l.shape[0]
        s6 = s_l.reshape(bsz, 2, H_GLA, DK_GLA, H_GLA, DK_GLA)
        s_heads = jnp.stack([s6[:, :, h, :, h, :] for h in range(H_GLA)], axis=2)
        new_s.append(jnp.swapaxes(s_heads, -1, -2))

    xs = x_sample
    for l in range(depth):
        st = jnp.swapaxes(state_gla[:, l], -1, -2)
        s0 = jnp.zeros(st.shape[:2] + (H_GLA, DK_GLA, H_GLA, DK_GLA), F32)
        for h in range(H_GLA):
            s0 = s0.at[:, :, h, :, h, :].set(st[:, :, h])
        s0 = s0.reshape(st.shape[0], 2, w_gla, w_gla)
        past = cache_diff_v.shape[3]
        tile = _pick_tile(x_sample.shape[1], 512)
        assert past % tile == 0, "cached context length must be a whole number of key tiles"
        cvt = cache_diff_v[:, l].astype(BF16).reshape(n_dec, H_DIFF, past // tile, tile, LANES)
        cvt = jnp.concatenate([jnp.swapaxes(cvt, -1, -2), jnp.ones(cvt.shape[:3] + (ONES_ROWS, tile), BF16)], axis=3)
        ctx = dict(k=cache_diff_k[:, l].astype(BF16), vt=cvt, s0=s0)
        xs, _ = run_layer(xs, mod[l, :n_dec], layers[l], ctx)

    return (xp, xs, jnp.stack(new_k, axis=1), jnp.stack(new_v, axis=1), jnp.stack(new_s, axis=1))
```

```python
import functools
import math

import jax
import jax.numpy as jnp
from jax import lax
from jax.experimental import pallas as pl
from jax.experimental.pallas import tpu as pltpu

F32 = jnp.float32
BF16 = jnp.bfloat16

GRID_W = 64
CONV_WIDTH = 31
H_GLA = 4
DK_GLA = 64
GLA_RANK = 16
GLA_TAU = 16.0
GLA_CHUNK = 64
H_DIFF = 4
DH_DIFF = 64
ROPE_BASE = 10000.0
LN_EPS = 1e-5
N_MOD = 9

LANES = 128
SUBLANES = 8
V7X_VMEM_BYTES = 64 * 1024 * 1024
VMEM_LIMIT = V7X_VMEM_BYTES - 8 * 1024 * 1024

CONV_HALO = 16
CONV_ROWS = 32
ATTN_STRIP = 32
ATTN_KEY_TILE = 512
ATTN_LAG = 1
ATTN_SLOTS = 3
ONES_ROWS = 16
ROW_STRIP = 32
LN_STRIP = 64


def _cparams(sem):
    return pltpu.CompilerParams(dimension_semantics=sem, vmem_limit_bytes=VMEM_LIMIT)


def _const_spec(shape):
    nd = len(shape)
    return pl.BlockSpec(shape, lambda *_: (0,) * nd, pipeline_mode=pl.Buffered(1))


def _dot(a, b):
    return jnp.dot(a, b, preferred_element_type=F32)


def _dot_nt(a, b):
    return lax.dot_general(a, b, (((1,), (1,)), ((), ())), preferred_element_type=F32)


def _dot_tn(a, b):
    return lax.dot_general(a, b, (((0,), (0,)), ((), ())), preferred_element_type=F32)


def _silu(x):
    return x * jax.nn.sigmoid(x)


def _layer_norm(y, g, b):
    mu = jnp.mean(y, axis=-1, keepdims=True)
    yc = y - mu
    var = jnp.mean(yc * yc, axis=-1, keepdims=True)
    return yc * lax.rsqrt(var + LN_EPS) * g + b


def _modulate(x, mod_ref, row0):
    sh = mod_ref[0, row0:row0 + 1, :]
    sc = mod_ref[0, row0 + 1:row0 + 2, :]
    g = mod_ref[0, row0 + 2:row0 + 3, :]
    return x * (1.0 + sc) + sh, g


def _ada_kernel(c_ref, w_ref, b_ref, o_ref):
    s = _silu(c_ref[...]).astype(BF16)
    o_ref[0] = _dot(s, w_ref[0].astype(BF16)) + b_ref[0]


def _ada(cvec, w_ada, b_ada):
    depth, d, n = w_ada.shape
    rows = cvec.shape[0]
    tn = d
    return pl.pallas_call(
        _ada_kernel,
        grid=(depth, n // tn),
        in_specs=[
            pl.BlockSpec((rows, d), lambda l, j: (0, 0)),
            pl.BlockSpec((1, d, tn), lambda l, j: (l, 0, j)),
            pl.BlockSpec((1, 1, tn), lambda l, j: (l, 0, j)),
        ],
        out_specs=pl.BlockSpec((1, rows, tn), lambda l, j: (l, 0, j)),
        out_shape=jax.ShapeDtypeStruct((depth, rows, n), F32),
        compiler_params=_cparams(("arbitrary", "arbitrary")),
        name="ada_mod",
    )(cvec, w_ada, b_ada.reshape(depth, 1, n))


def _strips(n_rows, fn):
    def body(i, carry):
        fn(pl.ds(pl.multiple_of(i * ROW_STRIP, ROW_STRIP), ROW_STRIP))
        return carry

    lax.fori_loop(0, n_rows // ROW_STRIP, body, 0, unroll=2)


def _layer_norm_rows(n_rows, make_y, o_ref, g, b):
    n = n_rows // LN_STRIP
    sl = lambda i: pl.ds(i * LN_STRIP if isinstance(i, int) else pl.multiple_of(i * LN_STRIP, LN_STRIP), LN_STRIP)

    def means(i):
        y = make_y(sl(i))
        o_ref[sl(i), :] = y
        return jnp.mean(y, axis=-1, keepdims=True)

    def rstd(i, mu):
        yc = o_ref[sl(i), :] - mu
        return lax.rsqrt(jnp.mean(yc * yc, axis=-1, keepdims=True) + LN_EPS)

    def normalise(i, mu, r):
        o_ref[sl(i), :] = (o_ref[sl(i), :] - mu) * r * g + b

    if n == 1:
        mu = means(0)
        normalise(0, mu, rstd(0, mu))
        return
    mu_a = means(0)
    mu_b = means(1)
    r_a = rstd(0, mu_a)

    def body(i, carry):
        mu_a, r_a, mu_b = carry
        r_b = rstd(i - 1, mu_b)
        mu_c = means(i)
        normalise(i - 2, mu_a, r_a)
        return mu_b, r_b, mu_c

    mu_a, r_a, mu_b = lax.fori_loop(2, n, body, (mu_a, r_a, mu_b))
    r_b = rstd(n - 1, mu_b)
    normalise(n - 2, mu_a, r_a)
    normalise(n - 1, mu_b, r_b)


def _ffn_kernel(x_ref, mod_ref, win_ref, wout_ref, lng_ref, lnb_ref, o_ref, h_scr, acc_ref, y_scr, *, row0, alpha):
    i = pl.program_id(0)
    n_tiles = pl.num_programs(0) - 1
    tm = x_ref.shape[0]
    n_chunks, ck, _ = wout_ref.shape
    lng = lng_ref[...]
    lnb = lnb_ref[...]
    n_strips = tm // LN_STRIP
    unroll = 2 if n_chunks % 2 else 1
    n_iters = (n_chunks - 1) // unroll
    per_iter = -(-n_strips // (n_iters + 1))

    def norm_strip(idx):
        idx = jnp.minimum(idx, n_strips - 1)
        rs = pl.ds(pl.multiple_of(idx * LN_STRIP, LN_STRIP), LN_STRIP)
        o_ref[rs, :] = _layer_norm(y_scr[rs, :], lng, lnb)

    @pl.when(i == 0)
    def _():
        y_scr[...] = jnp.zeros_like(y_scr)

    @pl.when(i < n_tiles)
    def _():
        sh = mod_ref[0, row0:row0 + 1, :]
        sc1 = 1.0 + mod_ref[0, row0 + 1:row0 + 2, :]
        half_g = 0.5 * mod_ref[0, row0 + 2:row0 + 3, :]

        def modulate(rs):
            h_scr[rs, :] = (x_ref[rs, :] * sc1 + sh).astype(BF16)

        _strips(tm, modulate)

        def chunk(j):
            cols = lambda start: pl.ds(pl.multiple_of(start, ck), ck)
            h = h_scr[...]
            a = _dot(h, win_ref[:, cols(j * ck)])
            b = _dot(h, win_ref[:, cols(n_chunks * ck + j * ck)])
            return _dot((_silu(a) * b).astype(BF16), wout_ref[j])

        acc_ref[...] = chunk(0)
        for s in range(per_iter):
            norm_strip(s)

        def body(k, carry):
            for u in range(unroll):
                acc_ref[...] += chunk(1 + k * unroll + u)
            for s in range(per_iter):
                norm_strip((k + 1) * per_iter + s)
            return carry

        lax.fori_loop(0, n_iters, body, 0)

        def residual(rs):
            y_scr[rs, :] = alpha * x_ref[rs, :] + half_g * acc_ref[rs, :]

        _strips(tm, residual)

    @pl.when(i == n_tiles)
    def _():
        _layer_norm_rows(tm, lambda rs: y_scr[rs, :], o_ref, lng, lnb)


def _ffn(x, mod, win, wout, lng, lnb, *, row0, alpha, tm):
    n_tok, d = x.shape
    n_tiles = n_tok // tm
    per_req = n_tok // mod.shape[0]
    cur = lambda ti: jnp.minimum(ti, n_tiles - 1)
    kern = functools.partial(_ffn_kernel, row0=row0, alpha=alpha)
    return pl.pallas_call(
        kern,
        grid=(n_tiles + 1,),
        in_specs=[
            pl.BlockSpec((tm, d), lambda ti: (cur(ti), 0)),
            pl.BlockSpec((1, N_MOD, d), lambda ti: ((cur(ti) * tm) // per_req, 0, 0)),
            _const_spec(win.shape),
            _const_spec(wout.shape),
            _const_spec((1, d)),
            _const_spec((1, d)),
        ],
        out_specs=pl.BlockSpec((tm, d), lambda ti: (jnp.maximum(ti - 1, 0), 0)),
        out_shape=jax.ShapeDtypeStruct((n_tok, d), F32),
        scratch_shapes=[pltpu.VMEM((tm, d), BF16), pltpu.VMEM((tm, d), F32), pltpu.VMEM((tm, d), F32)],
        compiler_params=_cparams(("arbitrary",)),
        name="ffn",
    )(x, mod, win, wout, lng, lnb)


def _log_sigmoid(x):
    return jnp.minimum(x, 0.0) - jnp.log(1.0 + jnp.exp(-jnp.abs(x)))


def _rope(x, cos, sin_signed, first_half):
    rot = jnp.where(first_half, pltpu.roll(x, LANES - 16, 1), pltpu.roll(x, 16, 1))
    return x * cos + rot * sin_signed


def _mixin_kernel(*refs, rope, c_conv, w_gla, w_diff):
    if rope:
        (x_ref, mod_ref, w_ref, w2_ref, b2_ref, cos_ref, sin_ref,
         yglu_ref, gq_ref, gk_ref, gv_ref, gate_ref, gdec_ref, dq_ref, dk_ref, dvt_ref) = refs
    else:
        (x_ref, mod_ref, w_ref, w2_ref, b2_ref,
         yglu_ref, gq_ref, gk_ref, gv_ref, gate_ref, gdec_ref, dq_ref, dk_ref, dvt_ref, dv_ref) = refs
    x = x_ref[0]
    h, _ = _modulate(x, mod_ref, 3)
    h = h.astype(BF16)
    o_gla = 2 * c_conv
    o_diff = o_gla + 4 * w_gla
    o_lr = o_diff + 3 * w_diff

    tm = x.shape[0]
    halves = [slice(0, tm // 2), slice(tm // 2, tm)] if tm % (2 * LANES) == 0 else [slice(0, tm)]
    zs = []
    for rs in halves:
        hh = h[rs]
        zs.append((_dot(hh, w_ref[:, 0:o_gla]), _dot(hh, w_ref[:, o_gla:o_diff]),
                   _dot(hh, w_ref[:, o_lr:o_lr + LANES]).astype(BF16), _dot(hh, w_ref[:, o_diff:o_lr])))
    pres = [_dot(z[2], w2_ref[...]) + b2_ref[...] for z in zs]

    for rs, (zc, zg, _, zd), pre in zip(halves, zs, pres):
        yglu_ref[0, rs, :] = zc[:, :c_conv] * jax.nn.sigmoid(zc[:, c_conv:])
        gq_ref[0, rs, :] = zg[:, 0:w_gla] * (DK_GLA ** -0.5)
        gk_ref[0, rs, :] = zg[:, w_gla:2 * w_gla]
        gv_ref[0, rs, :] = zg[:, 2 * w_gla:3 * w_gla].astype(BF16)
        gate_ref[0, rs, :] = zg[:, 3 * w_gla:4 * w_gla]
        gdec_ref[0, rs, :] = _log_sigmoid(pre) * (1.0 / GLA_TAU)
        if rope:
            cos = cos_ref[rs, :]
            sin = sin_ref[rs, :]
            lane = lax.broadcasted_iota(jnp.int32, cos.shape, 1)
            first_half = (lane % 32) < 16
        for hd in range(H_DIFF):
            sl = slice(hd * LANES, (hd + 1) * LANES)
            q = zd[:, sl]
            k = zd[:, w_diff + hd * LANES:w_diff + (hd + 1) * LANES]
            v = zd[:, 2 * w_diff + hd * LANES:2 * w_diff + (hd + 1) * LANES]
            if rope:
                q = _rope(q, cos, sin, first_half)
                k = _rope(k, cos, sin, first_half)
            dq_ref[0, rs, sl] = (q * (DH_DIFF ** -0.5 * math.log2(math.e))).astype(BF16)
            dk_ref[0, hd, rs, :] = k.astype(dk_ref.dtype)
            dvt_ref[0, hd, 0, 0:LANES, rs] = v.T.astype(BF16)
            if not rope:
                dv_ref[0, hd, rs, :] = v
    for hd in range(H_DIFF):
        dvt_ref[0, hd, 0, LANES:, :] = jnp.ones((ONES_ROWS, tm), BF16)


def _mixin(x, mod, w, w2, b2, cos, sin, *, tm):
    b, t, d = x.shape
    rope = cos is not None
    c_conv = 256
    w_gla = H_GLA * DK_GLA
    w_diff = H_DIFF * 2 * DH_DIFF
    kv_dtype = BF16 if rope else F32
    mod_rows = mod.shape[0]
    mod_idx = (lambda bi, ti: (bi, 0, 0)) if mod_rows > 1 else (lambda bi, ti: (0, 0, 0))
    tok = lambda n: pl.BlockSpec((1, tm, n), lambda bi, ti: (bi, ti, 0))
    heads = pl.BlockSpec((1, H_DIFF, tm, LANES), lambda bi, ti: (bi, 0, ti, 0))
    in_specs = [tok(d), pl.BlockSpec((1, N_MOD, d), mod_idx),
                _const_spec(w.shape), _const_spec(w2.shape), _const_spec(b2.shape)]
    args = [x, mod, w, w2, b2]
    if rope:
        in_specs += [pl.BlockSpec((tm, LANES), lambda bi, ti: (ti, 0))] * 2
        args += [cos, sin]
    sds = jax.ShapeDtypeStruct
    out_shape = [
        sds((b, t, c_conv), F32),
        sds((b, t, w_gla), F32),
        sds((b, t, w_gla), F32),
        sds((b, t, w_gla), BF16),
        sds((b, t, w_gla), F32),
        sds((b, t, 2 * w_gla), F32),
        sds((b, t, w_diff), BF16),
        sds((b, H_DIFF, t, LANES), kv_dtype),
        sds((b, H_DIFF, t // tm, LANES + ONES_ROWS, tm), BF16),
    ]
    out_specs = [tok(c_conv), tok(w_gla), tok(w_gla), tok(w_gla), tok(w_gla), tok(2 * w_gla),
                 tok(w_diff), heads,
                 pl.BlockSpec((1, H_DIFF, 1, LANES + ONES_ROWS, tm), lambda bi, ti: (bi, 0, ti, 0, 0))]
    if not rope:
        out_shape.append(sds((b, H_DIFF, t, LANES), F32))
        out_specs.append(heads)
    kern = functools.partial(_mixin_kernel, rope=rope, c_conv=c_conv, w_gla=w_gla, w_diff=w_diff)
    return pl.pallas_call(
        kern,
        grid=(b, t // tm),
        in_specs=in_specs,
        out_specs=out_specs,
        out_shape=out_shape,
        compiler_params=_cparams(("parallel", "parallel")),
        name="mixer_in",
    )(*args)


def _conv_kernel(prev_ref, cur_ref, next_ref, w_ref, b_ref, g_ref, beta_ref, o_ref, win_ref, *, tc, nt):
    i = pl.program_id(1)
    pad = CONV_WIDTH // 2
    win_ref[0:CONV_HALO] = jnp.where(i > 0, prev_ref[0], 0.0)
    win_ref[CONV_HALO:CONV_HALO + tc] = cur_ref[0]
    win_ref[CONV_HALO + tc:2 * CONV_HALO + tc] = jnp.where(i < nt - 1, next_ref[0], 0.0)
    w = w_ref[...]
    first = CONV_HALO - pad
    n_groups = -(-(first + CONV_WIDTH) // SUBLANES)
    for r in range(0, tc, CONV_ROWS):
        wide = win_ref[r:r + CONV_ROWS + n_groups * SUBLANES, :]
        acc = None
        for s in range(SUBLANES):
            z = None
            for a in range(n_groups):
                k = a * SUBLANES + s - first
                if 0 <= k < CONV_WIDTH:
                    term = wide[a * SUBLANES:a * SUBLANES + CONV_ROWS + SUBLANES, :] * w[k:k + 1, :]
                    z = term if z is None else z + term
            if z is not None:
                acc = z[s:s + CONV_ROWS, :] if acc is None else acc + z[s:s + CONV_ROWS, :]
        y = _layer_norm(acc + b_ref[...], g_ref[...], beta_ref[...])
        o_ref[0, r:r + CONV_ROWS, :] = _silu(y).astype(o_ref.dtype)


def _conv(yglu, w, bias, g, beta, *, tc):
    b, t, c = yglu.shape
    nt = t // tc
    hb = tc // CONV_HALO
    n_halo = t // CONV_HALO
    kern = functools.partial(_conv_kernel, tc=tc, nt=nt)
    return pl.pallas_call(
        kern,
        grid=(b, nt),
        in_specs=[
            pl.BlockSpec((1, CONV_HALO, c), lambda bi, i: (bi, jnp.maximum(i * hb - 1, 0), 0)),
            pl.BlockSpec((1, tc, c), lambda bi, i: (bi, i, 0)),
            pl.BlockSpec((1, CONV_HALO, c), lambda bi, i: (bi, jnp.minimum((i + 1) * hb, n_halo - 1), 0)),
            _const_spec(w.shape), _const_spec((1, c)), _const_spec((1, c)), _const_spec((1, c)),
        ],
        out_specs=pl.BlockSpec((1, tc, c), lambda bi, i: (bi, i, 0)),
        out_shape=jax.ShapeDtypeStruct((b, t, c), BF16),
        scratch_shapes=[pltpu.VMEM((tc + 2 * CONV_HALO, c), F32)],
        compiler_params=_cparams(("parallel", "parallel")),
        name="conv_module",
    )(yglu, yglu, yglu, w, bias, g, beta)


def _split3(x):
    hi = x.astype(BF16)
    r1 = x - hi.astype(F32)
    mid = r1.astype(BF16)
    lo = (r1 - mid.astype(F32)).astype(BF16)
    return hi, mid, lo


def _gla_kernel(*refs, nc, has_state, want_state):
    it = iter(refs)
    ins = [[next(it) for _ in range(4)] for _ in range(2)]
    s0_ref = next(it) if has_state else None
    o_refs = [next(it), next(it)]
    sfin_ref = next(it) if want_state else None
    s_scr, qt_scr, kv_scr, dec_scr, sin_scr = (next(it) for _ in range(5))

    i = pl.program_id(1)
    nblk = pl.num_programs(1)
    c = GLA_CHUNK
    width = s_scr.shape[1]
    nh = width // c

    @pl.when(i == 0)
    def _():
        if has_state:
            s_scr[...] = s0_ref[0]
        else:
            s_scr[...] = jnp.zeros_like(s_scr)

    row = lax.broadcasted_iota(jnp.int32, (c, c), 0)
    col = lax.broadcasted_iota(jnp.int32, (c, c), 1)
    rowa = lax.broadcasted_iota(jnp.int32, (c, width), 0)
    cola = lax.broadcasted_iota(jnp.int32, (c, width), 1) % c
    tri_b = [jnp.where(row >= col, 1.0, 0.0).astype(BF16), jnp.where(row <= col, 1.0, 0.0).astype(BF16)]
    tri_cat = [rowa >= cola, rowa <= cola]
    rb = lax.broadcasted_iota(jnp.int32, (width, width), 0) // c
    cb = lax.broadcasted_iota(jnp.int32, (width, width), 1) // c
    bd = rb == cb
    rows = [slice(ci * c, (ci + 1) * c) for ci in range(nc)]
    work = [(d, ci) for ci in range(nc) for d in range(2)]

    gcums = {}
    for d, ci in work:
        ghi, gmid, glo = _split3(ins[d][3][0, rows[ci], :])
        gcums[d, ci] = _dot(tri_b[d], ghi) + _dot(tri_b[d], gmid) + _dot(tri_b[d], glo)
    scores = {}
    for d, ci in work:
        q_ref, k_ref, v_ref, _ = ins[d]
        rs = rows[ci]
        gcum = gcums[d, ci]
        gtot = gcum[c - 1:c, :] if d == 0 else gcum[0:1, :]
        k = k_ref[0, rs, :]
        q_t = (q_ref[0, rs, :] * jnp.exp(gcum)).astype(BF16)
        k_t = k * jnp.exp(-gcum)
        k_hat = (k * jnp.exp(gtot - gcum)).astype(BF16)
        k_bd = jnp.where(bd, jnp.concatenate([k_t] * nh, axis=0), 0.0).astype(BF16)
        scores[d, ci] = _dot_nt(q_t, k_bd)
        qt_scr[d, rs, :] = q_t
        kv_scr[d, ci] = jnp.where(bd, _dot_tn(v_ref[0, rs, :], k_hat), 0.0)
        dec_scr[d, ci] = jnp.broadcast_to(jnp.exp(gtot), dec_scr.shape[2:])
    for d, ci in work:
        v = ins[d][2][0, rows[ci], :]
        a = jnp.where(tri_cat[d], scores[d, ci], 0.0).astype(BF16)
        v_bd = jnp.where(bd, jnp.concatenate([v] * nh, axis=0), jnp.zeros((), v.dtype))
        o_refs[d][0, rows[ci], :] = _dot(a, v_bd)

    scan = [(d, j, j if d == 0 else nc - 1 - j) for j in range(nc) for d in range(2)]
    for d, j, ci in scan:
        s_in = s_scr[d]
        sin_scr[d, j] = s_in.astype(BF16)
        s_scr[d] = s_in * dec_scr[d, ci, 0:1, :] + kv_scr[d, ci]
    for d, j, ci in scan:
        o_refs[d][0, rows[ci], :] += _dot_nt(qt_scr[d, rows[ci], :], sin_scr[d, j])

    if want_state:
        @pl.when(i == nblk - 1)
        def _():
            sfin_ref[0] = s_scr[...]


def _gla(q, k, v, gdec, s0, *, tb, want_state):
    b, t, w = q.shape
    nblk = t // tb
    nc = tb // GLA_CHUNK
    blk = [lambda i: i, lambda i: nblk - 1 - i]
    in_specs, args = [], []
    for d in range(2):
        tok = pl.BlockSpec((1, tb, w), lambda bi, i, d=d: (bi, blk[d](i), 0))
        in_specs += [tok, tok, tok, pl.BlockSpec((1, tb, w), lambda bi, i, d=d: (bi, blk[d](i), d))]
        args += [q, k, v, gdec]
    has_state = s0 is not None
    if has_state:
        in_specs.append(pl.BlockSpec((1, 2, w, w), lambda bi, i: (bi, 0, 0, 0)))
        args.append(s0)
    out_shape = [jax.ShapeDtypeStruct((b, t, w), F32)] * 2
    out_specs = [pl.BlockSpec((1, tb, w), lambda bi, i, d=d: (bi, blk[d](i), 0)) for d in range(2)]
    if want_state:
        out_shape.append(jax.ShapeDtypeStruct((b, 2, w, w), F32))
        out_specs.append(pl.BlockSpec((1, 2, w, w), lambda bi, i: (bi, 0, 0, 0)))
    kern = functools.partial(_gla_kernel, nc=nc, has_state=has_state, want_state=want_state)
    return pl.pallas_call(
        kern,
        grid=(b, nblk),
        in_specs=in_specs,
        out_specs=out_specs,
        out_shape=out_shape,
        scratch_shapes=[pltpu.VMEM((2, w, w), F32), pltpu.VMEM((2, tb, w), BF16),
                        pltpu.VMEM((2, nc, w, w), F32), pltpu.VMEM((2, nc, SUBLANES, w), F32),
                        pltpu.VMEM((2, nc, w, w), BF16)],
        compiler_params=_cparams(("parallel", "arbitrary")),
        name="gla",
    )(*args)


def _attn_kernel(*refs, has_cache, lam_init, n_slots):
    it = iter(refs)
    q_ref, k_ref, vt_ref = next(it), next(it), next(it)
    kc_ref, vtc_ref = (next(it), next(it)) if has_cache else (None, None)
    lam_ref, g_ref, o_ref = next(it), next(it), next(it)
    s_scr = [next(it) for _ in range(n_slots)]
    e_scr = [next(it) for _ in range(n_slots)]
    acc_scr = next(it)
    p = lam_ref[...]
    lam = (jnp.exp(jnp.sum(p[0:1] * p[1:2], axis=-1, keepdims=True))
           - jnp.exp(jnp.sum(p[2:3] * p[3:4], axis=-1, keepdims=True)) + lam_init)

    q = q_ref[0]
    tq = q.shape[0]
    lane = lax.broadcasted_iota(jnp.int32, q.shape, 1)
    zero = jnp.zeros((), q.dtype)
    qs = jnp.concatenate([jnp.where(lane < DH_DIFF, q, zero), jnp.where(lane >= DH_DIFF, q, zero)], axis=0)

    tk = vt_ref.shape[4]
    nq = 2 * tq
    acc_scr[...] = jnp.zeros_like(acc_scr)
    group = s_scr[0].shape[0] // tk
    tiles = []
    for kr, vr in [(k_ref, vt_ref)] + ([(kc_ref, vtc_ref)] if has_cache else []):
        for first in range(0, vr.shape[2], group):
            tiles.append((kr, vr, first, min(group, vr.shape[2] - first)))
    n = len(tiles)

    def scores(j, slot):
        kr, _, first, cnt = tiles[j]
        rows = cnt * tk
        s_scr[slot][0:rows, :] = _dot_nt(kr[0, 0, first * tk:first * tk + rows, :].astype(BF16), qs)
        strips = [s_scr[slot][r:r + ATTN_STRIP, :] for r in range(0, rows, ATTN_STRIP)]
        return jnp.max(functools.reduce(jnp.maximum, strips), axis=0, keepdims=True)

    def softmax(j, slot, mt, m):
        m_new = jnp.maximum(m, mt)
        alpha = jnp.exp2(m - m_new)
        for r in range(0, tiles[j][3] * tk, ATTN_STRIP):
            e_scr[slot][r:r + ATTN_STRIP, :] = jnp.exp2(s_scr[slot][r:r + ATTN_STRIP, :] - m_new).astype(BF16)
        return m_new, alpha

    def weighted(j, slot, alpha):
        _, vr, first, cnt = tiles[j]
        pv = functools.reduce(lambda a, b: a + b, [_dot(vr[0, 0, first + p], e_scr[slot][p * tk:(p + 1) * tk, :])
                                                   for p in range(cnt)])
        acc_scr[...] = acc_scr[...] * alpha + pv

    slot = lambda j: j % n_slots

    lag = min(ATTN_LAG, n_slots - 1)
    m = jnp.full((1, nq), -1e30, F32)
    mt = scores(0, slot(0))
    alphas = {}
    for j in range(n + lag):
        if j < n:
            mt_next = scores(j + 1, slot(j + 1)) if j + 1 < n else None
            m, alphas[j] = softmax(j, slot(j), mt, m)
            mt = mt_next
        if 0 <= j - lag < n:
            weighted(j - lag, slot(j - lag), alphas.pop(j - lag))
    dv = o_ref.shape[2]
    on = acc_scr[0:dv, :] / acc_scr[dv:dv + 1, :]
    o = on[:, :tq] - lam * on[:, tq:]
    ms = jnp.mean(o * o, axis=0, keepdims=True)
    y = o * lax.rsqrt(ms + LN_EPS) * g_ref[...] * (1.0 - lam_init)
    o_ref[0] = y.T.astype(o_ref.dtype)


def _attn(dq, dk, dvt, ck, cvt, lam_p, g_col, *, lam_init, tq):
    b, t, _ = dq.shape
    dv, tk = dvt.shape[3:]
    qspec = pl.BlockSpec((1, tq, LANES), lambda bi, h, qi: (bi, qi, h))
    kspec = lambda a: pl.BlockSpec((1, 1) + a.shape[2:], lambda bi, h, qi: (bi, h, 0, 0))
    vspec = lambda a: pl.BlockSpec((1, 1) + a.shape[2:], lambda bi, h, qi: (bi, h, 0, 0, 0))
    has_cache = ck is not None
    group = max(1, min(ATTN_KEY_TILE // tk, dvt.shape[2]))
    n_tiles = -(-dvt.shape[2] // group) + (-(-cvt.shape[2] // group) if has_cache else 0)
    n_slots = min(n_tiles, ATTN_SLOTS)
    rows = group * tk
    kv_args = [dk, dvt] + ([ck, cvt] if has_cache else [])
    kv_specs = [kspec(dk), vspec(dvt)] + ([kspec(ck), vspec(cvt)] if has_cache else [])
    kern = functools.partial(_attn_kernel, has_cache=has_cache, lam_init=lam_init, n_slots=n_slots)
    return pl.pallas_call(
        kern,
        grid=(b, H_DIFF, t // tq),
        in_specs=[qspec] + kv_specs + [_const_spec(lam_p.shape), _const_spec(g_col.shape)],
        out_specs=qspec,
        out_shape=jax.ShapeDtypeStruct(dq.shape, BF16),
        scratch_shapes=([pltpu.VMEM((rows, 2 * tq), F32)] * n_slots + [pltpu.VMEM((rows, 2 * tq), BF16)] * n_slots
                        + [pltpu.VMEM((dv, 2 * tq), F32)]),
        compiler_params=_cparams(("parallel", "parallel", "parallel")),
        name="diff_attn",
    )(dq, *kv_args, lam_p, g_col)


def _mixout_kernel(x_ref, mod_ref, yc_ref, of_ref, ob_ref, gate_ref, yd_ref, gn_ref, w_ref,
                   lng_ref, lnb_ref, o_ref, *, alpha):
    x = x_ref[0]
    g1 = mod_ref[0, 5:6, :]
    o = of_ref[0] + ob_ref[0]
    width = o.shape[1]
    rb = lax.broadcasted_iota(jnp.int32, (width, width), 0) // DK_GLA
    cb = lax.broadcasted_iota(jnp.int32, (width, width), 1) // DK_GLA
    ones_bd = jnp.where(rb == cb, 1.0, 0.0).astype(BF16)
    sq = o * o
    hi = sq.astype(BF16)
    lo = (sq - hi.astype(F32)).astype(BF16)
    ms = (_dot(hi, ones_bd) + _dot(lo, ones_bd)) * (1.0 / DK_GLA)
    yg = (o * lax.rsqrt(ms + LN_EPS) * gn_ref[...] * _silu(gate_ref[0])).astype(BF16)
    c_conv = yc_ref.shape[2]
    y = (_dot(yc_ref[0], w_ref[0:c_conv, :])
         + _dot(yg, w_ref[c_conv:c_conv + width, :])
         + _dot(yd_ref[0], w_ref[c_conv + width:, :]))
    o_ref[0] = _layer_norm(alpha * x + g1 * y, lng_ref[...], lnb_ref[...])


def _mixout(x, mod, yconv, o_fwd, o_bwd, gate, ydiff, gn, w, lng, lnb, *, alpha, tm):
    b, t, d = x.shape
    mod_rows = mod.shape[0]
    mod_idx = (lambda bi, ti: (bi, 0, 0)) if mod_rows > 1 else (lambda bi, ti: (0, 0, 0))
    tok = lambda n: pl.BlockSpec((1, tm, n), lambda bi, ti: (bi, ti, 0))
    wg = o_fwd.shape[2]
    kern = functools.partial(_mixout_kernel, alpha=alpha)
    return pl.pallas_call(
        kern,
        grid=(b, t // tm),
        in_specs=[
            tok(d), pl.BlockSpec((1, N_MOD, d), mod_idx), tok(yconv.shape[2]),
            tok(wg), tok(wg), tok(wg), tok(ydiff.shape[2]),
            _const_spec(gn.shape), _const_spec(w.shape), _const_spec((1, d)), _const_spec((1, d)),
        ],
        out_specs=tok(d),
        out_shape=jax.ShapeDtypeStruct((b, t, d), F32),
        compiler_params=_cparams(("parallel", "parallel")),
        name="mixer_out",
    )(x, mod, yconv, o_fwd, o_bwd, gate, ydiff, gn, w, lng, lnb)


def _rope_tables(t):
    rows = t // GRID_W
    row = jnp.repeat(jnp.arange(rows, dtype=F32), GRID_W)
    col = jnp.tile(jnp.arange(GRID_W, dtype=F32), rows)
    seg = DH_DIFF // 2
    inv = ROPE_BASE ** (-jnp.arange(0, seg, 2, dtype=F32) / seg)
    a_r = row[:, None] * inv
    a_c = col[:, None] * inv
    ang = jnp.concatenate([a_r, a_r, a_c, a_c], axis=-1)
    ang = jnp.concatenate([ang, ang], axis=-1)
    sign = jnp.where((jnp.arange(LANES) % 32) < 16, -1.0, 1.0).astype(F32)
    return jnp.cos(ang), jnp.sin(ang) * sign


def _ffn_weights(w_in, w_out, ck):
    d, two_ff = w_in.shape
    ff = two_ff // 2
    n = ff // ck
    wout = w_out.reshape(n, ck, d).astype(BF16)
    return w_in.astype(BF16), wout


def _pick_tile(t, pref):
    return pref if t % pref == 0 else t


def kernel(x_prompt, x_sample, cache_diff_k, cache_diff_v, state_gla, c, c_ctx, w_ada, b_ada, w_ffn1_in,
           w_ffn1_out, w_ffn2_in, w_ffn2_out, w_in, conv_w, conv_b, conv_ln_g, conv_ln_b, gla_w_a2, gla_b_a,
           gla_norm_g, diff_lam, diff_norm_g, w_out, ln_g, ln_b):
    depth, d, _ = w_ada.shape
    alpha = (2 * depth) ** 0.25
    n_dec = c.shape[0]
    c_conv = conv_w.shape[2]
    w_gla = H_GLA * DK_GLA
    w_diff = H_DIFF * 2 * DH_DIFF
    in_conv, in_gla = 2 * c_conv, 4 * w_gla + 2 * GLA_RANK
    ff_chunk = 256

    rows = -(-(n_dec + 1) // SUBLANES) * SUBLANES
    cvec = jnp.zeros((rows, d), F32).at[:n_dec].set(c).at[n_dec].set(c_ctx)
    mod = _ada(cvec, w_ada, b_ada).reshape(depth, rows, N_MOD, d)

    cos, sin = _rope_tables(x_sample.shape[1])

    layers = []
    for l in range(depth):
        wi = w_in[l]
        w_lr = jnp.pad(wi[:, in_conv + 4 * w_gla:in_conv + in_gla], ((0, 0), (0, LANES - 2 * GLA_RANK)))
        w_all = jnp.concatenate([wi[:, :in_conv + 4 * w_gla], wi[:, in_conv + in_gla:], w_lr], axis=1).astype(BF16)
        w2 = jnp.zeros((LANES, 2 * w_gla), F32)
        w2 = w2.at[:GLA_RANK, :w_gla].set(gla_w_a2[l, 0]).at[GLA_RANK:2 * GLA_RANK, w_gla:].set(gla_w_a2[l, 1])
        layers.append(dict(
            ffn1=_ffn_weights(w_ffn1_in[l], w_ffn1_out[l], ff_chunk),
            ffn2=_ffn_weights(w_ffn2_in[l], w_ffn2_out[l], ff_chunk),
            w_all=w_all, w2=w2.astype(BF16), b2=gla_b_a[l].reshape(1, 2 * w_gla),
            conv_w=conv_w[l], conv_b=conv_b[l][None], conv_g=conv_ln_g[l][None], conv_beta=conv_ln_b[l][None],
            gn=jnp.tile(gla_norm_g[l], H_GLA)[None], lam=diff_lam[l], dg=diff_norm_g[l][:, None],
            w_out=w_out[l].astype(BF16),
            lng=[ln_g[l, i][None] for i in range(3)], lnb=[ln_b[l, i][None] for i in range(3)],
            lam_init=0.8 - 0.6 * math.exp(-0.3 * l),
        ))

    def run_layer(x, mod_l, P, ctx):
        t = x.shape[1]
        tm = _pick_tile(t, 512)
        bsz, _, d_model = x.shape
        shared_mod = mod_l.shape[0] == 1
        tf = _pick_tile(bsz * t if shared_mod else t, 1024)

        def ffn(x, which, row0, ln_idx):
            y = _ffn(x.reshape(bsz * t, d_model), mod_l, *P[which], P['lng'][ln_idx], P['lnb'][ln_idx],
                     row0=row0, alpha=alpha, tm=tf)
            return y.reshape(bsz, t, d_model)

        x = ffn(x, 'ffn1', 0, 0)
        rope = ctx is not None
        yglu, gq, gk, gv, gate, gdec, dq, dk, dvt, *dv = _mixin(
            x, mod_l, P['w_all'], P['w2'], P['b2'], cos if rope else None, sin if rope else None, tm=tm)
        yconv = _conv(yglu, P['conv_w'], P['conv_b'], P['conv_g'], P['conv_beta'], tc=_pick_tile(t, 256))
        gla_out = _gla(gq, gk, gv, gdec, ctx['s0'] if rope else None, tb=_pick_tile(t, 512), want_state=not rope)
        ydiff = _attn(dq, dk, dvt, ctx['k'] if rope else None, ctx['vt'] if rope else None, P['lam'], P['dg'],
                      lam_init=P['lam_init'], tq=_pick_tile(t, 512))
        x = _mixout(x, mod_l, yconv, gla_out[0], gla_out[1], gate, ydiff, P['gn'], P['w_out'], P['lng'][1], P['lnb'][1],
                    alpha=alpha, tm=tm)
        x = ffn(x, 'ffn2', 6, 2)
        return x, (dk, dv[0] if dv else None, gla_out[2] if not rope else None)

    xp = x_prompt
    new_k, new_v, new_s = [], [], []
    for l in range(depth):
        xp, (k_l, v_l, s_l) = run_layer(xp, mod[l, n_dec:n_dec + 1], layers[l], None)
        new_k.append(k_l)
        new_v.append(v_l)
        bsz = s_l.shape[0]
        s6 = s_l.reshape(bsz, 2, H_GLA, DK_GLA, H_GLA, DK_GLA)
        s_heads = jnp.stack([s6[:, :, h, :, h, :] for h in range(H_GLA)], axis=2)
        new_s.append(jnp.swapaxes(s_heads, -1, -2))

    xs = x_sample
    for l in range(depth):
        st = jnp.swapaxes(state_gla[:, l], -1, -2)
        s0 = jnp.zeros(st.shape[:2] + (H_GLA, DK_GLA, H_GLA, DK_GLA), F32)
        for h in range(H_GLA):
            s0 = s0.at[:, :, h, :, h, :].set(st[:, :, h])
        s0 = s0.reshape(st.shape[0], 2, w_gla, w_gla)
        past = cache_diff_v.shape[3]
        tile = _pick_tile(x_sample.shape[1], 512)
        assert past % tile == 0, "cached context length must be a whole number of key tiles"
        cvt = cache_diff_v[:, l].astype(BF16).reshape(n_dec, H_DIFF, past // tile, tile, LANES)
        cvt = jnp.concatenate([jnp.swapaxes(cvt, -1, -2), jnp.ones(cvt.shape[:3] + (ONES_ROWS, tile), BF16)], axis=3)
        ctx = dict(k=cache_diff_k[:, l].astype(BF16), vt=cvt, s0=s0)
        xs, _ = run_layer(xs, mod[l, :n_dec], layers[l], ctx)

    return (xp, xs, jnp.stack(new_k, axis=1), jnp.stack(new_v, axis=1), jnp.stack(new_s, axis=1))
```

```python
import functools
import math

import jax
import jax.numpy as jnp
from jax import lax
from jax.experimental import pallas as pl
from jax.experimental.pallas import tpu as pltpu

F32 = jnp.float32
BF16 = jnp.bfloat16

GRID_W = 64
CONV_WIDTH = 31
H_GLA = 4
DK_GLA = 64
GLA_RANK = 16
GLA_TAU = 16.0
GLA_CHUNK = 64
H_DIFF = 4
DH_DIFF = 64
ROPE_BASE = 10000.0
LN_EPS = 1e-5
N_MOD = 9

LANES = 128
SUBLANES = 8
V7X_VMEM_BYTES = 64 * 1024 * 1024
VMEM_LIMIT = V7X_VMEM_BYTES - 8 * 1024 * 1024

CONV_HALO = 16
CONV_ROWS = 32
ATTN_STRIP = 32
MIXOUT_PARTS = 2
ATTN_KEY_TILE = 512
ATTN_LAG = 1
ATTN_SLOTS = 3
ONES_ROWS = 16
ROW_STRIP = 32
LN_STRIP = 64


def _cparams(sem):
    return pltpu.CompilerParams(dimension_semantics=sem, vmem_limit_bytes=VMEM_LIMIT)


def _const_spec(shape):
    nd = len(shape)
    return pl.BlockSpec(shape, lambda *_: (0,) * nd, pipeline_mode=pl.Buffered(1))


def _dot(a, b):
    return jnp.dot(a, b, preferred_element_type=F32)


def _dot_nt(a, b):
    return lax.dot_general(a, b, (((1,), (1,)), ((), ())), preferred_element_type=F32)


def _dot_tn(a, b):
    return lax.dot_general(a, b, (((0,), (0,)), ((), ())), preferred_element_type=F32)


def _silu(x):
    return x * jax.nn.sigmoid(x)


def _layer_norm(y, g, b):
    mu = jnp.mean(y, axis=-1, keepdims=True)
    yc = y - mu
    var = jnp.mean(yc * yc, axis=-1, keepdims=True)
    return yc * lax.rsqrt(var + LN_EPS) * g + b


def _modulate(x, mod_ref, row0):
    sh = mod_ref[0, row0:row0 + 1, :]
    sc = mod_ref[0, row0 + 1:row0 + 2, :]
    g = mod_ref[0, row0 + 2:row0 + 3, :]
    return x * (1.0 + sc) + sh, g


def _ada_kernel(c_ref, w_ref, b_ref, o_ref):
    s = _silu(c_ref[...]).astype(BF16)
    o_ref[0] = _dot(s, w_ref[0].astype(BF16)) + b_ref[0]


def _ada(cvec, w_ada, b_ada):
    depth, d, n = w_ada.shape
    rows = cvec.shape[0]
    tn = d
    return pl.pallas_call(
        _ada_kernel,
        grid=(depth, n // tn),
        in_specs=[
            pl.BlockSpec((rows, d), lambda l, j: (0, 0)),
            pl.BlockSpec((1, d, tn), lambda l, j: (l, 0, j)),
            pl.BlockSpec((1, 1, tn), lambda l, j: (l, 0, j)),
        ],
        out_specs=pl.BlockSpec((1, rows, tn), lambda l, j: (l, 0, j)),
        out_shape=jax.ShapeDtypeStruct((depth, rows, n), F32),
        compiler_params=_cparams(("arbitrary", "arbitrary")),
        name="ada_mod",
    )(cvec, w_ada, b_ada.reshape(depth, 1, n))


def _strips(n_rows, fn):
    def body(i, carry):
        fn(pl.ds(pl.multiple_of(i * ROW_STRIP, ROW_STRIP), ROW_STRIP))
        return carry

    lax.fori_loop(0, n_rows // ROW_STRIP, body, 0, unroll=2)


def _layer_norm_rows(n_rows, make_y, o_ref, g, b):
    n = n_rows // LN_STRIP
    sl = lambda i: pl.ds(i * LN_STRIP if isinstance(i, int) else pl.multiple_of(i * LN_STRIP, LN_STRIP), LN_STRIP)

    def means(i):
        y = make_y(sl(i))
        o_ref[sl(i), :] = y
        return jnp.mean(y, axis=-1, keepdims=True)

    def rstd(i, mu):
        yc = o_ref[sl(i), :] - mu
        return lax.rsqrt(jnp.mean(yc * yc, axis=-1, keepdims=True) + LN_EPS)

    def normalise(i, mu, r):
        o_ref[sl(i), :] = (o_ref[sl(i), :] - mu) * r * g + b

    if n == 1:
        mu = means(0)
        normalise(0, mu, rstd(0, mu))
        return
    mu_a = means(0)
    mu_b = means(1)
    r_a = rstd(0, mu_a)

    def body(i, carry):
        mu_a, r_a, mu_b = carry
        r_b = rstd(i - 1, mu_b)
        mu_c = means(i)
        normalise(i - 2, mu_a, r_a)
        return mu_b, r_b, mu_c

    mu_a, r_a, mu_b = lax.fori_loop(2, n, body, (mu_a, r_a, mu_b))
    r_b = rstd(n - 1, mu_b)
    normalise(n - 2, mu_a, r_a)
    normalise(n - 1, mu_b, r_b)


def _ffn_kernel(x_ref, mod_ref, win_ref, wout_ref, lng_ref, lnb_ref, o_ref, h_scr, acc_ref, y_scr, *, row0, alpha):
    i = pl.program_id(0)
    n_tiles = pl.num_programs(0) - 1
    tm = x_ref.shape[0]
    n_chunks, ck, _ = wout_ref.shape
    lng = lng_ref[...]
    lnb = lnb_ref[...]
    n_strips = tm // LN_STRIP
    unroll = 2 if n_chunks % 2 else 1
    n_iters = (n_chunks - 1) // unroll
    per_iter = -(-n_strips // (n_iters + 1))

    def norm_strip(idx):
        idx = jnp.minimum(idx, n_strips - 1)
        rs = pl.ds(pl.multiple_of(idx * LN_STRIP, LN_STRIP), LN_STRIP)
        o_ref[rs, :] = _layer_norm(y_scr[rs, :], lng, lnb)

    @pl.when(i == 0)
    def _():
        y_scr[...] = jnp.zeros_like(y_scr)

    @pl.when(i < n_tiles)
    def _():
        sh = mod_ref[0, row0:row0 + 1, :]
        sc1 = 1.0 + mod_ref[0, row0 + 1:row0 + 2, :]
        half_g = 0.5 * mod_ref[0, row0 + 2:row0 + 3, :]

        def modulate(rs):
            h_scr[rs, :] = (x_ref[rs, :] * sc1 + sh).astype(BF16)

        _strips(tm, modulate)

        def chunk(j):
            cols = lambda start: pl.ds(pl.multiple_of(start, ck), ck)
            h = h_scr[...]
            a = _dot(h, win_ref[:, cols(j * ck)])
            b = _dot(h, win_ref[:, cols(n_chunks * ck + j * ck)])
            return _dot((_silu(a) * b).astype(BF16), wout_ref[j])

        acc_ref[...] = chunk(0)
        for s in range(per_iter):
            norm_strip(s)

        def body(k, carry):
            for u in range(unroll):
                acc_ref[...] += chunk(1 + k * unroll + u)
            for s in range(per_iter):
                norm_strip((k + 1) * per_iter + s)
            return carry

        lax.fori_loop(0, n_iters, body, 0)

        def residual(rs):
            y_scr[rs, :] = alpha * x_ref[rs, :] + half_g * acc_ref[rs, :]

        _strips(tm, residual)

    @pl.when(i == n_tiles)
    def _():
        _layer_norm_rows(tm, lambda rs: y_scr[rs, :], o_ref, lng, lnb)


def _ffn(x, mod, win, wout, lng, lnb, *, row0, alpha, tm):
    n_tok, d = x.shape
    n_tiles = n_tok // tm
    per_req = n_tok // mod.shape[0]
    cur = lambda ti: jnp.minimum(ti, n_tiles - 1)
    kern = functools.partial(_ffn_kernel, row0=row0, alpha=alpha)
    return pl.pallas_call(
        kern,
        grid=(n_tiles + 1,),
        in_specs=[
            pl.BlockSpec((tm, d), lambda ti: (cur(ti), 0)),
            pl.BlockSpec((1, N_MOD, d), lambda ti: ((cur(ti) * tm) // per_req, 0, 0)),
            _const_spec(win.shape),
            _const_spec(wout.shape),
            _const_spec((1, d)),
            _const_spec((1, d)),
        ],
        out_specs=pl.BlockSpec((tm, d), lambda ti: (jnp.maximum(ti - 1, 0), 0)),
        out_shape=jax.ShapeDtypeStruct((n_tok, d), F32),
        scratch_shapes=[pltpu.VMEM((tm, d), BF16), pltpu.VMEM((tm, d), F32), pltpu.VMEM((tm, d), F32)],
        compiler_params=_cparams(("arbitrary",)),
        name="ffn",
    )(x, mod, win, wout, lng, lnb)


def _log_sigmoid(x):
    return jnp.minimum(x, 0.0) - jnp.log(1.0 + jnp.exp(-jnp.abs(x)))


def _rope(x, cos, sin_signed, first_half):
    rot = jnp.where(first_half, pltpu.roll(x, LANES - 16, 1), pltpu.roll(x, 16, 1))
    return x * cos + rot * sin_signed


def _mixin_kernel(*refs, rope, c_conv, w_gla, w_diff):
    if rope:
        (x_ref, mod_ref, w_ref, w2_ref, b2_ref, cos_ref, sin_ref,
         yglu_ref, gq_ref, gk_ref, gv_ref, gate_ref, gdec_ref, dq_ref, dk_ref, dvt_ref) = refs
    else:
        (x_ref, mod_ref, w_ref, w2_ref, b2_ref,
         yglu_ref, gq_ref, gk_ref, gv_ref, gate_ref, gdec_ref, dq_ref, dk_ref, dvt_ref, dv_ref) = refs
    x = x_ref[0]
    h, _ = _modulate(x, mod_ref, 3)
    h = h.astype(BF16)
    o_gla = 2 * c_conv
    o_diff = o_gla + 4 * w_gla
    o_lr = o_diff + 3 * w_diff

    tm = x.shape[0]
    halves = [slice(0, tm // 2), slice(tm // 2, tm)] if tm % (2 * LANES) == 0 else [slice(0, tm)]
    zs = []
    for rs in halves:
        hh = h[rs]
        zs.append((_dot(hh, w_ref[:, 0:o_gla]), _dot(hh, w_ref[:, o_gla:o_diff]),
                   _dot(hh, w_ref[:, o_lr:o_lr + LANES]).astype(BF16), _dot(hh, w_ref[:, o_diff:o_lr])))
    pres = [_dot(z[2], w2_ref[...]) + b2_ref[...] for z in zs]

    for rs, (zc, zg, _, zd), pre in zip(halves, zs, pres):
        yglu_ref[0, rs, :] = zc[:, :c_conv] * jax.nn.sigmoid(zc[:, c_conv:])
        gq_ref[0, rs, :] = zg[:, 0:w_gla] * (DK_GLA ** -0.5)
        gk_ref[0, rs, :] = zg[:, w_gla:2 * w_gla]
        gv_ref[0, rs, :] = zg[:, 2 * w_gla:3 * w_gla].astype(BF16)
        gate_ref[0, rs, :] = zg[:, 3 * w_gla:4 * w_gla]
        gdec_ref[0, rs, :] = _log_sigmoid(pre) * (1.0 / GLA_TAU)
        if rope:
            cos = cos_ref[rs, :]
            sin = sin_ref[rs, :]
            lane = lax.broadcasted_iota(jnp.int32, cos.shape, 1)
            first_half = (lane % 32) < 16
        for hd in range(H_DIFF):
            sl = slice(hd * LANES, (hd + 1) * LANES)
            q = zd[:, sl]
            k = zd[:, w_diff + hd * LANES:w_diff + (hd + 1) * LANES]
            v = zd[:, 2 * w_diff + hd * LANES:2 * w_diff + (hd + 1) * LANES]
            if rope:
                q = _rope(q, cos, sin, first_half)
                k = _rope(k, cos, sin, first_half)
            dq_ref[0, rs, sl] = (q * (DH_DIFF ** -0.5 * math.log2(math.e))).astype(BF16)
            dk_ref[0, hd, rs, :] = k.astype(dk_ref.dtype)
            dvt_ref[0, hd, 0, 0:LANES, rs] = v.T.astype(BF16)
            if not rope:
                dv_ref[0, hd, rs, :] = v
    for hd in range(H_DIFF):
        dvt_ref[0, hd, 0, LANES:, :] = jnp.ones((ONES_ROWS, tm), BF16)


def _mixin(x, mod, w, w2, b2, cos, sin, *, tm):
    b, t, d = x.shape
    rope = cos is not None
    c_conv = 256
    w_gla = H_GLA * DK_GLA
    w_diff = H_DIFF * 2 * DH_DIFF
    kv_dtype = BF16 if rope else F32
    mod_rows = mod.shape[0]
    mod_idx = (lambda bi, ti: (bi, 0, 0)) if mod_rows > 1 else (lambda bi, ti: (0, 0, 0))
    tok = lambda n: pl.BlockSpec((1, tm, n), lambda bi, ti: (bi, ti, 0))
    heads = pl.BlockSpec((1, H_DIFF, tm, LANES), lambda bi, ti: (bi, 0, ti, 0))
    in_specs = [tok(d), pl.BlockSpec((1, N_MOD, d), mod_idx),
                _const_spec(w.shape), _const_spec(w2.shape), _const_spec(b2.shape)]
    args = [x, mod, w, w2, b2]
    if rope:
        in_specs += [pl.BlockSpec((tm, LANES), lambda bi, ti: (ti, 0))] * 2
        args += [cos, sin]
    sds = jax.ShapeDtypeStruct
    out_shape = [
        sds((b, t, c_conv), F32),
        sds((b, t, w_gla), F32),
        sds((b, t, w_gla), F32),
        sds((b, t, w_gla), BF16),
        sds((b, t, w_gla), F32),
        sds((b, t, 2 * w_gla), F32),
        sds((b, t, w_diff), BF16),
        sds((b, H_DIFF, t, LANES), kv_dtype),
        sds((b, H_DIFF, t // tm, LANES + ONES_ROWS, tm), BF16),
    ]
    out_specs = [tok(c_conv), tok(w_gla), tok(w_gla), tok(w_gla), tok(w_gla), tok(2 * w_gla),
                 tok(w_diff), heads,
                 pl.BlockSpec((1, H_DIFF, 1, LANES + ONES_ROWS, tm), lambda bi, ti: (bi, 0, ti, 0, 0))]
    if not rope:
        out_shape.append(sds((b, H_DIFF, t, LANES), F32))
        out_specs.append(heads)
    kern = functools.partial(_mixin_kernel, rope=rope, c_conv=c_conv, w_gla=w_gla, w_diff=w_diff)
    return pl.pallas_call(
        kern,
        grid=(b, t // tm),
        in_specs=in_specs,
        out_specs=out_specs,
        out_shape=out_shape,
        compiler_params=_cparams(("parallel", "parallel")),
        name="mixer_in",
    )(*args)


def _conv_kernel(prev_ref, cur_ref, next_ref, w_ref, b_ref, g_ref, beta_ref, o_ref, win_ref, *, tc, nt):
    i = pl.program_id(1)
    pad = CONV_WIDTH // 2
    win_ref[0:CONV_HALO] = jnp.where(i > 0, prev_ref[0], 0.0)
    win_ref[CONV_HALO:CONV_HALO + tc] = cur_ref[0]
    win_ref[CONV_HALO + tc:2 * CONV_HALO + tc] = jnp.where(i < nt - 1, next_ref[0], 0.0)
    w = w_ref[...]
    first = CONV_HALO - pad
    n_groups = -(-(first + CONV_WIDTH) // SUBLANES)
    for r in range(0, tc, CONV_ROWS):
        wide = win_ref[r:r + CONV_ROWS + n_groups * SUBLANES, :]
        acc = None
        for s in range(SUBLANES):
            z = None
            for a in range(n_groups):
                k = a * SUBLANES + s - first
                if 0 <= k < CONV_WIDTH:
                    term = wide[a * SUBLANES:a * SUBLANES + CONV_ROWS + SUBLANES, :] * w[k:k + 1, :]
                    z = term if z is None else z + term
            if z is not None:
                acc = z[s:s + CONV_ROWS, :] if acc is None else acc + z[s:s + CONV_ROWS, :]
        y = _layer_norm(acc + b_ref[...], g_ref[...], beta_ref[...])
        o_ref[0, r:r + CONV_ROWS, :] = _silu(y).astype(o_ref.dtype)


def _conv(yglu, w, bias, g, beta, *, tc):
    b, t, c = yglu.shape
    nt = t // tc
    hb = tc // CONV_HALO
    n_halo = t // CONV_HALO
    kern = functools.partial(_conv_kernel, tc=tc, nt=nt)
    return pl.pallas_call(
        kern,
        grid=(b, nt),
        in_specs=[
            pl.BlockSpec((1, CONV_HALO, c), lambda bi, i: (bi, jnp.maximum(i * hb - 1, 0), 0)),
            pl.BlockSpec((1, tc, c), lambda bi, i: (bi, i, 0)),
            pl.BlockSpec((1, CONV_HALO, c), lambda bi, i: (bi, jnp.minimum((i + 1) * hb, n_halo - 1), 0)),
            _const_spec(w.shape), _const_spec((1, c)), _const_spec((1, c)), _const_spec((1, c)),
        ],
        out_specs=pl.BlockSpec((1, tc, c), lambda bi, i: (bi, i, 0)),
        out_shape=jax.ShapeDtypeStruct((b, t, c), BF16),
        scratch_shapes=[pltpu.VMEM((tc + 2 * CONV_HALO, c), F32)],
        compiler_params=_cparams(("parallel", "parallel")),
        name="conv_module",
    )(yglu, yglu, yglu, w, bias, g, beta)


def _split3(x):
    hi = x.astype(BF16)
    r1 = x - hi.astype(F32)
    mid = r1.astype(BF16)
    lo = (r1 - mid.astype(F32)).astype(BF16)
    return hi, mid, lo


def _gla_kernel(*refs, nc, has_state, want_state):
    it = iter(refs)
    ins = [[next(it) for _ in range(4)] for _ in range(2)]
    s0_ref = next(it) if has_state else None
    o_refs = [next(it), next(it)]
    sfin_ref = next(it) if want_state else None
    s_scr, qt_scr, kv_scr, dec_scr, sin_scr = (next(it) for _ in range(5))

    i = pl.program_id(1)
    nblk = pl.num_programs(1)
    c = GLA_CHUNK
    width = s_scr.shape[1]
    nh = width // c

    @pl.when(i == 0)
    def _():
        if has_state:
            s_scr[...] = s0_ref[0]
        else:
            s_scr[...] = jnp.zeros_like(s_scr)

    row = lax.broadcasted_iota(jnp.int32, (c, c), 0)
    col = lax.broadcasted_iota(jnp.int32, (c, c), 1)
    rowa = lax.broadcasted_iota(jnp.int32, (c, width), 0)
    cola = lax.broadcasted_iota(jnp.int32, (c, width), 1) % c
    tri_b = [jnp.where(row >= col, 1.0, 0.0).astype(BF16), jnp.where(row <= col, 1.0, 0.0).astype(BF16)]
    tri_cat = [rowa >= cola, rowa <= cola]
    rb = lax.broadcasted_iota(jnp.int32, (width, width), 0) // c
    cb = lax.broadcasted_iota(jnp.int32, (width, width), 1) // c
    bd = rb == cb
    rows = [slice(ci * c, (ci + 1) * c) for ci in range(nc)]
    work = [(d, ci) for ci in range(nc) for d in range(2)]

    gcums = {}
    for d, ci in work:
        ghi, gmid, glo = _split3(ins[d][3][0, rows[ci], :])
        gcums[d, ci] = _dot(tri_b[d], ghi) + _dot(tri_b[d], gmid) + _dot(tri_b[d], glo)
    scores = {}
    for d, ci in work:
        q_ref, k_ref, v_ref, _ = ins[d]
        rs = rows[ci]
        gcum = gcums[d, ci]
        gtot = gcum[c - 1:c, :] if d == 0 else gcum[0:1, :]
        k = k_ref[0, rs, :]
        q_t = (q_ref[0, rs, :] * jnp.exp(gcum)).astype(BF16)
        k_t = k * jnp.exp(-gcum)
        k_hat = (k * jnp.exp(gtot - gcum)).astype(BF16)
        k_bd = jnp.where(bd, jnp.concatenate([k_t] * nh, axis=0), 0.0).astype(BF16)
        scores[d, ci] = _dot_nt(q_t, k_bd)
        qt_scr[d, rs, :] = q_t
        kv_scr[d, ci] = jnp.where(bd, _dot_tn(v_ref[0, rs, :], k_hat), 0.0)
        dec_scr[d, ci] = jnp.broadcast_to(jnp.exp(gtot), dec_scr.shape[2:])
    for d, ci in work:
        v = ins[d][2][0, rows[ci], :]
        a = jnp.where(tri_cat[d], scores[d, ci], 0.0).astype(BF16)
        v_bd = jnp.where(bd, jnp.concatenate([v] * nh, axis=0), jnp.zeros((), v.dtype))
        o_refs[d][0, rows[ci], :] = _dot(a, v_bd)

    scan = [(d, j, j if d == 0 else nc - 1 - j) for j in range(nc) for d in range(2)]
    for d, j, ci in scan:
        s_in = s_scr[d]
        sin_scr[d, j] = s_in.astype(BF16)
        s_scr[d] = s_in * dec_scr[d, ci, 0:1, :] + kv_scr[d, ci]
    for d, j, ci in scan:
        o_refs[d][0, rows[ci], :] += _dot_nt(qt_scr[d, rows[ci], :], sin_scr[d, j])

    if want_state:
        @pl.when(i == nblk - 1)
        def _():
            sfin_ref[0] = s_scr[...]


def _gla(q, k, v, gdec, s0, *, tb, want_state):
    b, t, w = q.shape
    nblk = t // tb
    nc = tb // GLA_CHUNK
    blk = [lambda i: i, lambda i: nblk - 1 - i]
    in_specs, args = [], []
    for d in range(2):
        tok = pl.BlockSpec((1, tb, w), lambda bi, i, d=d: (bi, blk[d](i), 0))
        in_specs += [tok, tok, tok, pl.BlockSpec((1, tb, w), lambda bi, i, d=d: (bi, blk[d](i), d))]
        args += [q, k, v, gdec]
    has_state = s0 is not None
    if has_state:
        in_specs.append(pl.BlockSpec((1, 2, w, w), lambda bi, i: (bi, 0, 0, 0)))
        args.append(s0)
    out_shape = [jax.ShapeDtypeStruct((b, t, w), F32)] * 2
    out_specs = [pl.BlockSpec((1, tb, w), lambda bi, i, d=d: (bi, blk[d](i), 0)) for d in range(2)]
    if want_state:
        out_shape.append(jax.ShapeDtypeStruct((b, 2, w, w), F32))
        out_specs.append(pl.BlockSpec((1, 2, w, w), lambda bi, i: (bi, 0, 0, 0)))
    kern = functools.partial(_gla_kernel, nc=nc, has_state=has_state, want_state=want_state)
    return pl.pallas_call(
        kern,
        grid=(b, nblk),
        in_specs=in_specs,
        out_specs=out_specs,
        out_shape=out_shape,
        scratch_shapes=[pltpu.VMEM((2, w, w), F32), pltpu.VMEM((2, tb, w), BF16),
                        pltpu.VMEM((2, nc, w, w), F32), pltpu.VMEM((2, nc, SUBLANES, w), F32),
                        pltpu.VMEM((2, nc, w, w), BF16)],
        compiler_params=_cparams(("parallel", "arbitrary")),
        name="gla",
    )(*args)


def _attn_kernel(*refs, has_cache, lam_init, n_slots):
    it = iter(refs)
    q_ref, k_ref, vt_ref = next(it), next(it), next(it)
    kc_ref, vtc_ref = (next(it), next(it)) if has_cache else (None, None)
    lam_ref, g_ref, o_ref = next(it), next(it), next(it)
    s_scr = [next(it) for _ in range(n_slots)]
    e_scr = [next(it) for _ in range(n_slots)]
    acc_scr = next(it)
    p = lam_ref[...]
    lam = (jnp.exp(jnp.sum(p[0:1] * p[1:2], axis=-1, keepdims=True))
           - jnp.exp(jnp.sum(p[2:3] * p[3:4], axis=-1, keepdims=True)) + lam_init)
    for hh in range(k_ref.shape[1]):
        lanes = pl.ds(hh * LANES, LANES)
        _attn_head(q_ref.at[0, :, lanes], k_ref.at[0, hh], vt_ref.at[0, hh],
                   kc_ref.at[0, hh] if has_cache else None, vtc_ref.at[0, hh] if has_cache else None,
                   lam, g_ref, o_ref.at[0, :, lanes], s_scr, e_scr, acc_scr, lam_init)


def _attn_head(q_ref, k_ref, vt_ref, kc_ref, vtc_ref, lam, g_ref, o_ref, s_scr, e_scr, acc_scr, lam_init):
    has_cache = kc_ref is not None
    n_slots = len(s_scr)
    q = q_ref[...]
    tq = q.shape[0]
    lane = lax.broadcasted_iota(jnp.int32, q.shape, 1)
    zero = jnp.zeros((), q.dtype)
    qs = jnp.concatenate([jnp.where(lane < DH_DIFF, q, zero), jnp.where(lane >= DH_DIFF, q, zero)], axis=0)

    tk = vt_ref.shape[2]
    nq = 2 * tq
    acc_scr[...] = jnp.zeros_like(acc_scr)
    group = s_scr[0].shape[0] // tk
    tiles = []
    for kr, vr in [(k_ref, vt_ref)] + ([(kc_ref, vtc_ref)] if has_cache else []):
        for first in range(0, vr.shape[0], group):
            tiles.append((kr, vr, first, min(group, vr.shape[0] - first)))
    n = len(tiles)

    def scores(j, slot):
        kr, _, first, cnt = tiles[j]
        rows = cnt * tk
        s_scr[slot][0:rows, :] = _dot_nt(kr[first * tk:first * tk + rows, :].astype(BF16), qs)
        strips = [s_scr[slot][r:r + ATTN_STRIP, :] for r in range(0, rows, ATTN_STRIP)]
        return jnp.max(functools.reduce(jnp.maximum, strips), axis=0, keepdims=True)

    def softmax(j, slot, mt, m):
        m_new = jnp.maximum(m, mt)
        alpha = jnp.exp2(m - m_new)
        for r in range(0, tiles[j][3] * tk, ATTN_STRIP):
            e_scr[slot][r:r + ATTN_STRIP, :] = jnp.exp2(s_scr[slot][r:r + ATTN_STRIP, :] - m_new).astype(BF16)
        return m_new, alpha

    def weighted(j, slot, alpha):
        _, vr, first, cnt = tiles[j]
        pv = functools.reduce(lambda a, b: a + b, [_dot(vr[first + p], e_scr[slot][p * tk:(p + 1) * tk, :])
                                                   for p in range(cnt)])
        acc_scr[...] = acc_scr[...] * alpha + pv

    slot = lambda j: j % n_slots

    lag = min(ATTN_LAG, n_slots - 1)
    m = jnp.full((1, nq), -1e30, F32)
    mt = scores(0, slot(0))
    alphas = {}
    for j in range(n + lag):
        if j < n:
            mt_next = scores(j + 1, slot(j + 1)) if j + 1 < n else None
            m, alphas[j] = softmax(j, slot(j), mt, m)
            mt = mt_next
        if 0 <= j - lag < n:
            weighted(j - lag, slot(j - lag), alphas.pop(j - lag))
    dv = o_ref.shape[1]
    on = acc_scr[0:dv, :] / acc_scr[dv:dv + 1, :]
    o = on[:, :tq] - lam * on[:, tq:]
    ms = jnp.mean(o * o, axis=0, keepdims=True)
    y = o * lax.rsqrt(ms + LN_EPS) * g_ref[...] * (1.0 - lam_init)
    o_ref[...] = y.T.astype(o_ref.dtype)


def _attn(dq, dk, dvt, ck, cvt, lam_p, g_col, *, lam_init, tq):
    b, t, _ = dq.shape
    dv, tk = dvt.shape[3:]
    has_cache = ck is not None
    group = max(1, min(ATTN_KEY_TILE // tk, dvt.shape[2]))
    n_tiles = -(-dvt.shape[2] // group) + (-(-cvt.shape[2] // group) if has_cache else 0)
    n_slots = min(n_tiles, ATTN_SLOTS)
    rows = group * tk
    hps = H_DIFF if n_tiles == 1 else 1
    qspec = pl.BlockSpec((1, tq, hps * LANES), lambda bi, h, qi: (bi, qi, h))
    kspec = lambda a: pl.BlockSpec((1, hps) + a.shape[2:], lambda bi, h, qi: (bi, h, 0, 0))
    vspec = lambda a: pl.BlockSpec((1, hps) + a.shape[2:], lambda bi, h, qi: (bi, h, 0, 0, 0))
    kv_args = [dk, dvt] + ([ck, cvt] if has_cache else [])
    kv_specs = [kspec(dk), vspec(dvt)] + ([kspec(ck), vspec(cvt)] if has_cache else [])
    kern = functools.partial(_attn_kernel, has_cache=has_cache, lam_init=lam_init, n_slots=n_slots)
    return pl.pallas_call(
        kern,
        grid=(b, H_DIFF // hps, t // tq),
        in_specs=[qspec] + kv_specs + [_const_spec(lam_p.shape), _const_spec(g_col.shape)],
        out_specs=qspec,
        out_shape=jax.ShapeDtypeStruct(dq.shape, BF16),
        scratch_shapes=([pltpu.VMEM((rows, 2 * tq), F32)] * n_slots + [pltpu.VMEM((rows, 2 * tq), BF16)] * n_slots
                        + [pltpu.VMEM((dv, 2 * tq), F32)]),
        compiler_params=_cparams(("parallel", "parallel", "parallel")),
        name="diff_attn",
    )(dq, *kv_args, lam_p, g_col)


def _mixout_kernel(x_ref, mod_ref, yc_ref, of_ref, ob_ref, gate_ref, yd_ref, gn_ref, w_ref,
                   lng_ref, lnb_ref, o_ref, *, alpha):
    g1 = mod_ref[0, 5:6, :]
    tm = x_ref.shape[1]
    width = of_ref.shape[2]
    c_conv = yc_ref.shape[2]
    rb = lax.broadcasted_iota(jnp.int32, (width, width), 0) // DK_GLA
    cb = lax.broadcasted_iota(jnp.int32, (width, width), 1) // DK_GLA
    ones_bd = jnp.where(rb == cb, 1.0, 0.0).astype(BF16)
    n_parts = MIXOUT_PARTS if tm % (MIXOUT_PARTS * LANES) == 0 else 1
    parts = [slice(i * tm // n_parts, (i + 1) * tm // n_parts) for i in range(n_parts)]
    sums = []
    for rs in parts:
        o = of_ref[0, rs, :] + ob_ref[0, rs, :]
        sq = o * o
        hi = sq.astype(BF16)
        lo = (sq - hi.astype(F32)).astype(BF16)
        sums.append((o, _dot(hi, ones_bd) + _dot(lo, ones_bd)))
    ys = []
    for rs, (o, ss) in zip(parts, sums):
        yg = (o * lax.rsqrt(ss * (1.0 / DK_GLA) + LN_EPS) * gn_ref[...] * _silu(gate_ref[0, rs, :])).astype(BF16)
        ys.append(_dot(yc_ref[0, rs, :], w_ref[0:c_conv, :])
                  + _dot(yg, w_ref[c_conv:c_conv + width, :])
                  + _dot(yd_ref[0, rs, :], w_ref[c_conv + width:, :]))
    for rs, y in zip(parts, ys):
        o_ref[0, rs, :] = _layer_norm(alpha * x_ref[0, rs, :] + g1 * y, lng_ref[...], lnb_ref[...])


def _mixout(x, mod, yconv, o_fwd, o_bwd, gate, ydiff, gn, w, lng, lnb, *, alpha, tm):
    b, t, d = x.shape
    mod_rows = mod.shape[0]
    mod_idx = (lambda bi, ti: (bi, 0, 0)) if mod_rows > 1 else (lambda bi, ti: (0, 0, 0))
    tok = lambda n: pl.BlockSpec((1, tm, n), lambda bi, ti: (bi, ti, 0))
    wg = o_fwd.shape[2]
    kern = functools.partial(_mixout_kernel, alpha=alpha)
    return pl.pallas_call(
        kern,
        grid=(b, t // tm),
        in_specs=[
            tok(d), pl.BlockSpec((1, N_MOD, d), mod_idx), tok(yconv.shape[2]),
            tok(wg), tok(wg), tok(wg), tok(ydiff.shape[2]),
            _const_spec(gn.shape), _const_spec(w.shape), _const_spec((1, d)), _const_spec((1, d)),
        ],
        out_specs=tok(d),
        out_shape=jax.ShapeDtypeStruct((b, t, d), F32),
        compiler_params=_cparams(("parallel", "parallel")),
        name="mixer_out",
    )(x, mod, yconv, o_fwd, o_bwd, gate, ydiff, gn, w, lng, lnb)


def _rope_tables(t):
    rows = t // GRID_W
    row = jnp.repeat(jnp.arange(rows, dtype=F32), GRID_W)
    col = jnp.tile(jnp.arange(GRID_W, dtype=F32), rows)
    seg = DH_DIFF // 2
    inv = ROPE_BASE ** (-jnp.arange(0, seg, 2, dtype=F32) / seg)
    a_r = row[:, None] * inv
    a_c = col[:, None] * inv
    ang = jnp.concatenate([a_r, a_r, a_c, a_c], axis=-1)
    ang = jnp.concatenate([ang, ang], axis=-1)
    sign = jnp.where((jnp.arange(LANES) % 32) < 16, -1.0, 1.0).astype(F32)
    return jnp.cos(ang), jnp.sin(ang) * sign


def _ffn_weights(w_in, w_out, ck):
    d, two_ff = w_in.shape
    ff = two_ff // 2
    n = ff // ck
    wout = w_out.reshape(n, ck, d).astype(BF16)
    return w_in.astype(BF16), wout


def _pick_tile(t, pref):
    return pref if t % pref == 0 else t


def kernel(x_prompt, x_sample, cache_diff_k, cache_diff_v, state_gla, c, c_ctx, w_ada, b_ada, w_ffn1_in,
           w_ffn1_out, w_ffn2_in, w_ffn2_out, w_in, conv_w, conv_b, conv_ln_g, conv_ln_b, gla_w_a2, gla_b_a,
           gla_norm_g, diff_lam, diff_norm_g, w_out, ln_g, ln_b):
    depth, d, _ = w_ada.shape
    alpha = (2 * depth) ** 0.25
    n_dec = c.shape[0]
    c_conv = conv_w.shape[2]
    w_gla = H_GLA * DK_GLA
    w_diff = H_DIFF * 2 * DH_DIFF
    in_conv, in_gla = 2 * c_conv, 4 * w_gla + 2 * GLA_RANK
    ff_chunk = 256

    rows = -(-(n_dec + 1) // SUBLANES) * SUBLANES
    cvec = jnp.zeros((rows, d), F32).at[:n_dec].set(c).at[n_dec].set(c_ctx)
    mod = _ada(cvec, w_ada, b_ada).reshape(depth, rows, N_MOD, d)

    cos, sin = _rope_tables(x_sample.shape[1])

    layers = []
    for l in range(depth):
        wi = w_in[l]
        w_lr = jnp.pad(wi[:, in_conv + 4 * w_gla:in_conv + in_gla], ((0, 0), (0, LANES - 2 * GLA_RANK)))
        w_all = jnp.concatenate([wi[:, :in_conv + 4 * w_gla], wi[:, in_conv + in_gla:], w_lr], axis=1).astype(BF16)
        w2 = jnp.zeros((LANES, 2 * w_gla), F32)
        w2 = w2.at[:GLA_RANK, :w_gla].set(gla_w_a2[l, 0]).at[GLA_RANK:2 * GLA_RANK, w_gla:].set(gla_w_a2[l, 1])
        layers.append(dict(
            ffn1=_ffn_weights(w_ffn1_in[l], w_ffn1_out[l], ff_chunk),
            ffn2=_ffn_weights(w_ffn2_in[l], w_ffn2_out[l], ff_chunk),
            w_all=w_all, w2=w2.astype(BF16), b2=gla_b_a[l].reshape(1, 2 * w_gla),
            conv_w=conv_w[l], conv_b=conv_b[l][None], conv_g=conv_ln_g[l][None], conv_beta=conv_ln_b[l][None],
            gn=jnp.tile(gla_norm_g[l], H_GLA)[None], lam=diff_lam[l], dg=diff_norm_g[l][:, None],
            w_out=w_out[l].astype(BF16),
            lng=[ln_g[l, i][None] for i in range(3)], lnb=[ln_b[l, i][None] for i in range(3)],
            lam_init=0.8 - 0.6 * math.exp(-0.3 * l),
        ))

    def run_layer(x, mod_l, P, ctx):
        t = x.shape[1]
        tm = _pick_tile(t, 512)
        bsz, _, d_model = x.shape
        shared_mod = mod_l.shape[0] == 1
        tf = _pick_tile(bsz * t if shared_mod else t, 1024)

        def ffn(x, which, row0, ln_idx):
            y = _ffn(x.reshape(bsz * t, d_model), mod_l, *P[which], P['lng'][ln_idx], P['lnb'][ln_idx],
                     row0=row0, alpha=alpha, tm=tf)
            return y.reshape(bsz, t, d_model)

        x = ffn(x, 'ffn1', 0, 0)
        rope = ctx is not None
        yglu, gq, gk, gv, gate, gdec, dq, dk, dvt, *dv = _mixin(
            x, mod_l, P['w_all'], P['w2'], P['b2'], cos if rope else None, sin if rope else None, tm=tm)
        yconv = _conv(yglu, P['conv_w'], P['conv_b'], P['conv_g'], P['conv_beta'], tc=_pick_tile(t, 256))
        gla_out = _gla(gq, gk, gv, gdec, ctx['s0'] if rope else None, tb=_pick_tile(t, 512), want_state=not rope)
        ydiff = _attn(dq, dk, dvt, ctx['k'] if rope else None, ctx['vt'] if rope else None, P['lam'], P['dg'],
                      lam_init=P['lam_init'], tq=_pick_tile(t, 512))
        x = _mixout(x, mod_l, yconv, gla_out[0], gla_out[1], gate, ydiff, P['gn'], P['w_out'], P['lng'][1], P['lnb'][1],
                    alpha=alpha, tm=tm)
        x = ffn(x, 'ffn2', 6, 2)
        return x, (dk, dv[0] if dv else None, gla_out[2] if not rope else None)

    xp = x_prompt
    new_k, new_v, new_s = [], [], []
    for l in range(depth):
        xp, (k_l, v_l, s_l) = run_layer(xp, mod[l, n_dec:n_dec + 1], layers[l], None)
        new_k.append(k_l)
        new_v.append(v_l)
        bsz = s_l.shape[0]
        s6 = s_l.reshape(bsz, 2, H_GLA, DK_GLA, H_GLA, DK_GLA)
        s_heads = jnp.stack([s6[:, :, h, :, h, :] for h in range(H_GLA)], axis=2)
        new_s.append(jnp.swapaxes(s_heads, -1, -2))

    xs = x_sample
    for l in range(depth):
        st = jnp.swapaxes(state_gla[:, l], -1, -2)
        s0 = jnp.zeros(st.shape[:2] + (H_GLA, DK_GLA, H_GLA, DK_GLA), F32)
        for h in range(H_GLA):
            s0 = s0.at[:, :, h, :, h, :].set(st[:, :, h])
        s0 = s0.reshape(st.shape[0], 2, w_gla, w_gla)
        past = cache_diff_v.shape[3]
        tile = _pick_tile(x_sample.shape[1], 512)
        assert past % tile == 0, "cached context length must be a whole number of key tiles"
        cvt = cache_diff_v[:, l].astype(BF16).reshape(n_dec, H_DIFF, past // tile, tile, LANES)
        cvt = jnp.concatenate([jnp.swapaxes(cvt, -1, -2), jnp.ones(cvt.shape[:3] + (ONES_ROWS, tile), BF16)], axis=3)
        ctx = dict(k=cache_diff_k[:, l].astype(BF16), vt=cvt, s0=s0)
        xs, _ = run_layer(xs, mod[l, :n_dec], layers[l], ctx)

    return (xp, xs, jnp.stack(new_k, axis=1), jnp.stack(new_v, axis=1), jnp.stack(new_s, axis=1))
```

```python
import functools
import math

import jax
import jax.numpy as jnp
from jax import lax
from jax.experimental import pallas as pl
from jax.experimental.pallas import tpu as pltpu

F32 = jnp.float32
BF16 = jnp.bfloat16

GRID_W = 64
CONV_WIDTH = 31
H_GLA = 4
DK_GLA = 64
GLA_RANK = 16
GLA_TAU = 16.0
GLA_CHUNK = 64
H_DIFF = 4
DH_DIFF = 64
ROPE_BASE = 10000.0
LN_EPS = 1e-5
N_MOD = 9

LANES = 128
SUBLANES = 8
V7X_VMEM_BYTES = 64 * 1024 * 1024
VMEM_LIMIT = V7X_VMEM_BYTES - 8 * 1024 * 1024

CONV_HALO = 16
CONV_ROWS = 32
ATTN_STRIP = 32
MIXOUT_PARTS = 2
ATTN_KEY_TILE = 512
ATTN_LAG = 1
ATTN_SLOTS = 3
ONES_ROWS = 16
ROW_STRIP = 32
LN_STRIP = 64


def _cparams(sem):
    return pltpu.CompilerParams(dimension_semantics=sem, vmem_limit_bytes=VMEM_LIMIT)


def _const_spec(shape):
    nd = len(shape)
    return pl.BlockSpec(shape, lambda *_: (0,) * nd, pipeline_mode=pl.Buffered(1))


def _dot(a, b):
    return jnp.dot(a, b, preferred_element_type=F32)


def _dot_nt(a, b):
    return lax.dot_general(a, b, (((1,), (1,)), ((), ())), preferred_element_type=F32)


def _dot_tn(a, b):
    return lax.dot_general(a, b, (((0,), (0,)), ((), ())), preferred_element_type=F32)


def _silu(x):
    return x * jax.nn.sigmoid(x)


def _layer_norm(y, g, b):
    mu = jnp.mean(y, axis=-1, keepdims=True)
    yc = y - mu
    var = jnp.mean(yc * yc, axis=-1, keepdims=True)
    return yc * lax.rsqrt(var + LN_EPS) * g + b


def _modulate(x, mod_ref, row0):
    sh = mod_ref[0, row0:row0 + 1, :]
    sc = mod_ref[0, row0 + 1:row0 + 2, :]
    g = mod_ref[0, row0 + 2:row0 + 3, :]
    return x * (1.0 + sc) + sh, g


def _ada_kernel(c_ref, w_ref, b_ref, o_ref):
    s = _silu(c_ref[...]).astype(BF16)
    o_ref[0] = _dot(s, w_ref[0].astype(BF16)) + b_ref[0]


def _ada(cvec, w_ada, b_ada):
    depth, d, n = w_ada.shape
    rows = cvec.shape[0]
    tn = d
    return pl.pallas_call(
        _ada_kernel,
        grid=(depth, n // tn),
        in_specs=[
            pl.BlockSpec((rows, d), lambda l, j: (0, 0)),
            pl.BlockSpec((1, d, tn), lambda l, j: (l, 0, j)),
            pl.BlockSpec((1, 1, tn), lambda l, j: (l, 0, j)),
        ],
        out_specs=pl.BlockSpec((1, rows, tn), lambda l, j: (l, 0, j)),
        out_shape=jax.ShapeDtypeStruct((depth, rows, n), F32),
        compiler_params=_cparams(("arbitrary", "arbitrary")),
        name="ada_mod",
    )(cvec, w_ada, b_ada.reshape(depth, 1, n))


def _strips(n_rows, fn):
    def body(i, carry):
        fn(pl.ds(pl.multiple_of(i * ROW_STRIP, ROW_STRIP), ROW_STRIP))
        return carry

    lax.fori_loop(0, n_rows // ROW_STRIP, body, 0, unroll=2)


def _layer_norm_rows(n_rows, make_y, o_ref, g, b):
    n = n_rows // LN_STRIP
    sl = lambda i: pl.ds(i * LN_STRIP if isinstance(i, int) else pl.multiple_of(i * LN_STRIP, LN_STRIP), LN_STRIP)

    def means(i):
        y = make_y(sl(i))
        o_ref[sl(i), :] = y
        return jnp.mean(y, axis=-1, keepdims=True)

    def rstd(i, mu):
        yc = o_ref[sl(i), :] - mu
        return lax.rsqrt(jnp.mean(yc * yc, axis=-1, keepdims=True) + LN_EPS)

    def normalise(i, mu, r):
        o_ref[sl(i), :] = (o_ref[sl(i), :] - mu) * r * g + b

    if n == 1:
        mu = means(0)
        normalise(0, mu, rstd(0, mu))
        return
    mu_a = means(0)
    mu_b = means(1)
    r_a = rstd(0, mu_a)

    def body(i, carry):
        mu_a, r_a, mu_b = carry
        r_b = rstd(i - 1, mu_b)
        mu_c = means(i)
        normalise(i - 2, mu_a, r_a)
        return mu_b, r_b, mu_c

    mu_a, r_a, mu_b = lax.fori_loop(2, n, body, (mu_a, r_a, mu_b))
    r_b = rstd(n - 1, mu_b)
    normalise(n - 2, mu_a, r_a)
    normalise(n - 1, mu_b, r_b)


def _ffn_kernel(x_ref, mod_ref, win_ref, wout_ref, lng_ref, lnb_ref, o_ref, h_scr, acc_ref, y_scr, *, row0, alpha):
    i = pl.program_id(0)
    n_tiles = pl.num_programs(0) - 1
    tm = x_ref.shape[0]
    n_chunks, ck, _ = wout_ref.shape
    lng = lng_ref[...]
    lnb = lnb_ref[...]
    n_strips = tm // LN_STRIP
    unroll = 2
    n_iters = (n_chunks - 2) // unroll
    tail = range(1 + n_iters * unroll, n_chunks)
    per_iter = -(-n_strips // (n_iters + 2))

    def norm_strip(idx):
        idx = jnp.minimum(idx, n_strips - 1)
        rs = pl.ds(pl.multiple_of(idx * LN_STRIP, LN_STRIP), LN_STRIP)
        o_ref[rs, :] = _layer_norm(y_scr[rs, :], lng, lnb)

    @pl.when(i == 0)
    def _():
        y_scr[...] = jnp.zeros_like(y_scr)

    @pl.when(i < n_tiles)
    def _():
        sh = mod_ref[0, row0:row0 + 1, :]
        sc1 = 1.0 + mod_ref[0, row0 + 1:row0 + 2, :]
        half_g = 0.5 * mod_ref[0, row0 + 2:row0 + 3, :]

        h_scr[...] = (x_ref[...] * sc1 + sh).astype(BF16)

        def chunk(j):
            cols = lambda start: pl.ds(pl.multiple_of(start, ck), ck)
            h = h_scr[...]
            a = _dot(h, win_ref[:, cols(j * ck)])
            b = _dot(h, win_ref[:, cols(n_chunks * ck + j * ck)])
            return _dot((_silu(a) * b).astype(BF16), wout_ref[j])

        acc_ref[...] = chunk(0)
        for s in range(per_iter):
            norm_strip(s)

        def body(k, carry):
            for u in range(unroll):
                acc_ref[...] += chunk(1 + k * unroll + u)
            for s in range(per_iter):
                norm_strip((k + 1) * per_iter + s)
            return carry

        lax.fori_loop(0, n_iters, body, 0)
        for s in range(per_iter):
            norm_strip((n_iters + 1) * per_iter + s)
        for j in tail[:-1]:
            acc_ref[...] += chunk(j)
        y_scr[...] = alpha * x_ref[...] + half_g * (acc_ref[...] + chunk(tail[-1]))

    @pl.when(i == n_tiles)
    def _():
        _layer_norm_rows(tm, lambda rs: y_scr[rs, :], o_ref, lng, lnb)


def _ffn(x, mod, win, wout, lng, lnb, *, row0, alpha, tm):
    n_tok, d = x.shape
    n_tiles = n_tok // tm
    per_req = n_tok // mod.shape[0]
    assert n_tok % tm == 0 and per_req % tm == 0, "a token tile must not straddle two requests"
    cur = lambda ti: jnp.minimum(ti, n_tiles - 1)
    kern = functools.partial(_ffn_kernel, row0=row0, alpha=alpha)
    return pl.pallas_call(
        kern,
        grid=(n_tiles + 1,),
        in_specs=[
            pl.BlockSpec((tm, d), lambda ti: (cur(ti), 0)),
            pl.BlockSpec((1, N_MOD, d), lambda ti: ((cur(ti) * tm) // per_req, 0, 0)),
            _const_spec(win.shape),
            _const_spec(wout.shape),
            _const_spec((1, d)),
            _const_spec((1, d)),
        ],
        out_specs=pl.BlockSpec((tm, d), lambda ti: (jnp.maximum(ti - 1, 0), 0)),
        out_shape=jax.ShapeDtypeStruct((n_tok, d), F32),
        scratch_shapes=[pltpu.VMEM((tm, d), BF16), pltpu.VMEM((tm, d), F32), pltpu.VMEM((tm, d), F32)],
        compiler_params=_cparams(("arbitrary",)),
        name="ffn",
    )(x, mod, win, wout, lng, lnb)


def _log_sigmoid(x):
    return jnp.minimum(x, 0.0) - jnp.log(1.0 + jnp.exp(-jnp.abs(x)))


def _rope(x, cos, sin_signed, first_half):
    rot = jnp.where(first_half, pltpu.roll(x, LANES - 16, 1), pltpu.roll(x, 16, 1))
    return x * cos + rot * sin_signed


def _mixin_kernel(*refs, rope, c_conv, w_gla, w_diff):
    if rope:
        (x_ref, mod_ref, w_ref, w2_ref, b2_ref, cos_ref, sin_ref,
         yglu_ref, gq_ref, gk_ref, gv_ref, gate_ref, gdec_ref, dq_ref, dk_ref, dvt_ref) = refs
    else:
        (x_ref, mod_ref, w_ref, w2_ref, b2_ref,
         yglu_ref, gq_ref, gk_ref, gv_ref, gate_ref, gdec_ref, dq_ref, dk_ref, dvt_ref, dv_ref) = refs
    x = x_ref[0]
    h, _ = _modulate(x, mod_ref, 3)
    h = h.astype(BF16)
    o_gla = 2 * c_conv
    o_diff = o_gla + 4 * w_gla
    o_lr = o_diff + 3 * w_diff

    tm = x.shape[0]
    halves = [slice(0, tm // 2), slice(tm // 2, tm)] if tm % (2 * LANES) == 0 else [slice(0, tm)]
    zs = []
    for rs in halves:
        hh = h[rs]
        zs.append((_dot(hh, w_ref[:, 0:o_gla]), _dot(hh, w_ref[:, o_gla:o_diff]),
                   _dot(hh, w_ref[:, o_lr:o_lr + LANES]).astype(BF16), _dot(hh, w_ref[:, o_diff:o_lr])))
    pres = [_dot(z[2], w2_ref[...]) + b2_ref[...] for z in zs]

    for rs, (zc, zg, _, zd), pre in zip(halves, zs, pres):
        yglu_ref[0, rs, :] = zc[:, :c_conv] * jax.nn.sigmoid(zc[:, c_conv:])
        gq_ref[0, rs, :] = zg[:, 0:w_gla] * (DK_GLA ** -0.5)
        gk_ref[0, rs, :] = zg[:, w_gla:2 * w_gla]
        gv_ref[0, rs, :] = zg[:, 2 * w_gla:3 * w_gla].astype(BF16)
        gate_ref[0, rs, :] = zg[:, 3 * w_gla:4 * w_gla]
        gdec_ref[0, rs, :] = _log_sigmoid(pre) * (1.0 / GLA_TAU)
        if rope:
            cos = cos_ref[rs, :]
            sin = sin_ref[rs, :]
            lane = lax.broadcasted_iota(jnp.int32, cos.shape, 1)
            first_half = (lane % 32) < 16
        for hd in range(H_DIFF):
            sl = slice(hd * LANES, (hd + 1) * LANES)
            q = zd[:, sl]
            k = zd[:, w_diff + hd * LANES:w_diff + (hd + 1) * LANES]
            v = zd[:, 2 * w_diff + hd * LANES:2 * w_diff + (hd + 1) * LANES]
            if rope:
                q = _rope(q, cos, sin, first_half)
                k = _rope(k, cos, sin, first_half)
            dq_ref[0, rs, sl] = (q * (DH_DIFF ** -0.5 * math.log2(math.e))).astype(BF16)
            dk_ref[0, hd, rs, :] = k.astype(dk_ref.dtype)
            dvt_ref[0, hd, 0, 0:LANES, rs] = v.T.astype(BF16)
            if not rope:
                dv_ref[0, hd, rs, :] = v
    for hd in range(H_DIFF):
        dvt_ref[0, hd, 0, LANES:, :] = jnp.ones((ONES_ROWS, tm), BF16)


def _mixin(x, mod, w, w2, b2, cos, sin, *, tm):
    b, t, d = x.shape
    rope = cos is not None
    w_gla = H_GLA * DK_GLA
    w_diff = H_DIFF * 2 * DH_DIFF
    c_conv = (w.shape[1] - LANES - 4 * w_gla - 3 * w_diff) // 2
    kv_dtype = BF16 if rope else F32
    mod_rows = mod.shape[0]
    mod_idx = (lambda bi, ti: (bi, 0, 0)) if mod_rows > 1 else (lambda bi, ti: (0, 0, 0))
    tok = lambda n: pl.BlockSpec((1, tm, n), lambda bi, ti: (bi, ti, 0))
    heads = pl.BlockSpec((1, H_DIFF, tm, LANES), lambda bi, ti: (bi, 0, ti, 0))
    in_specs = [tok(d), pl.BlockSpec((1, N_MOD, d), mod_idx),
                _const_spec(w.shape), _const_spec(w2.shape), _const_spec(b2.shape)]
    args = [x, mod, w, w2, b2]
    if rope:
        in_specs += [pl.BlockSpec((tm, LANES), lambda bi, ti: (ti, 0))] * 2
        args += [cos, sin]
    sds = jax.ShapeDtypeStruct
    out_shape = [
        sds((b, t, c_conv), F32),
        sds((b, t, w_gla), F32),
        sds((b, t, w_gla), F32),
        sds((b, t, w_gla), BF16),
        sds((b, t, w_gla), F32),
        sds((b, t, 2 * w_gla), F32),
        sds((b, t, w_diff), BF16),
        sds((b, H_DIFF, t, LANES), kv_dtype),
        sds((b, H_DIFF, t // tm, LANES + ONES_ROWS, tm), BF16),
    ]
    out_specs = [tok(c_conv), tok(w_gla), tok(w_gla), tok(w_gla), tok(w_gla), tok(2 * w_gla),
                 tok(w_diff), heads,
                 pl.BlockSpec((1, H_DIFF, 1, LANES + ONES_ROWS, tm), lambda bi, ti: (bi, 0, ti, 0, 0))]
    if not rope:
        out_shape.append(sds((b, H_DIFF, t, LANES), F32))
        out_specs.append(heads)
    kern = functools.partial(_mixin_kernel, rope=rope, c_conv=c_conv, w_gla=w_gla, w_diff=w_diff)
    return pl.pallas_call(
        kern,
        grid=(b, t // tm),
        in_specs=in_specs,
        out_specs=out_specs,
        out_shape=out_shape,
        compiler_params=_cparams(("parallel", "parallel")),
        name="mixer_in",
    )(*args)


def _conv_kernel(prev_ref, cur_ref, next_ref, w_ref, b_ref, g_ref, beta_ref, o_ref, win_ref, *, tc, nt):
    i = pl.program_id(1)
    pad = CONV_WIDTH // 2
    win_ref[0:CONV_HALO] = jnp.where(i > 0, prev_ref[0], 0.0)
    win_ref[CONV_HALO:CONV_HALO + tc] = cur_ref[0]
    win_ref[CONV_HALO + tc:2 * CONV_HALO + tc] = jnp.where(i < nt - 1, next_ref[0], 0.0)
    w = w_ref[...]
    first = CONV_HALO - pad
    n_groups = -(-(first + CONV_WIDTH) // SUBLANES)
    for r in range(0, tc, CONV_ROWS):
        wide = win_ref[r:r + CONV_ROWS + n_groups * SUBLANES, :]
        acc = None
        for s in range(SUBLANES):
            z = None
            for a in range(n_groups):
                k = a * SUBLANES + s - first
                if 0 <= k < CONV_WIDTH:
                    term = wide[a * SUBLANES:a * SUBLANES + CONV_ROWS + SUBLANES, :] * w[k:k + 1, :]
                    z = term if z is None else z + term
            if z is not None:
                acc = z[s:s + CONV_ROWS, :] if acc is None else acc + z[s:s + CONV_ROWS, :]
        y = _layer_norm(acc + b_ref[...], g_ref[...], beta_ref[...])
        o_ref[0, r:r + CONV_ROWS, :] = _silu(y).astype(o_ref.dtype)


def _conv(yglu, w, bias, g, beta, *, tc):
    b, t, c = yglu.shape
    nt = t // tc
    hb = tc // CONV_HALO
    n_halo = t // CONV_HALO
    kern = functools.partial(_conv_kernel, tc=tc, nt=nt)
    return pl.pallas_call(
        kern,
        grid=(b, nt),
        in_specs=[
            pl.BlockSpec((1, CONV_HALO, c), lambda bi, i: (bi, jnp.maximum(i * hb - 1, 0), 0)),
            pl.BlockSpec((1, tc, c), lambda bi, i: (bi, i, 0)),
            pl.BlockSpec((1, CONV_HALO, c), lambda bi, i: (bi, jnp.minimum((i + 1) * hb, n_halo - 1), 0)),
            _const_spec(w.shape), _const_spec((1, c)), _const_spec((1, c)), _const_spec((1, c)),
        ],
        out_specs=pl.BlockSpec((1, tc, c), lambda bi, i: (bi, i, 0)),
        out_shape=jax.ShapeDtypeStruct((b, t, c), BF16),
        scratch_shapes=[pltpu.VMEM((tc + 2 * CONV_HALO, c), F32)],
        compiler_params=_cparams(("parallel", "parallel")),
        name="conv_module",
    )(yglu, yglu, yglu, w, bias, g, beta)


def _split3(x):
    hi = x.astype(BF16)
    r1 = x - hi.astype(F32)
    mid = r1.astype(BF16)
    lo = (r1 - mid.astype(F32)).astype(BF16)
    return hi, mid, lo


def _gla_kernel(*refs, nc, has_state, want_state):
    it = iter(refs)
    ins = [[next(it) for _ in range(4)] for _ in range(2)]
    s0_ref = next(it) if has_state else None
    o_refs = [next(it), next(it)]
    sfin_ref = next(it) if want_state else None
    s_scr, qt_scr, kv_scr, dec_scr, sin_scr = (next(it) for _ in range(5))

    i = pl.program_id(1)
    nblk = pl.num_programs(1)
    c = GLA_CHUNK
    width = s_scr.shape[1]
    nh = width // c

    @pl.when(i == 0)
    def _():
        if has_state:
            s_scr[...] = s0_ref[0]
        else:
            s_scr[...] = jnp.zeros_like(s_scr)

    row = lax.broadcasted_iota(jnp.int32, (c, c), 0)
    col = lax.broadcasted_iota(jnp.int32, (c, c), 1)
    rowa = lax.broadcasted_iota(jnp.int32, (c, width), 0)
    cola = lax.broadcasted_iota(jnp.int32, (c, width), 1) % c
    tri_b = [jnp.where(row >= col, 1.0, 0.0).astype(BF16), jnp.where(row <= col, 1.0, 0.0).astype(BF16)]
    tri_cat = [rowa >= cola, rowa <= cola]
    rb = lax.broadcasted_iota(jnp.int32, (width, width), 0) // c
    cb = lax.broadcasted_iota(jnp.int32, (width, width), 1) // c
    bd = rb == cb
    rows = [slice(ci * c, (ci + 1) * c) for ci in range(nc)]
    work = [(d, ci) for ci in range(nc) for d in range(2)]

    gcums = {}
    for d, ci in work:
        ghi, gmid, glo = _split3(ins[d][3][0, rows[ci], :])
        gcums[d, ci] = _dot(tri_b[d], ghi) + _dot(tri_b[d], gmid) + _dot(tri_b[d], glo)
    scores = {}
    for d, ci in work:
        q_ref, k_ref, v_ref, _ = ins[d]
        rs = rows[ci]
        gcum = gcums[d, ci]
        gtot = gcum[c - 1:c, :] if d == 0 else gcum[0:1, :]
        k = k_ref[0, rs, :]
        q_t = (q_ref[0, rs, :] * jnp.exp(gcum)).astype(BF16)
        k_t = k * jnp.exp(-gcum)
        k_hat = (k * jnp.exp(gtot - gcum)).astype(BF16)
        k_bd = jnp.where(bd, jnp.concatenate([k_t] * nh, axis=0), 0.0).astype(BF16)
        scores[d, ci] = _dot_nt(q_t, k_bd)
        qt_scr[d, rs, :] = q_t
        kv_scr[d, ci] = jnp.where(bd, _dot_tn(v_ref[0, rs, :], k_hat), 0.0)
        dec_scr[d, ci] = jnp.broadcast_to(jnp.exp(gtot), dec_scr.shape[2:])
    for d, ci in work:
        v = ins[d][2][0, rows[ci], :]
        a = jnp.where(tri_cat[d], scores[d, ci], 0.0).astype(BF16)
        v_bd = jnp.where(bd, jnp.concatenate([v] * nh, axis=0), jnp.zeros((), v.dtype))
        o_refs[d][0, rows[ci], :] = _dot(a, v_bd)

    scan = [(d, j, j if d == 0 else nc - 1 - j) for j in range(nc) for d in range(2)]
    for d, j, ci in scan:
        s_in = s_scr[d]
        sin_scr[d, j] = s_in.astype(BF16)
        s_scr[d] = s_in * dec_scr[d, ci, 0:1, :] + kv_scr[d, ci]
    for d, j, ci in scan:
        o_refs[d][0, rows[ci], :] += _dot_nt(qt_scr[d, rows[ci], :], sin_scr[d, j])

    if want_state:
        @pl.when(i == nblk - 1)
        def _():
            sfin_ref[0] = s_scr[...]


def _gla(q, k, v, gdec, s0, *, tb, want_state):
    b, t, w = q.shape
    nblk = t // tb
    nc = tb // GLA_CHUNK
    blk = [lambda i: i, lambda i: nblk - 1 - i]
    in_specs, args = [], []
    for d in range(2):
        tok = pl.BlockSpec((1, tb, w), lambda bi, i, d=d: (bi, blk[d](i), 0))
        in_specs += [tok, tok, tok, pl.BlockSpec((1, tb, w), lambda bi, i, d=d: (bi, blk[d](i), d))]
        args += [q, k, v, gdec]
    has_state = s0 is not None
    if has_state:
        in_specs.append(pl.BlockSpec((1, 2, w, w), lambda bi, i: (bi, 0, 0, 0)))
        args.append(s0)
    out_shape = [jax.ShapeDtypeStruct((b, t, w), F32)] * 2
    out_specs = [pl.BlockSpec((1, tb, w), lambda bi, i, d=d: (bi, blk[d](i), 0)) for d in range(2)]
    if want_state:
        out_shape.append(jax.ShapeDtypeStruct((b, 2, w, w), F32))
        out_specs.append(pl.BlockSpec((1, 2, w, w), lambda bi, i: (bi, 0, 0, 0)))
    kern = functools.partial(_gla_kernel, nc=nc, has_state=has_state, want_state=want_state)
    return pl.pallas_call(
        kern,
        grid=(b, nblk),
        in_specs=in_specs,
        out_specs=out_specs,
        out_shape=out_shape,
        scratch_shapes=[pltpu.VMEM((2, w, w), F32), pltpu.VMEM((2, tb, w), BF16),
                        pltpu.VMEM((2, nc, w, w), F32), pltpu.VMEM((2, nc, SUBLANES, w), F32),
                        pltpu.VMEM((2, nc, w, w), BF16)],
        compiler_params=_cparams(("parallel", "arbitrary")),
        name="gla",
    )(*args)


def _attn_kernel(*refs, has_cache, lam_init, n_slots):
    it = iter(refs)
    q_ref, k_ref, vt_ref = next(it), next(it), next(it)
    kc_ref, vtc_ref = (next(it), next(it)) if has_cache else (None, None)
    lam_ref, g_ref, o_ref = next(it), next(it), next(it)
    s_scr = [next(it) for _ in range(n_slots)]
    e_scr = [next(it) for _ in range(n_slots)]
    acc_scr = next(it)
    p = lam_ref[...]
    lam = (jnp.exp(jnp.sum(p[0:1] * p[1:2], axis=-1, keepdims=True))
           - jnp.exp(jnp.sum(p[2:3] * p[3:4], axis=-1, keepdims=True)) + lam_init)
    for hh in range(k_ref.shape[1]):
        lanes = pl.ds(hh * LANES, LANES)
        _attn_head(q_ref.at[0, :, lanes], k_ref.at[0, hh], vt_ref.at[0, hh],
                   kc_ref.at[0, hh] if has_cache else None, vtc_ref.at[0, hh] if has_cache else None,
                   lam, g_ref, o_ref.at[0, :, lanes], s_scr, e_scr, acc_scr, lam_init)


def _attn_head(q_ref, k_ref, vt_ref, kc_ref, vtc_ref, lam, g_ref, o_ref, s_scr, e_scr, acc_scr, lam_init):
    has_cache = kc_ref is not None
    n_slots = len(s_scr)
    q = q_ref[...]
    tq = q.shape[0]
    lane = lax.broadcasted_iota(jnp.int32, q.shape, 1)
    zero = jnp.zeros((), q.dtype)
    qs = jnp.concatenate([jnp.where(lane < DH_DIFF, q, zero), jnp.where(lane >= DH_DIFF, q, zero)], axis=0)

    tk = vt_ref.shape[2]
    nq = 2 * tq
    acc_scr[...] = jnp.zeros_like(acc_scr)
    group = s_scr[0].shape[0] // tk
    tiles = []
    for kr, vr in [(k_ref, vt_ref)] + ([(kc_ref, vtc_ref)] if has_cache else []):
        for first in range(0, vr.shape[0], group):
            tiles.append((kr, vr, first, min(group, vr.shape[0] - first)))
    n = len(tiles)

    def scores(j, slot):
        kr, _, first, cnt = tiles[j]
        rows = cnt * tk
        s_scr[slot][0:rows, :] = _dot_nt(kr[first * tk:first * tk + rows, :].astype(BF16), qs)
        strips = [s_scr[slot][r:r + ATTN_STRIP, :] for r in range(0, rows, ATTN_STRIP)]
        return jnp.max(functools.reduce(jnp.maximum, strips), axis=0, keepdims=True)

    def softmax(j, slot, mt, m):
        m_new = jnp.maximum(m, mt)
        alpha = jnp.exp2(m - m_new)
        for r in range(0, tiles[j][3] * tk, ATTN_STRIP):
            e_scr[slot][r:r + ATTN_STRIP, :] = jnp.exp2(s_scr[slot][r:r + ATTN_STRIP, :] - m_new).astype(BF16)
        return m_new, alpha

    def weighted(j, slot, alpha):
        _, vr, first, cnt = tiles[j]
        pv = functools.reduce(lambda a, b: a + b, [_dot(vr[first + p], e_scr[slot][p * tk:(p + 1) * tk, :])
                                                   for p in range(cnt)])
        acc_scr[...] = acc_scr[...] * alpha + pv

    slot = lambda j: j % n_slots

    lag = min(ATTN_LAG, n_slots - 1)
    m = jnp.full((1, nq), -1e30, F32)
    mt = scores(0, slot(0))
    alphas = {}
    for j in range(n + lag):
        if j < n:
            mt_next = scores(j + 1, slot(j + 1)) if j + 1 < n else None
            m, alphas[j] = softmax(j, slot(j), mt, m)
            mt = mt_next
        if 0 <= j - lag < n:
            weighted(j - lag, slot(j - lag), alphas.pop(j - lag))
    dv = o_ref.shape[1]
    on = acc_scr[0:dv, :] / acc_scr[dv:dv + 1, :]
    o = on[:, :tq] - lam * on[:, tq:]
    ms = jnp.mean(o * o, axis=0, keepdims=True)
    y = o * lax.rsqrt(ms + LN_EPS) * g_ref[...] * (1.0 - lam_init)
    o_ref[...] = y.T.astype(o_ref.dtype)


def _attn(dq, dk, dvt, ck, cvt, lam_p, g_col, *, lam_init, tq):
    b, t, _ = dq.shape
    dv, tk = dvt.shape[3:]
    has_cache = ck is not None
    group = max(1, min(ATTN_KEY_TILE // tk, dvt.shape[2]))
    n_tiles = -(-dvt.shape[2] // group) + (-(-cvt.shape[2] // group) if has_cache else 0)
    n_slots = min(n_tiles, ATTN_SLOTS)
    rows = group * tk
    hps = H_DIFF if n_tiles == 1 else 1
    qspec = pl.BlockSpec((1, tq, hps * LANES), lambda bi, h, qi: (bi, qi, h))
    kspec = lambda a: pl.BlockSpec((1, hps) + a.shape[2:], lambda bi, h, qi: (bi, h, 0, 0))
    vspec = lambda a: pl.BlockSpec((1, hps) + a.shape[2:], lambda bi, h, qi: (bi, h, 0, 0, 0))
    kv_args = [dk, dvt] + ([ck, cvt] if has_cache else [])
    kv_specs = [kspec(dk), vspec(dvt)] + ([kspec(ck), vspec(cvt)] if has_cache else [])
    kern = functools.partial(_attn_kernel, has_cache=has_cache, lam_init=lam_init, n_slots=n_slots)
    return pl.pallas_call(
        kern,
        grid=(b, H_DIFF // hps, t // tq),
        in_specs=[qspec] + kv_specs + [_const_spec(lam_p.shape), _const_spec(g_col.shape)],
        out_specs=qspec,
        out_shape=jax.ShapeDtypeStruct(dq.shape, BF16),
        scratch_shapes=([pltpu.VMEM((rows, 2 * tq), F32)] * n_slots + [pltpu.VMEM((rows, 2 * tq), BF16)] * n_slots
                        + [pltpu.VMEM((dv, 2 * tq), F32)]),
        compiler_params=_cparams(("parallel", "parallel", "parallel")),
        name="diff_attn",
    )(dq, *kv_args, lam_p, g_col)


def _mixout_kernel(x_ref, mod_ref, yc_ref, of_ref, ob_ref, gate_ref, yd_ref, gn_ref, w_ref,
                   lng_ref, lnb_ref, o_ref, *, alpha):
    g1 = mod_ref[0, 5:6, :]
    tm = x_ref.shape[1]
    width = of_ref.shape[2]
    c_conv = yc_ref.shape[2]
    rb = lax.broadcasted_iota(jnp.int32, (width, width), 0) // DK_GLA
    cb = lax.broadcasted_iota(jnp.int32, (width, width), 1) // DK_GLA
    ones_bd = jnp.where(rb == cb, 1.0, 0.0).astype(BF16)
    n_parts = MIXOUT_PARTS if tm % (MIXOUT_PARTS * LANES) == 0 else 1
    parts = [slice(i * tm // n_parts, (i + 1) * tm // n_parts) for i in range(n_parts)]
    sums = []
    for rs in parts:
        o = of_ref[0, rs, :] + ob_ref[0, rs, :]
        sq = o * o
        hi = sq.astype(BF16)
        lo = (sq - hi.astype(F32)).astype(BF16)
        sums.append((o, _dot(hi, ones_bd) + _dot(lo, ones_bd)))
    ys = []
    for rs, (o, ss) in zip(parts, sums):
        yg = (o * lax.rsqrt(ss * (1.0 / DK_GLA) + LN_EPS) * gn_ref[...] * _silu(gate_ref[0, rs, :])).astype(BF16)
        ys.append(_dot(yc_ref[0, rs, :], w_ref[0:c_conv, :])
                  + _dot(yg, w_ref[c_conv:c_conv + width, :])
                  + _dot(yd_ref[0, rs, :], w_ref[c_conv + width:, :]))
    for rs, y in zip(parts, ys):
        o_ref[0, rs, :] = _layer_norm(alpha * x_ref[0, rs, :] + g1 * y, lng_ref[...], lnb_ref[...])


def _mixout(x, mod, yconv, o_fwd, o_bwd, gate, ydiff, gn, w, lng, lnb, *, alpha, tm):
    b, t, d = x.shape
    mod_rows = mod.shape[0]
    mod_idx = (lambda bi, ti: (bi, 0, 0)) if mod_rows > 1 else (lambda bi, ti: (0, 0, 0))
    tok = lambda n: pl.BlockSpec((1, tm, n), lambda bi, ti: (bi, ti, 0))
    wg = o_fwd.shape[2]
    kern = functools.partial(_mixout_kernel, alpha=alpha)
    return pl.pallas_call(
        kern,
        grid=(b, t // tm),
        in_specs=[
            tok(d), pl.BlockSpec((1, N_MOD, d), mod_idx), tok(yconv.shape[2]),
            tok(wg), tok(wg), tok(wg), tok(ydiff.shape[2]),
            _const_spec(gn.shape), _const_spec(w.shape), _const_spec((1, d)), _const_spec((1, d)),
        ],
        out_specs=tok(d),
        out_shape=jax.ShapeDtypeStruct((b, t, d), F32),
        compiler_params=_cparams(("parallel", "parallel")),
        name="mixer_out",
    )(x, mod, yconv, o_fwd, o_bwd, gate, ydiff, gn, w, lng, lnb)


def _rope_tables(t):
    rows = t // GRID_W
    row = jnp.repeat(jnp.arange(rows, dtype=F32), GRID_W)
    col = jnp.tile(jnp.arange(GRID_W, dtype=F32), rows)
    seg = DH_DIFF // 2
    inv = ROPE_BASE ** (-jnp.arange(0, seg, 2, dtype=F32) / seg)
    a_r = row[:, None] * inv
    a_c = col[:, None] * inv
    ang = jnp.concatenate([a_r, a_r, a_c, a_c], axis=-1)
    ang = jnp.concatenate([ang, ang], axis=-1)
    sign = jnp.where((jnp.arange(LANES) % 32) < 16, -1.0, 1.0).astype(F32)
    return jnp.cos(ang), jnp.sin(ang) * sign


def _ffn_weights(w_in, w_out, ck):
    d, two_ff = w_in.shape
    ff = two_ff // 2
    n = ff // ck
    wout = w_out.reshape(n, ck, d).astype(BF16)
    return w_in.astype(BF16), wout


def _pick_tile(t, pref):
    return pref if t % pref == 0 else t


def kernel(x_prompt, x_sample, cache_diff_k, cache_diff_v, state_gla, c, c_ctx, w_ada, b_ada, w_ffn1_in,
           w_ffn1_out, w_ffn2_in, w_ffn2_out, w_in, conv_w, conv_b, conv_ln_g, conv_ln_b, gla_w_a2, gla_b_a,
           gla_norm_g, diff_lam, diff_norm_g, w_out, ln_g, ln_b):
    depth, d, _ = w_ada.shape
    alpha = (2 * depth) ** 0.25
    n_dec = c.shape[0]
    c_conv = conv_w.shape[2]
    w_gla = H_GLA * DK_GLA
    w_diff = H_DIFF * 2 * DH_DIFF
    in_conv, in_gla = 2 * c_conv, 4 * w_gla + 2 * GLA_RANK
    ff_chunk = 256

    rows = -(-(n_dec + 1) // SUBLANES) * SUBLANES
    cvec = jnp.zeros((rows, d), F32).at[:n_dec].set(c).at[n_dec].set(c_ctx)
    mod = _ada(cvec, w_ada, b_ada).reshape(depth, rows, N_MOD, d)

    cos, sin = _rope_tables(x_sample.shape[1])

    layers = []
    for l in range(depth):
        wi = w_in[l]
        w_lr = jnp.pad(wi[:, in_conv + 4 * w_gla:in_conv + in_gla], ((0, 0), (0, LANES - 2 * GLA_RANK)))
        w_all = jnp.concatenate([wi[:, :in_conv + 4 * w_gla], wi[:, in_conv + in_gla:], w_lr], axis=1).astype(BF16)
        w2 = jnp.zeros((LANES, 2 * w_gla), F32)
        w2 = w2.at[:GLA_RANK, :w_gla].set(gla_w_a2[l, 0]).at[GLA_RANK:2 * GLA_RANK, w_gla:].set(gla_w_a2[l, 1])
        layers.append(dict(
            ffn1=_ffn_weights(w_ffn1_in[l], w_ffn1_out[l], ff_chunk),
            ffn2=_ffn_weights(w_ffn2_in[l], w_ffn2_out[l], ff_chunk),
            w_all=w_all, w2=w2.astype(BF16), b2=gla_b_a[l].reshape(1, 2 * w_gla),
            conv_w=conv_w[l], conv_b=conv_b[l][None], conv_g=conv_ln_g[l][None], conv_beta=conv_ln_b[l][None],
            gn=jnp.tile(gla_norm_g[l], H_GLA)[None], lam=diff_lam[l], dg=diff_norm_g[l][:, None],
            w_out=w_out[l].astype(BF16),
            lng=[ln_g[l, i][None] for i in range(3)], lnb=[ln_b[l, i][None] for i in range(3)],
            lam_init=0.8 - 0.6 * math.exp(-0.3 * l),
        ))

    def run_layer(x, mod_l, P, ctx):
        t = x.shape[1]
        tm = _pick_tile(t, 512)
        bsz, _, d_model = x.shape
        shared_mod = mod_l.shape[0] == 1
        tf = _pick_tile(bsz * t if shared_mod else t, 1024)

        def ffn(x, which, row0, ln_idx):
            y = _ffn(x.reshape(bsz * t, d_model), mod_l, *P[which], P['lng'][ln_idx], P['lnb'][ln_idx],
                     row0=row0, alpha=alpha, tm=tf)
            return y.reshape(bsz, t, d_model)

        x = ffn(x, 'ffn1', 0, 0)
        rope = ctx is not None
        yglu, gq, gk, gv, gate, gdec, dq, dk, dvt, *dv = _mixin(
            x, mod_l, P['w_all'], P['w2'], P['b2'], cos if rope else None, sin if rope else None, tm=tm)
        yconv = _conv(yglu, P['conv_w'], P['conv_b'], P['conv_g'], P['conv_beta'], tc=_pick_tile(t, 256))
        gla_out = _gla(gq, gk, gv, gdec, ctx['s0'] if rope else None, tb=_pick_tile(t, 512), want_state=not rope)
        ydiff = _attn(dq, dk, dvt, ctx['k'] if rope else None, ctx['vt'] if rope else None, P['lam'], P['dg'],
                      lam_init=P['lam_init'], tq=_pick_tile(t, 512))
        x = _mixout(x, mod_l, yconv, gla_out[0], gla_out[1], gate, ydiff, P['gn'], P['w_out'], P['lng'][1], P['lnb'][1],
                    alpha=alpha, tm=tm)
        x = ffn(x, 'ffn2', 6, 2)
        return x, (dk, dv[0] if dv else None, gla_out[2] if not rope else None)

    xp = x_prompt
    new_k, new_v, new_s = [], [], []
    for l in range(depth):
        xp, (k_l, v_l, s_l) = run_layer(xp, mod[l, n_dec:n_dec + 1], layers[l], None)
        new_k.append(k_l)
        new_v.append(v_l)
        bsz = s_l.shape[0]
        s6 = s_l.reshape(bsz, 2, H_GLA, DK_GLA, H_GLA, DK_GLA)
        s_heads = jnp.stack([s6[:, :, h, :, h, :] for h in range(H_GLA)], axis=2)
        new_s.append(jnp.swapaxes(s_heads, -1, -2))

    xs = x_sample
    for l in range(depth):
        st = jnp.swapaxes(state_gla[:, l], -1, -2)
        s0 = jnp.zeros(st.shape[:2] + (H_GLA, DK_GLA, H_GLA, DK_GLA), F32)
        for h in range(H_GLA):
            s0 = s0.at[:, :, h, :, h, :].set(st[:, :, h])
        s0 = s0.reshape(st.shape[0], 2, w_gla, w_gla)
        past = cache_diff_v.shape[3]
        tile = _pick_tile(x_sample.shape[1], 512)
        assert past % tile == 0, "cached context length must be a whole number of key tiles"
        cvt = cache_diff_v[:, l].astype(BF16).reshape(n_dec, H_DIFF, past // tile, tile, LANES)
        cvt = jnp.concatenate([jnp.swapaxes(cvt, -1, -2), jnp.ones(cvt.shape[:3] + (ONES_ROWS, tile), BF16)], axis=3)
        ctx = dict(k=cache_diff_k[:, l].astype(BF16), vt=cvt, s0=s0)
        xs, _ = run_layer(xs, mod[l, :n_dec], layers[l], ctx)

    return (xp, xs, jnp.stack(new_k, axis=1), jnp.stack(new_v, axis=1), jnp.stack(new_s, axis=1))
```

```python
import functools
import math

import jax
import jax.numpy as jnp
from jax import lax
from jax.experimental import pallas as pl
from jax.experimental.pallas import tpu as pltpu

F32 = jnp.float32
BF16 = jnp.bfloat16

GRID_W = 64
CONV_WIDTH = 31
H_GLA = 4
DK_GLA = 64
GLA_RANK = 16
GLA_TAU = 16.0
GLA_CHUNK = 64
H_DIFF = 4
DH_DIFF = 64
ROPE_BASE = 10000.0
LN_EPS = 1e-5
N_MOD = 9

LANES = 128
SUBLANES = 8
V7X_VMEM_BYTES = 64 * 1024 * 1024
VMEM_LIMIT = V7X_VMEM_BYTES - 8 * 1024 * 1024

CONV_HALO = 16
CONV_ROWS = 32
ATTN_STRIP = 32
MIXIN_PARTS = 2
MIXOUT_PARTS = 2
ATTN_KEY_TILE = 512
ATTN_AHEAD = 1
ATTN_LAG = 1
ATTN_SLOTS = 3
ONES_ROWS = 16
ROW_STRIP = 32
LN_STRIP = 64


def _cparams(sem):
    return pltpu.CompilerParams(dimension_semantics=sem, vmem_limit_bytes=VMEM_LIMIT)


def _const_spec(shape):
    nd = len(shape)
    return pl.BlockSpec(shape, lambda *_: (0,) * nd, pipeline_mode=pl.Buffered(1))


def _dot(a, b):
    return jnp.dot(a, b, preferred_element_type=F32)


def _dot_nt(a, b):
    return lax.dot_general(a, b, (((1,), (1,)), ((), ())), preferred_element_type=F32)


def _dot_tn(a, b):
    return lax.dot_general(a, b, (((0,), (0,)), ((), ())), preferred_element_type=F32)


def _silu(x):
    return x * jax.nn.sigmoid(x)


def _layer_norm(y, g, b):
    mu = jnp.mean(y, axis=-1, keepdims=True)
    yc = y - mu
    var = jnp.mean(yc * yc, axis=-1, keepdims=True)
    return yc * lax.rsqrt(var + LN_EPS) * g + b


def _modulate(x, mod_ref, row0):
    sh = mod_ref[0, row0:row0 + 1, :]
    sc = mod_ref[0, row0 + 1:row0 + 2, :]
    g = mod_ref[0, row0 + 2:row0 + 3, :]
    return x * (1.0 + sc) + sh, g


def _ada_kernel(c_ref, w_ref, b_ref, o_ref):
    s = _silu(c_ref[...]).astype(BF16)
    o_ref[0] = _dot(s, w_ref[0].astype(BF16)) + b_ref[0]


def _ada(cvec, w_ada, b_ada):
    depth, d, n = w_ada.shape
    rows = cvec.shape[0]
    tn = d
    return pl.pallas_call(
        _ada_kernel,
        grid=(depth, n // tn),
        in_specs=[
            pl.BlockSpec((rows, d), lambda l, j: (0, 0)),
            pl.BlockSpec((1, d, tn), lambda l, j: (l, 0, j)),
            pl.BlockSpec((1, 1, tn), lambda l, j: (l, 0, j)),
        ],
        out_specs=pl.BlockSpec((1, rows, tn), lambda l, j: (l, 0, j)),
        out_shape=jax.ShapeDtypeStruct((depth, rows, n), F32),
        compiler_params=_cparams(("arbitrary", "arbitrary")),
        name="ada_mod",
    )(cvec, w_ada, b_ada.reshape(depth, 1, n))


def _strips(n_rows, fn):
    def body(i, carry):
        fn(pl.ds(pl.multiple_of(i * ROW_STRIP, ROW_STRIP), ROW_STRIP))
        return carry

    lax.fori_loop(0, n_rows // ROW_STRIP, body, 0, unroll=2)


def _layer_norm_rows(n_rows, make_y, o_ref, g, b):
    n = n_rows // LN_STRIP
    sl = lambda i: pl.ds(i * LN_STRIP if isinstance(i, int) else pl.multiple_of(i * LN_STRIP, LN_STRIP), LN_STRIP)

    def means(i):
        y = make_y(sl(i))
        o_ref[sl(i), :] = y
        return jnp.mean(y, axis=-1, keepdims=True)

    def rstd(i, mu):
        yc = o_ref[sl(i), :] - mu
        return lax.rsqrt(jnp.mean(yc * yc, axis=-1, keepdims=True) + LN_EPS)

    def normalise(i, mu, r):
        o_ref[sl(i), :] = (o_ref[sl(i), :] - mu) * r * g + b

    if n == 1:
        mu = means(0)
        normalise(0, mu, rstd(0, mu))
        return
    mu_a = means(0)
    mu_b = means(1)
    r_a = rstd(0, mu_a)

    def body(i, carry):
        mu_a, r_a, mu_b = carry
        r_b = rstd(i - 1, mu_b)
        mu_c = means(i)
        normalise(i - 2, mu_a, r_a)
        return mu_b, r_b, mu_c

    mu_a, r_a, mu_b = lax.fori_loop(2, n, body, (mu_a, r_a, mu_b))
    r_b = rstd(n - 1, mu_b)
    normalise(n - 2, mu_a, r_a)
    normalise(n - 1, mu_b, r_b)


def _ffn_kernel(x_ref, mod_ref, win_ref, wout_ref, lng_ref, lnb_ref, o_ref, h_scr, acc_ref, y_scr, *, row0, alpha):
    i = pl.program_id(0)
    n_tiles = pl.num_programs(0) - 1
    tm = x_ref.shape[0]
    n_chunks, ck, _ = wout_ref.shape
    lng = lng_ref[...]
    lnb = lnb_ref[...]
    n_strips = tm // LN_STRIP
    unroll = 2
    n_iters = (n_chunks - 2) // unroll
    tail = range(1 + n_iters * unroll, n_chunks)
    per_iter = -(-n_strips // (n_iters + 2))

    def norm_strip(idx):
        idx = jnp.minimum(idx, n_strips - 1)
        rs = pl.ds(pl.multiple_of(idx * LN_STRIP, LN_STRIP), LN_STRIP)
        o_ref[rs, :] = _layer_norm(y_scr[rs, :], lng, lnb)

    @pl.when(i == 0)
    def _():
        y_scr[...] = jnp.zeros_like(y_scr)

    @pl.when(i < n_tiles)
    def _():
        sh = mod_ref[0, row0:row0 + 1, :]
        sc1 = 1.0 + mod_ref[0, row0 + 1:row0 + 2, :]
        half_g = 0.5 * mod_ref[0, row0 + 2:row0 + 3, :]

        h_scr[...] = (x_ref[...] * sc1 + sh).astype(BF16)

        def chunk(j):
            cols = lambda start: pl.ds(pl.multiple_of(start, ck), ck)
            h = h_scr[...]
            a = _dot(h, win_ref[:, cols(j * ck)])
            b = _dot(h, win_ref[:, cols(n_chunks * ck + j * ck)])
            return _dot((_silu(a) * b).astype(BF16), wout_ref[j])

        acc_ref[...] = chunk(0)
        for s in range(per_iter):
            norm_strip(s)

        def body(k, carry):
            for u in range(unroll):
                acc_ref[...] += chunk(1 + k * unroll + u)
            for s in range(per_iter):
                norm_strip((k + 1) * per_iter + s)
            return carry

        lax.fori_loop(0, n_iters, body, 0)
        for s in range(per_iter):
            norm_strip((n_iters + 1) * per_iter + s)
        for j in tail[:-1]:
            acc_ref[...] += chunk(j)
        y_scr[...] = alpha * x_ref[...] + half_g * (acc_ref[...] + chunk(tail[-1]))

    @pl.when(i == n_tiles)
    def _():
        _layer_norm_rows(tm, lambda rs: y_scr[rs, :], o_ref, lng, lnb)


def _layer_spec(stacked, layer):
    nd = stacked.ndim - 1
    return pl.BlockSpec((None,) + stacked.shape[1:], lambda *_: (layer,) + (0,) * nd, pipeline_mode=pl.Buffered(1))


def _ffn(x, mod, win, wout, layer, lng, lnb, *, row0, alpha, tm):
    n_tok, d = x.shape
    n_tiles = n_tok // tm
    per_req = n_tok // mod.shape[0]
    assert n_tok % tm == 0 and per_req % tm == 0, "a token tile must not straddle two requests"
    cur = lambda ti: jnp.minimum(ti, n_tiles - 1)
    kern = functools.partial(_ffn_kernel, row0=row0, alpha=alpha)
    return pl.pallas_call(
        kern,
        grid=(n_tiles + 1,),
        in_specs=[
            pl.BlockSpec((tm, d), lambda ti: (cur(ti), 0)),
            pl.BlockSpec((1, N_MOD, d), lambda ti: ((cur(ti) * tm) // per_req, 0, 0)),
            _layer_spec(win, layer),
            _layer_spec(wout, layer),
            _const_spec((1, d)),
            _const_spec((1, d)),
        ],
        out_specs=pl.BlockSpec((tm, d), lambda ti: (jnp.maximum(ti - 1, 0), 0)),
        out_shape=jax.ShapeDtypeStruct((n_tok, d), F32),
        scratch_shapes=[pltpu.VMEM((tm, d), BF16), pltpu.VMEM((tm, d), F32), pltpu.VMEM((tm, d), F32)],
        compiler_params=_cparams(("arbitrary",)),
        name="ffn",
    )(x, mod, win, wout, lng, lnb)


def _log_sigmoid(x):
    return jnp.minimum(x, 0.0) - jnp.log(1.0 + jnp.exp(-jnp.abs(x)))


def _rope(x, cos, sin_signed, first_half):
    rot = jnp.where(first_half, pltpu.roll(x, LANES - 16, 1), pltpu.roll(x, 16, 1))
    return x * cos + rot * sin_signed


def _mixin_kernel(*refs, rope, c_conv, w_gla, w_diff):
    if rope:
        (x_ref, mod_ref, w_ref, w2_ref, b2_ref, cos_ref, sin_ref,
         yglu_ref, gq_ref, gk_ref, gv_ref, gate_ref, gdec_ref, dq_ref, dk_ref, dvt_ref) = refs
    else:
        (x_ref, mod_ref, w_ref, w2_ref, b2_ref,
         yglu_ref, gq_ref, gk_ref, gv_ref, gate_ref, gdec_ref, dq_ref, dk_ref, dvt_ref, dv_ref) = refs
    x = x_ref[0]
    h, _ = _modulate(x, mod_ref, 3)
    h = h.astype(BF16)
    o_gla = 2 * c_conv
    o_diff = o_gla + 4 * w_gla
    o_lr = o_diff + 3 * w_diff

    tm = x.shape[0]
    n_parts = MIXIN_PARTS if tm % (MIXIN_PARTS * LANES) == 0 else 1
    halves = [slice(i * tm // n_parts, (i + 1) * tm // n_parts) for i in range(n_parts)]
    zs = []
    for rs in halves:
        hh = h[rs]
        zs.append((_dot(hh, w_ref[:, 0:o_gla]), _dot(hh, w_ref[:, o_gla:o_diff]),
                   _dot(hh, w_ref[:, o_lr:o_lr + LANES]).astype(BF16), _dot(hh, w_ref[:, o_diff:o_lr])))
    pres = [_dot(z[2], w2_ref[...]) + b2_ref[...] for z in zs]

    for rs, (zc, zg, _, zd), pre in zip(halves, zs, pres):
        yglu_ref[0, rs, :] = zc[:, :c_conv] * jax.nn.sigmoid(zc[:, c_conv:])
        gq_ref[0, rs, :] = zg[:, 0:w_gla] * (DK_GLA ** -0.5)
        gk_ref[0, rs, :] = zg[:, w_gla:2 * w_gla]
        gv_ref[0, rs, :] = zg[:, 2 * w_gla:3 * w_gla].astype(BF16)
        gate_ref[0, rs, :] = zg[:, 3 * w_gla:4 * w_gla]
        gdec_ref[0, rs, :] = _log_sigmoid(pre) * (1.0 / GLA_TAU)
        if rope:
            cos = cos_ref[rs, :]
            sin = sin_ref[rs, :]
            lane = lax.broadcasted_iota(jnp.int32, cos.shape, 1)
            first_half = (lane % 32) < 16
        for hd in range(H_DIFF):
            sl = slice(hd * LANES, (hd + 1) * LANES)
            q = zd[:, sl]
            k = zd[:, w_diff + hd * LANES:w_diff + (hd + 1) * LANES]
            v = zd[:, 2 * w_diff + hd * LANES:2 * w_diff + (hd + 1) * LANES]
            if rope:
                q = _rope(q, cos, sin, first_half)
                k = _rope(k, cos, sin, first_half)
            dq_ref[0, rs, sl] = (q * (DH_DIFF ** -0.5 * math.log2(math.e))).astype(BF16)
            dk_ref[0, hd, rs, :] = k.astype(dk_ref.dtype)
            dvt_ref[0, hd, 0, 0:LANES, rs] = v.T.astype(BF16)
            if not rope:
                dv_ref[0, hd, rs, :] = v
    for hd in range(H_DIFF):
        dvt_ref[0, hd, 0, LANES:, :] = jnp.ones((ONES_ROWS, tm), BF16)


def _mixin(x, mod, w, w2, b2, cos, sin, *, tm):
    b, t, d = x.shape
    rope = cos is not None
    w_gla = H_GLA * DK_GLA
    w_diff = H_DIFF * 2 * DH_DIFF
    c_conv = (w.shape[1] - LANES - 4 * w_gla - 3 * w_diff) // 2
    kv_dtype = BF16 if rope else F32
    mod_rows = mod.shape[0]
    mod_idx = (lambda bi, ti: (bi, 0, 0)) if mod_rows > 1 else (lambda bi, ti: (0, 0, 0))
    tok = lambda n: pl.BlockSpec((1, tm, n), lambda bi, ti: (bi, ti, 0))
    heads = pl.BlockSpec((1, H_DIFF, tm, LANES), lambda bi, ti: (bi, 0, ti, 0))
    in_specs = [tok(d), pl.BlockSpec((1, N_MOD, d), mod_idx),
                _const_spec(w.shape), _const_spec(w2.shape), _const_spec(b2.shape)]
    args = [x, mod, w, w2, b2]
    if rope:
        in_specs += [pl.BlockSpec((tm, LANES), lambda bi, ti: (ti, 0))] * 2
        args += [cos, sin]
    sds = jax.ShapeDtypeStruct
    out_shape = [
        sds((b, t, c_conv), F32),
        sds((b, t, w_gla), F32),
        sds((b, t, w_gla), F32),
        sds((b, t, w_gla), BF16),
        sds((b, t, w_gla), F32),
        sds((b, t, 2 * w_gla), F32),
        sds((b, t, w_diff), BF16),
        sds((b, H_DIFF, t, LANES), kv_dtype),
        sds((b, H_DIFF, t // tm, LANES + ONES_ROWS, tm), BF16),
    ]
    out_specs = [tok(c_conv), tok(w_gla), tok(w_gla), tok(w_gla), tok(w_gla), tok(2 * w_gla),
                 tok(w_diff), heads,
                 pl.BlockSpec((1, H_DIFF, 1, LANES + ONES_ROWS, tm), lambda bi, ti: (bi, 0, ti, 0, 0))]
    if not rope:
        out_shape.append(sds((b, H_DIFF, t, LANES), F32))
        out_specs.append(heads)
    kern = functools.partial(_mixin_kernel, rope=rope, c_conv=c_conv, w_gla=w_gla, w_diff=w_diff)
    return pl.pallas_call(
        kern,
        grid=(b, t // tm),
        in_specs=in_specs,
        out_specs=out_specs,
        out_shape=out_shape,
        compiler_params=_cparams(("parallel", "parallel")),
        name="mixer_in",
    )(*args)


def _conv_kernel(prev_ref, cur_ref, next_ref, w_ref, b_ref, g_ref, beta_ref, o_ref, win_ref, *, tc, nt):
    i = pl.program_id(1)
    pad = CONV_WIDTH // 2
    win_ref[0:CONV_HALO] = jnp.where(i > 0, prev_ref[0], 0.0)
    win_ref[CONV_HALO:CONV_HALO + tc] = cur_ref[0]
    win_ref[CONV_HALO + tc:2 * CONV_HALO + tc] = jnp.where(i < nt - 1, next_ref[0], 0.0)
    w = w_ref[...]
    first = CONV_HALO - pad
    n_groups = -(-(first + CONV_WIDTH) // SUBLANES)
    for r in range(0, tc, CONV_ROWS):
        wide = win_ref[r:r + CONV_ROWS + n_groups * SUBLANES, :]
        acc = None
        for s in range(SUBLANES):
            z = None
            for a in range(n_groups):
                k = a * SUBLANES + s - first
                if 0 <= k < CONV_WIDTH:
                    term = wide[a * SUBLANES:a * SUBLANES + CONV_ROWS + SUBLANES, :] * w[k:k + 1, :]
                    z = term if z is None else z + term
            if z is not None:
                acc = z[s:s + CONV_ROWS, :] if acc is None else acc + z[s:s + CONV_ROWS, :]
        y = _layer_norm(acc + b_ref[...], g_ref[...], beta_ref[...])
        o_ref[0, r:r + CONV_ROWS, :] = _silu(y).astype(o_ref.dtype)


def _conv(yglu, w, bias, g, beta, *, tc):
    b, t, c = yglu.shape
    nt = t // tc
    hb = tc // CONV_HALO
    n_halo = t // CONV_HALO
    kern = functools.partial(_conv_kernel, tc=tc, nt=nt)
    return pl.pallas_call(
        kern,
        grid=(b, nt),
        in_specs=[
            pl.BlockSpec((1, CONV_HALO, c), lambda bi, i: (bi, jnp.maximum(i * hb - 1, 0), 0)),
            pl.BlockSpec((1, tc, c), lambda bi, i: (bi, i, 0)),
            pl.BlockSpec((1, CONV_HALO, c), lambda bi, i: (bi, jnp.minimum((i + 1) * hb, n_halo - 1), 0)),
            _const_spec(w.shape), _const_spec((1, c)), _const_spec((1, c)), _const_spec((1, c)),
        ],
        out_specs=pl.BlockSpec((1, tc, c), lambda bi, i: (bi, i, 0)),
        out_shape=jax.ShapeDtypeStruct((b, t, c), BF16),
        scratch_shapes=[pltpu.VMEM((tc + 2 * CONV_HALO, c), F32)],
        compiler_params=_cparams(("parallel", "parallel")),
        name="conv_module",
    )(yglu, yglu, yglu, w, bias, g, beta)


def _split3(x):
    hi = x.astype(BF16)
    r1 = x - hi.astype(F32)
    mid = r1.astype(BF16)
    lo = (r1 - mid.astype(F32)).astype(BF16)
    return hi, mid, lo


def _gla_kernel(*refs, nc, has_state, want_state):
    it = iter(refs)
    ins = [[next(it) for _ in range(4)] for _ in range(2)]
    s0_ref = next(it) if has_state else None
    o_refs = [next(it), next(it)]
    sfin_ref = next(it) if want_state else None
    s_scr, qt_scr, kv_scr, dec_scr, sin_scr = (next(it) for _ in range(5))

    i = pl.program_id(1)
    nblk = pl.num_programs(1)
    c = GLA_CHUNK
    width = s_scr.shape[1]
    nh = width // c

    @pl.when(i == 0)
    def _():
        if has_state:
            for d in range(2):
                bd_rows = [jnp.concatenate([s0_ref[d, h] if g == h else jnp.zeros((c, c), F32) for g in range(nh)],
                                           axis=1) for h in range(nh)]
                s_scr[d] = jnp.concatenate(bd_rows, axis=0).T
        else:
            s_scr[...] = jnp.zeros_like(s_scr)

    row = lax.broadcasted_iota(jnp.int32, (c, c), 0)
    col = lax.broadcasted_iota(jnp.int32, (c, c), 1)
    rowa = lax.broadcasted_iota(jnp.int32, (c, width), 0)
    cola = lax.broadcasted_iota(jnp.int32, (c, width), 1) % c
    tri_b = [jnp.where(row >= col, 1.0, 0.0).astype(BF16), jnp.where(row <= col, 1.0, 0.0).astype(BF16)]
    tri_cat = [rowa >= cola, rowa <= cola]
    rb = lax.broadcasted_iota(jnp.int32, (width, width), 0) // c
    cb = lax.broadcasted_iota(jnp.int32, (width, width), 1) // c
    bd = rb == cb
    rows = [slice(ci * c, (ci + 1) * c) for ci in range(nc)]
    work = [(d, ci) for ci in range(nc) for d in range(2)]

    gcums = {}
    for d, ci in work:
        ghi, gmid, glo = _split3(ins[d][3][0, rows[ci], :])
        gcums[d, ci] = _dot(tri_b[d], ghi) + _dot(tri_b[d], gmid) + _dot(tri_b[d], glo)
    scores = {}
    for d, ci in work:
        q_ref, k_ref, v_ref, _ = ins[d]
        rs = rows[ci]
        gcum = gcums[d, ci]
        gtot = gcum[c - 1:c, :] if d == 0 else gcum[0:1, :]
        k = k_ref[0, rs, :]
        q_t = (q_ref[0, rs, :] * jnp.exp(gcum)).astype(BF16)
        k_t = k * jnp.exp(-gcum)
        k_hat = (k * jnp.exp(gtot - gcum)).astype(BF16)
        k_bd = jnp.where(bd, jnp.concatenate([k_t] * nh, axis=0), 0.0).astype(BF16)
        scores[d, ci] = _dot_nt(q_t, k_bd)
        qt_scr[d, rs, :] = q_t
        kv_scr[d, ci] = jnp.where(bd, _dot_tn(v_ref[0, rs, :], k_hat), 0.0)
        dec_scr[d, ci] = jnp.broadcast_to(jnp.exp(gtot), dec_scr.shape[2:])
    for d, ci in work:
        v = ins[d][2][0, rows[ci], :]
        a = jnp.where(tri_cat[d], scores[d, ci], 0.0).astype(BF16)
        v_bd = jnp.where(bd, jnp.concatenate([v] * nh, axis=0), jnp.zeros((), v.dtype))
        o_refs[d][0, rows[ci], :] = _dot(a, v_bd)

    scan = [(d, j, j if d == 0 else nc - 1 - j) for j in range(nc) for d in range(2)]
    for d, j, ci in scan:
        s_in = s_scr[d]
        sin_scr[d, j] = s_in.astype(BF16)
        s_scr[d] = s_in * dec_scr[d, ci, 0:1, :] + kv_scr[d, ci]
    for d, j, ci in scan:
        o_refs[d][0, rows[ci], :] += _dot_nt(qt_scr[d, rows[ci], :], sin_scr[d, j])

    if want_state:
        @pl.when(i == nblk - 1)
        def _():
            for d in range(2):
                s_t = s_scr[d].T
                for h in range(nh):
                    sfin_ref[d, h] = s_t[h * c:(h + 1) * c, h * c:(h + 1) * c]


def _gla(q, k, v, gdec, s0, *, tb, want_state):
    b, t, w = q.shape
    nblk = t // tb
    nc = tb // GLA_CHUNK
    blk = [lambda i: i, lambda i: nblk - 1 - i]
    in_specs, args = [], []
    for d in range(2):
        tok = pl.BlockSpec((1, tb, w), lambda bi, i, d=d: (bi, blk[d](i), 0))
        in_specs += [tok, tok, tok, pl.BlockSpec((1, tb, w), lambda bi, i, d=d: (bi, blk[d](i), d))]
        args += [q, k, v, gdec]
    has_state = s0 is not None
    nh = w // GLA_CHUNK
    state_block = (None, 2, nh, GLA_CHUNK, GLA_CHUNK)
    if has_state:
        states, layer = s0
        in_specs.append(pl.BlockSpec((None,) + state_block, lambda bi, i: (bi, layer, 0, 0, 0, 0)))
        args.append(states)
    out_shape = [jax.ShapeDtypeStruct((b, t, w), F32)] * 2
    out_specs = [pl.BlockSpec((1, tb, w), lambda bi, i, d=d: (bi, blk[d](i), 0)) for d in range(2)]
    if want_state:
        out_shape.append(jax.ShapeDtypeStruct((b, 2, nh, GLA_CHUNK, GLA_CHUNK), F32))
        out_specs.append(pl.BlockSpec(state_block, lambda bi, i: (bi, 0, 0, 0, 0)))
    kern = functools.partial(_gla_kernel, nc=nc, has_state=has_state, want_state=want_state)
    return pl.pallas_call(
        kern,
        grid=(b, nblk),
        in_specs=in_specs,
        out_specs=out_specs,
        out_shape=out_shape,
        scratch_shapes=[pltpu.VMEM((2, w, w), F32), pltpu.VMEM((2, tb, w), BF16),
                        pltpu.VMEM((2, nc, w, w), F32), pltpu.VMEM((2, nc, SUBLANES, w), F32),
                        pltpu.VMEM((2, nc, w, w), BF16)],
        compiler_params=_cparams(("parallel", "arbitrary")),
        name="gla",
    )(*args)


def _attn_kernel(*refs, has_cache, lam_init, n_slots):
    it = iter(refs)
    q_ref, k_ref, vt_ref = next(it), next(it), next(it)
    kc_ref, vtc_ref = (next(it), next(it)) if has_cache else (None, None)
    lam_ref, g_ref, o_ref = next(it), next(it), next(it)
    s_scr = [next(it) for _ in range(n_slots)]
    e_scr = [next(it) for _ in range(n_slots)]
    acc_scr = next(it)
    p = lam_ref[...]
    lam = (jnp.exp(jnp.sum(p[0:1] * p[1:2], axis=-1, keepdims=True))
           - jnp.exp(jnp.sum(p[2:3] * p[3:4], axis=-1, keepdims=True)) + lam_init)
    for hh in range(k_ref.shape[1]):
        lanes = pl.ds(hh * LANES, LANES)
        _attn_head(q_ref.at[0, :, lanes], k_ref.at[0, hh], vt_ref.at[0, hh],
                   kc_ref.at[0, hh] if has_cache else None, vtc_ref.at[0, hh] if has_cache else None,
                   lam, g_ref, o_ref.at[0, :, lanes], s_scr, e_scr, acc_scr, lam_init)


def _attn_head(q_ref, k_ref, vt_ref, kc_ref, vtc_ref, lam, g_ref, o_ref, s_scr, e_scr, acc_scr, lam_init):
    has_cache = kc_ref is not None
    n_slots = len(s_scr)
    q = q_ref[...]
    tq = q.shape[0]
    lane = lax.broadcasted_iota(jnp.int32, q.shape, 1)
    zero = jnp.zeros((), q.dtype)
    qs = jnp.concatenate([jnp.where(lane < DH_DIFF, q, zero), jnp.where(lane >= DH_DIFF, q, zero)], axis=0)

    tk = vt_ref.shape[2]
    nq = 2 * tq
    acc_scr[...] = jnp.zeros_like(acc_scr)
    group = s_scr[0].shape[0] // tk
    tiles = []
    for kr, vr in [(k_ref, vt_ref)] + ([(kc_ref, vtc_ref)] if has_cache else []):
        for first in range(0, vr.shape[0], group):
            tiles.append((kr, vr, first, min(group, vr.shape[0] - first)))
    n = len(tiles)

    def scores(j, slot):
        kr, _, first, cnt = tiles[j]
        rows = cnt * tk
        s_scr[slot][0:rows, :] = _dot_nt(kr[first * tk:first * tk + rows, :].astype(BF16), qs)
        strips = [s_scr[slot][r:r + ATTN_STRIP, :] for r in range(0, rows, ATTN_STRIP)]
        return jnp.max(functools.reduce(jnp.maximum, strips), axis=0, keepdims=True)

    def softmax(j, slot, mt, m):
        m_new = jnp.maximum(m, mt)
        alpha = jnp.exp2(m - m_new)
        for r in range(0, tiles[j][3] * tk, ATTN_STRIP):
            e_scr[slot][r:r + ATTN_STRIP, :] = jnp.exp2(s_scr[slot][r:r + ATTN_STRIP, :] - m_new).astype(BF16)
        return m_new, alpha

    def weighted(j, slot, alpha):
        _, vr, first, cnt = tiles[j]
        pv = functools.reduce(lambda a, b: a + b, [_dot(vr[first + p], e_scr[slot][p * tk:(p + 1) * tk, :])
                                                   for p in range(cnt)])
        acc_scr[...] = acc_scr[...] * alpha + pv

    slot = lambda j: j % n_slots

    lag = min(ATTN_LAG, n_slots - 1)
    ahead = min(ATTN_AHEAD, n_slots - 1)
    m = jnp.full((1, nq), -1e30, F32)
    mts = {j: scores(j, slot(j)) for j in range(min(ahead, n))}
    alphas = {}
    for j in range(n + lag):
        if j + ahead < n:
            mts[j + ahead] = scores(j + ahead, slot(j + ahead))
        if j < n:
            m, alphas[j] = softmax(j, slot(j), mts.pop(j), m)
        if 0 <= j - lag < n:
            weighted(j - lag, slot(j - lag), alphas.pop(j - lag))
    dv = o_ref.shape[1]
    on = acc_scr[0:dv, :] / acc_scr[dv:dv + 1, :]
    o = on[:, :tq] - lam * on[:, tq:]
    ms = jnp.mean(o * o, axis=0, keepdims=True)
    y = o * lax.rsqrt(ms + LN_EPS) * g_ref[...] * (1.0 - lam_init)
    o_ref[...] = y.T.astype(o_ref.dtype)


def _attn(dq, dk, dvt, ck, cvt, lam_p, g_col, *, lam_init, tq):
    b, t, _ = dq.shape
    dv, tk = dvt.shape[3:]
    has_cache = ck is not None
    group = max(1, min(ATTN_KEY_TILE // tk, dvt.shape[2]))
    n_tiles = -(-dvt.shape[2] // group) + (-(-cvt.shape[2] // group) if has_cache else 0)
    n_slots = min(n_tiles, ATTN_SLOTS)
    rows = group * tk
    hps = H_DIFF if n_tiles == 1 else 1
    qspec = pl.BlockSpec((1, tq, hps * LANES), lambda bi, h, qi: (bi, qi, h))
    kspec = lambda a: pl.BlockSpec((1, hps) + a.shape[2:], lambda bi, h, qi: (bi, h, 0, 0))
    vspec = lambda a: pl.BlockSpec((1, hps) + a.shape[2:], lambda bi, h, qi: (bi, h, 0, 0, 0))
    kv_args = [dk, dvt] + ([ck, cvt] if has_cache else [])
    kv_specs = [kspec(dk), vspec(dvt)] + ([kspec(ck), vspec(cvt)] if has_cache else [])
    kern = functools.partial(_attn_kernel, has_cache=has_cache, lam_init=lam_init, n_slots=n_slots)
    return pl.pallas_call(
        kern,
        grid=(b, H_DIFF // hps, t // tq),
        in_specs=[qspec] + kv_specs + [_const_spec(lam_p.shape), _const_spec(g_col.shape)],
        out_specs=qspec,
        out_shape=jax.ShapeDtypeStruct(dq.shape, BF16),
        scratch_shapes=([pltpu.VMEM((rows, 2 * tq), F32)] * n_slots + [pltpu.VMEM((rows, 2 * tq), BF16)] * n_slots
                        + [pltpu.VMEM((dv, 2 * tq), F32)]),
        compiler_params=_cparams(("parallel", "parallel", "parallel")),
        name="diff_attn",
    )(dq, *kv_args, lam_p, g_col)


def _mixout_kernel(x_ref, mod_ref, yc_ref, of_ref, ob_ref, gate_ref, yd_ref, gn_ref, w_ref,
                   lng_ref, lnb_ref, o_ref, *, alpha):
    g1 = mod_ref[0, 5:6, :]
    tm = x_ref.shape[1]
    width = of_ref.shape[2]
    c_conv = yc_ref.shape[2]
    rb = lax.broadcasted_iota(jnp.int32, (width, width), 0) // DK_GLA
    cb = lax.broadcasted_iota(jnp.int32, (width, width), 1) // DK_GLA
    ones_bd = jnp.where(rb == cb, 1.0, 0.0).astype(BF16)
    n_parts = MIXOUT_PARTS if tm % (MIXOUT_PARTS * LANES) == 0 else 1
    parts = [slice(i * tm // n_parts, (i + 1) * tm // n_parts) for i in range(n_parts)]
    sums = []
    for rs in parts:
        o = of_ref[0, rs, :] + ob_ref[0, rs, :]
        sq = o * o
        hi = sq.astype(BF16)
        lo = (sq - hi.astype(F32)).astype(BF16)
        sums.append((o, _dot(hi, ones_bd) + _dot(lo, ones_bd)))
    ys = []
    for rs, (o, ss) in zip(parts, sums):
        yg = (o * lax.rsqrt(ss * (1.0 / DK_GLA) + LN_EPS) * gn_ref[...] * _silu(gate_ref[0, rs, :])).astype(BF16)
        ys.append(_dot(yc_ref[0, rs, :], w_ref[0:c_conv, :])
                  + _dot(yg, w_ref[c_conv:c_conv + width, :])
                  + _dot(yd_ref[0, rs, :], w_ref[c_conv + width:, :]))
    for rs, y in zip(parts, ys):
        o_ref[0, rs, :] = _layer_norm(alpha * x_ref[0, rs, :] + g1 * y, lng_ref[...], lnb_ref[...])


def _mixout(x, mod, yconv, o_fwd, o_bwd, gate, ydiff, gn, w, lng, lnb, *, alpha, tm):
    b, t, d = x.shape
    mod_rows = mod.shape[0]
    mod_idx = (lambda bi, ti: (bi, 0, 0)) if mod_rows > 1 else (lambda bi, ti: (0, 0, 0))
    tok = lambda n: pl.BlockSpec((1, tm, n), lambda bi, ti: (bi, ti, 0))
    wg = o_fwd.shape[2]
    kern = functools.partial(_mixout_kernel, alpha=alpha)
    return pl.pallas_call(
        kern,
        grid=(b, t // tm),
        in_specs=[
            tok(d), pl.BlockSpec((1, N_MOD, d), mod_idx), tok(yconv.shape[2]),
            tok(wg), tok(wg), tok(wg), tok(ydiff.shape[2]),
            _const_spec(gn.shape), _const_spec(w.shape), _const_spec((1, d)), _const_spec((1, d)),
        ],
        out_specs=tok(d),
        out_shape=jax.ShapeDtypeStruct((b, t, d), F32),
        compiler_params=_cparams(("parallel", "parallel")),
        name="mixer_out",
    )(x, mod, yconv, o_fwd, o_bwd, gate, ydiff, gn, w, lng, lnb)


def _rope_tables(t):
    rows = t // GRID_W
    row = jnp.repeat(jnp.arange(rows, dtype=F32), GRID_W)
    col = jnp.tile(jnp.arange(GRID_W, dtype=F32), rows)
    seg = DH_DIFF // 2
    inv = ROPE_BASE ** (-jnp.arange(0, seg, 2, dtype=F32) / seg)
    a_r = row[:, None] * inv
    a_c = col[:, None] * inv
    ang = jnp.concatenate([a_r, a_r, a_c, a_c], axis=-1)
    ang = jnp.concatenate([ang, ang], axis=-1)
    sign = jnp.where((jnp.arange(LANES) % 32) < 16, -1.0, 1.0).astype(F32)
    return jnp.cos(ang), jnp.sin(ang) * sign


def _ffn_weights(w_in, w_out, ck):
    depth, ff, d = w_out.shape
    return w_in.astype(BF16), w_out.reshape(depth, ff // ck, ck, d).astype(BF16)


def _pick_tile(t, pref):
    return pref if t % pref == 0 else t


def kernel(x_prompt, x_sample, cache_diff_k, cache_diff_v, state_gla, c, c_ctx, w_ada, b_ada, w_ffn1_in,
           w_ffn1_out, w_ffn2_in, w_ffn2_out, w_in, conv_w, conv_b, conv_ln_g, conv_ln_b, gla_w_a2, gla_b_a,
           gla_norm_g, diff_lam, diff_norm_g, w_out, ln_g, ln_b):
    depth, d, _ = w_ada.shape
    alpha = (2 * depth) ** 0.25
    n_dec = c.shape[0]
    c_conv = conv_w.shape[2]
    w_gla = H_GLA * DK_GLA
    w_diff = H_DIFF * 2 * DH_DIFF
    in_conv, in_gla = 2 * c_conv, 4 * w_gla + 2 * GLA_RANK
    ff_chunk = 256

    rows = -(-(n_dec + 1) // SUBLANES) * SUBLANES
    cvec = jnp.zeros((rows, d), F32).at[:n_dec].set(c).at[n_dec].set(c_ctx)
    mod = _ada(cvec, w_ada, b_ada).reshape(depth, rows, N_MOD, d)

    cos, sin = _rope_tables(x_sample.shape[1])

    ffn1_w = _ffn_weights(w_ffn1_in, w_ffn1_out, ff_chunk)
    ffn2_w = _ffn_weights(w_ffn2_in, w_ffn2_out, ff_chunk)
    layers = []
    for l in range(depth):
        wi = w_in[l]
        w_lr = jnp.pad(wi[:, in_conv + 4 * w_gla:in_conv + in_gla], ((0, 0), (0, LANES - 2 * GLA_RANK)))
        w_all = jnp.concatenate([wi[:, :in_conv + 4 * w_gla], wi[:, in_conv + in_gla:], w_lr], axis=1).astype(BF16)
        w2 = jnp.zeros((LANES, 2 * w_gla), F32)
        w2 = w2.at[:GLA_RANK, :w_gla].set(gla_w_a2[l, 0]).at[GLA_RANK:2 * GLA_RANK, w_gla:].set(gla_w_a2[l, 1])
        layers.append(dict(
            ffn1=ffn1_w + (l,), ffn2=ffn2_w + (l,),
            w_all=w_all, w2=w2.astype(BF16), b2=gla_b_a[l].reshape(1, 2 * w_gla),
            conv_w=conv_w[l], conv_b=conv_b[l][None], conv_g=conv_ln_g[l][None], conv_beta=conv_ln_b[l][None],
            gn=jnp.tile(gla_norm_g[l], H_GLA)[None], lam=diff_lam[l], dg=diff_norm_g[l][:, None],
            w_out=w_out[l].astype(BF16),
            lng=[ln_g[l, i][None] for i in range(3)], lnb=[ln_b[l, i][None] for i in range(3)],
            lam_init=0.8 - 0.6 * math.exp(-0.3 * l),
        ))

    def run_layer(x, mod_l, P, ctx):
        t = x.shape[1]
        tm = _pick_tile(t, 512)
        bsz, _, d_model = x.shape
        shared_mod = mod_l.shape[0] == 1
        tf = _pick_tile(bsz * t if shared_mod else t, 1024)

        def ffn(x, which, row0, ln_idx):
            y = _ffn(x.reshape(bsz * t, d_model), mod_l, *P[which], P['lng'][ln_idx], P['lnb'][ln_idx],
                     row0=row0, alpha=alpha, tm=tf)
            return y.reshape(bsz, t, d_model)

        x = ffn(x, 'ffn1', 0, 0)
        rope = ctx is not None
        yglu, gq, gk, gv, gate, gdec, dq, dk, dvt, *dv = _mixin(
            x, mod_l, P['w_all'], P['w2'], P['b2'], cos if rope else None, sin if rope else None, tm=tm)
        yconv = _conv(yglu, P['conv_w'], P['conv_b'], P['conv_g'], P['conv_beta'], tc=_pick_tile(t, 256))
        gla_out = _gla(gq, gk, gv, gdec, ctx['s0'] if rope else None, tb=_pick_tile(t, 512), want_state=not rope)
        ydiff = _attn(dq, dk, dvt, ctx['k'] if rope else None, ctx['vt'] if rope else None, P['lam'], P['dg'],
                      lam_init=P['lam_init'], tq=_pick_tile(t, 512))
        x = _mixout(x, mod_l, yconv, gla_out[0], gla_out[1], gate, ydiff, P['gn'], P['w_out'], P['lng'][1], P['lnb'][1],
                    alpha=alpha, tm=tm)
        x = ffn(x, 'ffn2', 6, 2)
        return x, (dk, dv[0] if dv else None, gla_out[2] if not rope else None)

    xp = x_prompt
    new_k, new_v, new_s = [], [], []
    for l in range(depth):
        xp, (k_l, v_l, s_l) = run_layer(xp, mod[l, n_dec:n_dec + 1], layers[l], None)
        new_k.append(k_l)
        new_v.append(v_l)
        new_s.append(s_l)

    xs = x_sample
    for l in range(depth):
        past = cache_diff_v.shape[3]
        tile = _pick_tile(x_sample.shape[1], 512)
        assert past % tile == 0, "cached context length must be a whole number of key tiles"
        cvt = cache_diff_v[:, l].astype(BF16).reshape(n_dec, H_DIFF, past // tile, tile, LANES)
        cvt = jnp.concatenate([jnp.swapaxes(cvt, -1, -2), jnp.ones(cvt.shape[:3] + (ONES_ROWS, tile), BF16)], axis=3)
        ctx = dict(k=cache_diff_k[:, l].astype(BF16), vt=cvt, s0=(state_gla, l))
        xs, _ = run_layer(xs, mod[l, :n_dec], layers[l], ctx)

    return (xp, xs, jnp.stack(new_k, axis=1), jnp.stack(new_v, axis=1), jnp.stack(new_s, axis=1))
```

```python
import functools
import math

import jax
import jax.numpy as jnp
from jax import lax
from jax.experimental import pallas as pl
from jax.experimental.pallas import tpu as pltpu

F32 = jnp.float32
BF16 = jnp.bfloat16

GRID_W = 64
CONV_WIDTH = 31
H_GLA = 4
DK_GLA = 64
GLA_RANK = 16
GLA_TAU = 16.0
GLA_CHUNK = 64
H_DIFF = 4
DH_DIFF = 64
ROPE_BASE = 10000.0
LN_EPS = 1e-5
N_MOD = 9

LANES = 128
SUBLANES = 8
V7X_VMEM_BYTES = 64 * 1024 * 1024
VMEM_LIMIT = V7X_VMEM_BYTES - 8 * 1024 * 1024

CONV_HALO = 16
CONV_ROWS = 32
ATTN_STRIP = 32
MIXIN_PARTS = 2
MIXOUT_PARTS = 2
ATTN_KEY_TILE = 512
ATTN_AHEAD = 1
ATTN_LAG = 1
ATTN_SLOTS = 3
ONES_ROWS = 16
ROW_STRIP = 32
LN_STRIP = 64


def _cparams(sem):
    return pltpu.CompilerParams(dimension_semantics=sem, vmem_limit_bytes=VMEM_LIMIT)


def _const_spec(shape):
    nd = len(shape)
    return pl.BlockSpec(shape, lambda *_: (0,) * nd, pipeline_mode=pl.Buffered(1))


def _dot(a, b):
    return jnp.dot(a, b, preferred_element_type=F32)


def _dot_nt(a, b):
    return lax.dot_general(a, b, (((1,), (1,)), ((), ())), preferred_element_type=F32)


def _dot_tn(a, b):
    return lax.dot_general(a, b, (((0,), (0,)), ((), ())), preferred_element_type=F32)


def _silu(x):
    return x * jax.nn.sigmoid(x)


def _layer_norm(y, g, b):
    mu = jnp.mean(y, axis=-1, keepdims=True)
    yc = y - mu
    var = jnp.mean(yc * yc, axis=-1, keepdims=True)
    return yc * lax.rsqrt(var + LN_EPS) * g + b


def _modulate(x, mod_ref, row0):
    sh = mod_ref[0, row0:row0 + 1, :]
    sc = mod_ref[0, row0 + 1:row0 + 2, :]
    g = mod_ref[0, row0 + 2:row0 + 3, :]
    return x * (1.0 + sc) + sh, g


def _ada_kernel(c_ref, w_ref, b_ref, o_ref):
    s = _silu(c_ref[...]).astype(BF16)
    o_ref[0] = _dot(s, w_ref[0].astype(BF16)) + b_ref[0]


def _ada(cvec, w_ada, b_ada):
    depth, d, n = w_ada.shape
    rows = cvec.shape[0]
    tn = d
    return pl.pallas_call(
        _ada_kernel,
        grid=(depth, n // tn),
        in_specs=[
            pl.BlockSpec((rows, d), lambda l, j: (0, 0)),
            pl.BlockSpec((1, d, tn), lambda l, j: (l, 0, j)),
            pl.BlockSpec((1, 1, tn), lambda l, j: (l, 0, j)),
        ],
        out_specs=pl.BlockSpec((1, rows, tn), lambda l, j: (l, 0, j)),
        out_shape=jax.ShapeDtypeStruct((depth, rows, n), F32),
        compiler_params=_cparams(("arbitrary", "arbitrary")),
        name="ada_mod",
    )(cvec, w_ada, b_ada.reshape(depth, 1, n))


def _strips(n_rows, fn):
    def body(i, carry):
        fn(pl.ds(pl.multiple_of(i * ROW_STRIP, ROW_STRIP), ROW_STRIP))
        return carry

    lax.fori_loop(0, n_rows // ROW_STRIP, body, 0, unroll=2)


def _layer_norm_rows(n_rows, make_y, o_ref, g, b):
    n = n_rows // LN_STRIP
    sl = lambda i: pl.ds(i * LN_STRIP if isinstance(i, int) else pl.multiple_of(i * LN_STRIP, LN_STRIP), LN_STRIP)

    def means(i):
        y = make_y(sl(i))
        o_ref[sl(i), :] = y
        return jnp.mean(y, axis=-1, keepdims=True)

    def rstd(i, mu):
        yc = o_ref[sl(i), :] - mu
        return lax.rsqrt(jnp.mean(yc * yc, axis=-1, keepdims=True) + LN_EPS)

    def normalise(i, mu, r):
        o_ref[sl(i), :] = (o_ref[sl(i), :] - mu) * r * g + b

    if n == 1:
        mu = means(0)
        normalise(0, mu, rstd(0, mu))
        return
    mu_a = means(0)
    mu_b = means(1)
    r_a = rstd(0, mu_a)

    def body(i, carry):
        mu_a, r_a, mu_b = carry
        r_b = rstd(i - 1, mu_b)
        mu_c = means(i)
        normalise(i - 2, mu_a, r_a)
        return mu_b, r_b, mu_c

    mu_a, r_a, mu_b = lax.fori_loop(2, n, body, (mu_a, r_a, mu_b))
    r_b = rstd(n - 1, mu_b)
    normalise(n - 2, mu_a, r_a)
    normalise(n - 1, mu_b, r_b)


def _ffn_kernel(x_ref, mod_ref, win_ref, wout_ref, lng_ref, lnb_ref, o_ref, h_scr, acc_ref, y_scr, *, row0, alpha):
    i = pl.program_id(0)
    n_tiles = pl.num_programs(0) - 1
    tm = x_ref.shape[0]
    n_chunks, ck, _ = wout_ref.shape
    lng = lng_ref[...]
    lnb = lnb_ref[...]
    n_strips = tm // LN_STRIP
    unroll = 2
    n_iters = (n_chunks - 2) // unroll
    tail = range(1 + n_iters * unroll, n_chunks)
    per_iter = -(-n_strips // (n_iters + 2))

    def norm_strip(idx):
        idx = jnp.minimum(idx, n_strips - 1)
        rs = pl.ds(pl.multiple_of(idx * LN_STRIP, LN_STRIP), LN_STRIP)
        o_ref[rs, :] = _layer_norm(y_scr[rs, :], lng, lnb)

    @pl.when(i == 0)
    def _():
        y_scr[...] = jnp.zeros_like(y_scr)

    @pl.when(i < n_tiles)
    def _():
        sh = mod_ref[0, row0:row0 + 1, :]
        sc1 = 1.0 + mod_ref[0, row0 + 1:row0 + 2, :]
        half_g = 0.5 * mod_ref[0, row0 + 2:row0 + 3, :]

        h_scr[...] = (x_ref[...] * sc1 + sh).astype(BF16)

        def chunk(j):
            cols = lambda start: pl.ds(pl.multiple_of(start, ck), ck)
            h = h_scr[...]
            a = _dot(h, win_ref[:, cols(j * ck)])
            b = _dot(h, win_ref[:, cols(n_chunks * ck + j * ck)])
            return _dot((_silu(a) * b).astype(BF16), wout_ref[j])

        acc_ref[...] = chunk(0)
        for s in range(per_iter):
            norm_strip(s)

        def body(k, carry):
            for u in range(unroll):
                acc_ref[...] += chunk(1 + k * unroll + u)
            for s in range(per_iter):
                norm_strip((k + 1) * per_iter + s)
            return carry

        lax.fori_loop(0, n_iters, body, 0)
        for s in range(per_iter):
            norm_strip((n_iters + 1) * per_iter + s)
        for j in tail[:-1]:
            acc_ref[...] += chunk(j)
        y_scr[...] = alpha * x_ref[...] + half_g * (acc_ref[...] + chunk(tail[-1]))

    @pl.when(i == n_tiles)
    def _():
        _layer_norm_rows(tm, lambda rs: y_scr[rs, :], o_ref, lng, lnb)


def _layer_spec(stacked, layer):
    nd = stacked.ndim - 1
    return pl.BlockSpec((None,) + stacked.shape[1:], lambda *_: (layer,) + (0,) * nd, pipeline_mode=pl.Buffered(1))


def _ffn(x, mod, win, wout, layer, lng, lnb, *, row0, alpha, tm):
    n_tok, d = x.shape
    n_tiles = n_tok // tm
    per_req = n_tok // mod.shape[0]
    assert n_tok % tm == 0 and per_req % tm == 0, "a token tile must not straddle two requests"
    cur = lambda ti: jnp.minimum(ti, n_tiles - 1)
    kern = functools.partial(_ffn_kernel, row0=row0, alpha=alpha)
    return pl.pallas_call(
        kern,
        grid=(n_tiles + 1,),
        in_specs=[
            pl.BlockSpec((tm, d), lambda ti: (cur(ti), 0)),
            pl.BlockSpec((1, N_MOD, d), lambda ti: ((cur(ti) * tm) // per_req, 0, 0)),
            _layer_spec(win, layer),
            _layer_spec(wout, layer),
            _const_spec((1, d)),
            _const_spec((1, d)),
        ],
        out_specs=pl.BlockSpec((tm, d), lambda ti: (jnp.maximum(ti - 1, 0), 0)),
        out_shape=jax.ShapeDtypeStruct((n_tok, d), F32),
        scratch_shapes=[pltpu.VMEM((tm, d), BF16), pltpu.VMEM((tm, d), F32), pltpu.VMEM((tm, d), F32)],
        compiler_params=_cparams(("arbitrary",)),
        name="ffn",
    )(x, mod, win, wout, lng, lnb)


def _log_sigmoid(x):
    return jnp.minimum(x, 0.0) - jnp.log(1.0 + jnp.exp(-jnp.abs(x)))


def _rope(x, cos, sin_signed, first_half):
    rot = jnp.where(first_half, pltpu.roll(x, LANES - 16, 1), pltpu.roll(x, 16, 1))
    return x * cos + rot * sin_signed


def _mixin_kernel(*refs, rope, c_conv, w_gla, w_diff):
    if rope:
        (x_ref, mod_ref, w_ref, w2_ref, b2_ref, cos_ref, sin_ref,
         yglu_ref, gq_ref, gk_ref, gv_ref, gate_ref, gdec_ref, dq_ref, dk_ref, dvt_ref) = refs
    else:
        (x_ref, mod_ref, w_ref, w2_ref, b2_ref,
         yglu_ref, gq_ref, gk_ref, gv_ref, gate_ref, gdec_ref, dq_ref, dk_ref, dvt_ref, dv_ref) = refs
    x = x_ref[0]
    h, _ = _modulate(x, mod_ref, 3)
    h = h.astype(BF16)
    o_gla = 2 * c_conv
    o_diff = o_gla + 4 * w_gla
    o_lr = o_diff + 3 * w_diff

    tm = x.shape[0]
    n_parts = MIXIN_PARTS if tm % (MIXIN_PARTS * LANES) == 0 else 1
    halves = [slice(i * tm // n_parts, (i + 1) * tm // n_parts) for i in range(n_parts)]
    zs, pres = [], []
    for rs in halves:
        hh = h[rs]
        zd = _dot(hh, w_ref[:, o_diff:o_lr])
        zl = _dot(hh, w_ref[:, o_lr:o_lr + LANES]).astype(BF16)
        zg = _dot(hh, w_ref[:, o_gla:o_diff])
        pres.append(_dot(zl, w2_ref[...]) + b2_ref[...])
        zs.append((_dot(hh, w_ref[:, 0:o_gla]), zg, zl, zd))

    for rs, (zc, zg, _, zd), pre in zip(halves, zs, pres):
        yglu_ref[0, rs, :] = zc[:, :c_conv] * jax.nn.sigmoid(zc[:, c_conv:])
        gq_ref[0, rs, :] = zg[:, 0:w_gla] * (DK_GLA ** -0.5)
        gk_ref[0, rs, :] = zg[:, w_gla:2 * w_gla]
        gv_ref[0, rs, :] = zg[:, 2 * w_gla:3 * w_gla].astype(BF16)
        gate_ref[0, rs, :] = zg[:, 3 * w_gla:4 * w_gla]
        gdec_ref[0, rs, :] = _log_sigmoid(pre) * (1.0 / GLA_TAU)
        if rope:
            cos = cos_ref[rs, :]
            sin = sin_ref[rs, :]
            lane = lax.broadcasted_iota(jnp.int32, cos.shape, 1)
            first_half = (lane % 32) < 16
        for hd in range(H_DIFF):
            sl = slice(hd * LANES, (hd + 1) * LANES)
            q = zd[:, sl]
            k = zd[:, w_diff + hd * LANES:w_diff + (hd + 1) * LANES]
            v = zd[:, 2 * w_diff + hd * LANES:2 * w_diff + (hd + 1) * LANES]
            if rope:
                q = _rope(q, cos, sin, first_half)
                k = _rope(k, cos, sin, first_half)
            dq_ref[0, rs, sl] = (q * (DH_DIFF ** -0.5 * math.log2(math.e))).astype(BF16)
            dk_ref[0, hd, rs, :] = k.astype(dk_ref.dtype)
            dvt_ref[0, hd, 0, 0:LANES, rs] = v.T.astype(BF16)
            if not rope:
                dv_ref[0, hd, rs, :] = v
    for hd in range(H_DIFF):
        dvt_ref[0, hd, 0, LANES:, :] = jnp.ones((ONES_ROWS, tm), BF16)


def _mixin(x, mod, w, w2, b2, cos, sin, *, tm):
    b, t, d = x.shape
    rope = cos is not None
    w_gla = H_GLA * DK_GLA
    w_diff = H_DIFF * 2 * DH_DIFF
    c_conv = (w.shape[1] - LANES - 4 * w_gla - 3 * w_diff) // 2
    kv_dtype = BF16 if rope else F32
    mod_rows = mod.shape[0]
    mod_idx = (lambda bi, ti: (bi, 0, 0)) if mod_rows > 1 else (lambda bi, ti: (0, 0, 0))
    tok = lambda n: pl.BlockSpec((1, tm, n), lambda bi, ti: (bi, ti, 0))
    heads = pl.BlockSpec((1, H_DIFF, tm, LANES), lambda bi, ti: (bi, 0, ti, 0))
    in_specs = [tok(d), pl.BlockSpec((1, N_MOD, d), mod_idx),
                _const_spec(w.shape), _const_spec(w2.shape), _const_spec(b2.shape)]
    args = [x, mod, w, w2, b2]
    if rope:
        in_specs += [pl.BlockSpec((tm, LANES), lambda bi, ti: (ti, 0))] * 2
        args += [cos, sin]
    sds = jax.ShapeDtypeStruct
    out_shape = [
        sds((b, t, c_conv), F32),
        sds((b, t, w_gla), F32),
        sds((b, t, w_gla), F32),
        sds((b, t, w_gla), BF16),
        sds((b, t, w_gla), F32),
        sds((b, t, 2 * w_gla), F32),
        sds((b, t, w_diff), BF16),
        sds((b, H_DIFF, t, LANES), kv_dtype),
        sds((b, H_DIFF, t // tm, LANES + ONES_ROWS, tm), BF16),
    ]
    out_specs = [tok(c_conv), tok(w_gla), tok(w_gla), tok(w_gla), tok(w_gla), tok(2 * w_gla),
                 tok(w_diff), heads,
                 pl.BlockSpec((1, H_DIFF, 1, LANES + ONES_ROWS, tm), lambda bi, ti: (bi, 0, ti, 0, 0))]
    if not rope:
        out_shape.append(sds((b, H_DIFF, t, LANES), F32))
        out_specs.append(heads)
    kern = functools.partial(_mixin_kernel, rope=rope, c_conv=c_conv, w_gla=w_gla, w_diff=w_diff)
    return pl.pallas_call(
        kern,
        grid=(b, t // tm),
        in_specs=in_specs,
        out_specs=out_specs,
        out_shape=out_shape,
        compiler_params=_cparams(("parallel", "parallel")),
        name="mixer_in",
    )(*args)


def _conv_kernel(prev_ref, cur_ref, next_ref, w_ref, b_ref, g_ref, beta_ref, o_ref, win_ref, *, tc, nt):
    i = pl.program_id(1)
    pad = CONV_WIDTH // 2
    win_ref[0:CONV_HALO] = jnp.where(i > 0, prev_ref[0], 0.0)
    win_ref[CONV_HALO:CONV_HALO + tc] = cur_ref[0]
    win_ref[CONV_HALO + tc:2 * CONV_HALO + tc] = jnp.where(i < nt - 1, next_ref[0], 0.0)
    w = w_ref[...]
    first = CONV_HALO - pad
    n_groups = -(-(first + CONV_WIDTH) // SUBLANES)
    for r in range(0, tc, CONV_ROWS):
        wide = win_ref[r:r + CONV_ROWS + n_groups * SUBLANES, :]
        acc = None
        for s in range(SUBLANES):
            z = None
            for a in range(n_groups):
                k = a * SUBLANES + s - first
                if 0 <= k < CONV_WIDTH:
                    term = wide[a * SUBLANES:a * SUBLANES + CONV_ROWS + SUBLANES, :] * w[k:k + 1, :]
                    z = term if z is None else z + term
            if z is not None:
                acc = z[s:s + CONV_ROWS, :] if acc is None else acc + z[s:s + CONV_ROWS, :]
        y = _layer_norm(acc + b_ref[...], g_ref[...], beta_ref[...])
        o_ref[0, r:r + CONV_ROWS, :] = _silu(y).astype(o_ref.dtype)


def _conv(yglu, w, bias, g, beta, *, tc):
    b, t, c = yglu.shape
    nt = t // tc
    hb = tc // CONV_HALO
    n_halo = t // CONV_HALO
    kern = functools.partial(_conv_kernel, tc=tc, nt=nt)
    return pl.pallas_call(
        kern,
        grid=(b, nt),
        in_specs=[
            pl.BlockSpec((1, CONV_HALO, c), lambda bi, i: (bi, jnp.maximum(i * hb - 1, 0), 0)),
            pl.BlockSpec((1, tc, c), lambda bi, i: (bi, i, 0)),
            pl.BlockSpec((1, CONV_HALO, c), lambda bi, i: (bi, jnp.minimum((i + 1) * hb, n_halo - 1), 0)),
            _const_spec(w.shape), _const_spec((1, c)), _const_spec((1, c)), _const_spec((1, c)),
        ],
        out_specs=pl.BlockSpec((1, tc, c), lambda bi, i: (bi, i, 0)),
        out_shape=jax.ShapeDtypeStruct((b, t, c), BF16),
        scratch_shapes=[pltpu.VMEM((tc + 2 * CONV_HALO, c), F32)],
        compiler_params=_cparams(("parallel", "parallel")),
        name="conv_module",
    )(yglu, yglu, yglu, w, bias, g, beta)


def _split3(x):
    hi = x.astype(BF16)
    r1 = x - hi.astype(F32)
    mid = r1.astype(BF16)
    lo = (r1 - mid.astype(F32)).astype(BF16)
    return hi, mid, lo


def _gla_kernel(*refs, nc, has_state, want_state):
    it = iter(refs)
    ins = [[next(it) for _ in range(4)] for _ in range(2)]
    s0_ref = next(it) if has_state else None
    o_refs = [next(it), next(it)]
    sfin_ref = next(it) if want_state else None
    s_scr, qt_scr, kv_scr, dec_scr, sin_scr = (next(it) for _ in range(5))

    i = pl.program_id(1)
    nblk = pl.num_programs(1)
    c = GLA_CHUNK
    width = s_scr.shape[1]
    nh = width // c

    @pl.when(i == 0)
    def _():
        if has_state:
            for d in range(2):
                bd_rows = [jnp.concatenate([s0_ref[d, h] if g == h else jnp.zeros((c, c), F32) for g in range(nh)],
                                           axis=1) for h in range(nh)]
                s_scr[d] = jnp.concatenate(bd_rows, axis=0).T
        else:
            s_scr[...] = jnp.zeros_like(s_scr)

    row = lax.broadcasted_iota(jnp.int32, (c, c), 0)
    col = lax.broadcasted_iota(jnp.int32, (c, c), 1)
    rowa = lax.broadcasted_iota(jnp.int32, (c, width), 0)
    cola = lax.broadcasted_iota(jnp.int32, (c, width), 1) % c
    tri_b = [jnp.where(row >= col, 1.0, 0.0).astype(BF16), jnp.where(row <= col, 1.0, 0.0).astype(BF16)]
    tri_cat = [rowa >= cola, rowa <= cola]
    rb = lax.broadcasted_iota(jnp.int32, (width, width), 0) // c
    cb = lax.broadcasted_iota(jnp.int32, (width, width), 1) // c
    bd = rb == cb
    rows = [slice(ci * c, (ci + 1) * c) for ci in range(nc)]
    work = [(d, ci) for ci in range(nc) for d in range(2)]

    gcums = {}
    for d, ci in work:
        ghi, gmid, glo = _split3(ins[d][3][0, rows[ci], :])
        gcums[d, ci] = _dot(tri_b[d], ghi) + _dot(tri_b[d], gmid) + _dot(tri_b[d], glo)
    scores = {}
    for d, ci in work:
        q_ref, k_ref, v_ref, _ = ins[d]
        rs = rows[ci]
        gcum = gcums[d, ci]
        gtot = gcum[c - 1:c, :] if d == 0 else gcum[0:1, :]
        k = k_ref[0, rs, :]
        q_t = (q_ref[0, rs, :] * jnp.exp(gcum)).astype(BF16)
        k_t = k * jnp.exp(-gcum)
        k_hat = (k * jnp.exp(gtot - gcum)).astype(BF16)
        k_bd = jnp.where(bd, jnp.concatenate([k_t] * nh, axis=0), 0.0).astype(BF16)
        scores[d, ci] = _dot_nt(q_t, k_bd)
        qt_scr[d, rs, :] = q_t
        kv_scr[d, ci] = jnp.where(bd, _dot_tn(v_ref[0, rs, :], k_hat), 0.0)
        dec_scr[d, ci] = jnp.broadcast_to(jnp.exp(gtot), dec_scr.shape[2:])
    for d, ci in work:
        v = ins[d][2][0, rows[ci], :]
        a = jnp.where(tri_cat[d], scores[d, ci], 0.0).astype(BF16)
        v_bd = jnp.where(bd, jnp.concatenate([v] * nh, axis=0), jnp.zeros((), v.dtype))
        o_refs[d][0, rows[ci], :] = _dot(a, v_bd)

    scan = [(d, j, j if d == 0 else nc - 1 - j) for j in range(nc) for d in range(2)]
    for d, j, ci in scan:
        s_in = s_scr[d]
        sin_scr[d, j] = s_in.astype(BF16)
        s_scr[d] = s_in * dec_scr[d, ci, 0:1, :] + kv_scr[d, ci]
    for d, j, ci in scan:
        o_refs[d][0, rows[ci], :] += _dot_nt(qt_scr[d, rows[ci], :], sin_scr[d, j])

    if want_state:
        @pl.when(i == nblk - 1)
        def _():
            for d in range(2):
                s_t = s_scr[d].T
                for h in range(nh):
                    sfin_ref[d, h] = s_t[h * c:(h + 1) * c, h * c:(h + 1) * c]


def _gla(q, k, v, gdec, s0, *, tb, want_state):
    b, t, w = q.shape
    nblk = t // tb
    nc = tb // GLA_CHUNK
    blk = [lambda i: i, lambda i: nblk - 1 - i]
    in_specs, args = [], []
    for d in range(2):
        tok = pl.BlockSpec((1, tb, w), lambda bi, i, d=d: (bi, blk[d](i), 0))
        in_specs += [tok, tok, tok, pl.BlockSpec((1, tb, w), lambda bi, i, d=d: (bi, blk[d](i), d))]
        args += [q, k, v, gdec]
    has_state = s0 is not None
    nh = w // GLA_CHUNK
    state_block = (None, 2, nh, GLA_CHUNK, GLA_CHUNK)
    if has_state:
        states, layer = s0
        in_specs.append(pl.BlockSpec((None,) + state_block, lambda bi, i: (bi, layer, 0, 0, 0, 0)))
        args.append(states)
    out_shape = [jax.ShapeDtypeStruct((b, t, w), F32)] * 2
    out_specs = [pl.BlockSpec((1, tb, w), lambda bi, i, d=d: (bi, blk[d](i), 0)) for d in range(2)]
    if want_state:
        out_shape.append(jax.ShapeDtypeStruct((b, 2, nh, GLA_CHUNK, GLA_CHUNK), F32))
        out_specs.append(pl.BlockSpec(state_block, lambda bi, i: (bi, 0, 0, 0, 0)))
    kern = functools.partial(_gla_kernel, nc=nc, has_state=has_state, want_state=want_state)
    return pl.pallas_call(
        kern,
        grid=(b, nblk),
        in_specs=in_specs,
        out_specs=out_specs,
        out_shape=out_shape,
        scratch_shapes=[pltpu.VMEM((2, w, w), F32), pltpu.VMEM((2, tb, w), BF16),
                        pltpu.VMEM((2, nc, w, w), F32), pltpu.VMEM((2, nc, SUBLANES, w), F32),
                        pltpu.VMEM((2, nc, w, w), BF16)],
        compiler_params=_cparams(("parallel", "arbitrary")),
        name="gla",
    )(*args)


def _attn_kernel(*refs, has_cache, lam_init, n_slots):
    it = iter(refs)
    q_ref, k_ref, vt_ref = next(it), next(it), next(it)
    kc_ref, vtc_ref = (next(it), next(it)) if has_cache else (None, None)
    lam_ref, g_ref, o_ref = next(it), next(it), next(it)
    s_scr = [next(it) for _ in range(n_slots)]
    e_scr = [next(it) for _ in range(n_slots)]
    acc_scr = next(it)
    p = lam_ref[...]
    lam = (jnp.exp(jnp.sum(p[0:1] * p[1:2], axis=-1, keepdims=True))
           - jnp.exp(jnp.sum(p[2:3] * p[3:4], axis=-1, keepdims=True)) + lam_init)
    for hh in range(k_ref.shape[1]):
        lanes = pl.ds(hh * LANES, LANES)
        _attn_head(q_ref.at[0, :, lanes], k_ref.at[0, hh], vt_ref.at[0, hh],
                   kc_ref.at[0, hh] if has_cache else None, vtc_ref.at[0, hh] if has_cache else None,
                   lam, g_ref, o_ref.at[0, :, lanes], s_scr, e_scr, acc_scr, lam_init)


def _attn_head(q_ref, k_ref, vt_ref, kc_ref, vtc_ref, lam, g_ref, o_ref, s_scr, e_scr, acc_scr, lam_init):
    has_cache = kc_ref is not None
    n_slots = len(s_scr)
    q = q_ref[...]
    tq = q.shape[0]
    lane = lax.broadcasted_iota(jnp.int32, q.shape, 1)
    zero = jnp.zeros((), q.dtype)
    qs = jnp.concatenate([jnp.where(lane < DH_DIFF, q, zero), jnp.where(lane >= DH_DIFF, q, zero)], axis=0)

    tk = vt_ref.shape[2]
    nq = 2 * tq
    acc_scr[...] = jnp.zeros_like(acc_scr)
    group = s_scr[0].shape[0] // tk
    tiles = []
    for kr, vr in [(k_ref, vt_ref)] + ([(kc_ref, vtc_ref)] if has_cache else []):
        for first in range(0, vr.shape[0], group):
            tiles.append((kr, vr, first, min(group, vr.shape[0] - first)))
    n = len(tiles)

    def scores(j, slot):
        kr, _, first, cnt = tiles[j]
        rows = cnt * tk
        s_scr[slot][0:rows, :] = _dot_nt(kr[first * tk:first * tk + rows, :].astype(BF16), qs)
        strips = [s_scr[slot][r:r + ATTN_STRIP, :] for r in range(0, rows, ATTN_STRIP)]
        return jnp.max(functools.reduce(jnp.maximum, strips), axis=0, keepdims=True)

    def softmax(j, slot, mt, m):
        m_new = jnp.maximum(m, mt)
        alpha = jnp.exp2(m - m_new)
        for r in range(0, tiles[j][3] * tk, ATTN_STRIP):
            e_scr[slot][r:r + ATTN_STRIP, :] = jnp.exp2(s_scr[slot][r:r + ATTN_STRIP, :] - m_new).astype(BF16)
        return m_new, alpha

    def weighted(j, slot, alpha):
        _, vr, first, cnt = tiles[j]
        pv = functools.reduce(lambda a, b: a + b, [_dot(vr[first + p], e_scr[slot][p * tk:(p + 1) * tk, :])
                                                   for p in range(cnt)])
        acc_scr[...] = acc_scr[...] * alpha + pv

    slot = lambda j: j % n_slots

    lag = min(ATTN_LAG, n_slots - 1)
    ahead = min(ATTN_AHEAD, n_slots - 1)
    m = jnp.full((1, nq), -1e30, F32)
    mts = {j: scores(j, slot(j)) for j in range(min(ahead, n))}
    alphas = {}
    for j in range(n + lag):
        if j + ahead < n:
            mts[j + ahead] = scores(j + ahead, slot(j + ahead))
        if j < n:
            m, alphas[j] = softmax(j, slot(j), mts.pop(j), m)
        if 0 <= j - lag < n:
            weighted(j - lag, slot(j - lag), alphas.pop(j - lag))
    dv = o_ref.shape[1]
    on = acc_scr[0:dv, :] / acc_scr[dv:dv + 1, :]
    o = on[:, :tq] - lam * on[:, tq:]
    ms = jnp.mean(o * o, axis=0, keepdims=True)
    y = o * lax.rsqrt(ms + LN_EPS) * g_ref[...] * (1.0 - lam_init)
    o_ref[...] = y.T.astype(o_ref.dtype)


def _attn(dq, dk, dvt, ck, cvt, lam_p, g_col, *, lam_init, tq):
    b, t, _ = dq.shape
    dv, tk = dvt.shape[3:]
    has_cache = ck is not None
    group = max(1, min(ATTN_KEY_TILE // tk, dvt.shape[2]))
    n_tiles = -(-dvt.shape[2] // group) + (-(-cvt.shape[2] // group) if has_cache else 0)
    n_slots = min(n_tiles, ATTN_SLOTS)
    rows = group * tk
    hps = H_DIFF if n_tiles == 1 else 1
    qspec = pl.BlockSpec((1, tq, hps * LANES), lambda bi, h, qi: (bi, qi, h))
    kspec = lambda a: pl.BlockSpec((1, hps) + a.shape[2:], lambda bi, h, qi: (bi, h, 0, 0))
    vspec = lambda a: pl.BlockSpec((1, hps) + a.shape[2:], lambda bi, h, qi: (bi, h, 0, 0, 0))
    kv_args = [dk, dvt] + ([ck, cvt] if has_cache else [])
    kv_specs = [kspec(dk), vspec(dvt)] + ([kspec(ck), vspec(cvt)] if has_cache else [])
    kern = functools.partial(_attn_kernel, has_cache=has_cache, lam_init=lam_init, n_slots=n_slots)
    return pl.pallas_call(
        kern,
        grid=(b, H_DIFF // hps, t // tq),
        in_specs=[qspec] + kv_specs + [_const_spec(lam_p.shape), _const_spec(g_col.shape)],
        out_specs=qspec,
        out_shape=jax.ShapeDtypeStruct(dq.shape, BF16),
        scratch_shapes=([pltpu.VMEM((rows, 2 * tq), F32)] * n_slots + [pltpu.VMEM((rows, 2 * tq), BF16)] * n_slots
                        + [pltpu.VMEM((dv, 2 * tq), F32)]),
        compiler_params=_cparams(("parallel", "parallel", "parallel")),
        name="diff_attn",
    )(dq, *kv_args, lam_p, g_col)


def _mixout_kernel(x_ref, mod_ref, yc_ref, of_ref, ob_ref, gate_ref, yd_ref, gn_ref, w_ref,
                   lng_ref, lnb_ref, o_ref, *, alpha):
    g1 = mod_ref[0, 5:6, :]
    tm = x_ref.shape[1]
    width = of_ref.shape[2]
    c_conv = yc_ref.shape[2]
    rb = lax.broadcasted_iota(jnp.int32, (width, width), 0) // DK_GLA
    cb = lax.broadcasted_iota(jnp.int32, (width, width), 1) // DK_GLA
    ones_bd = jnp.where(rb == cb, 1.0, 0.0).astype(BF16)
    n_parts = MIXOUT_PARTS if tm % (MIXOUT_PARTS * LANES) == 0 else 1
    parts = [slice(i * tm // n_parts, (i + 1) * tm // n_parts) for i in range(n_parts)]
    sums = []
    for rs in parts:
        o = of_ref[0, rs, :] + ob_ref[0, rs, :]
        sq = o * o
        hi = sq.astype(BF16)
        lo = (sq - hi.astype(F32)).astype(BF16)
        sums.append((o, _dot(hi, ones_bd) + _dot(lo, ones_bd)))
    ys = []
    for rs, (o, ss) in zip(parts, sums):
        yg = (o * lax.rsqrt(ss * (1.0 / DK_GLA) + LN_EPS) * gn_ref[...] * _silu(gate_ref[0, rs, :])).astype(BF16)
        ys.append(_dot(yc_ref[0, rs, :], w_ref[0:c_conv, :])
                  + _dot(yg, w_ref[c_conv:c_conv + width, :])
                  + _dot(yd_ref[0, rs, :], w_ref[c_conv + width:, :]))
    for rs, y in zip(parts, ys):
        o_ref[0, rs, :] = _layer_norm(alpha * x_ref[0, rs, :] + g1 * y, lng_ref[...], lnb_ref[...])


def _mixout(x, mod, yconv, o_fwd, o_bwd, gate, ydiff, gn, w, lng, lnb, *, alpha, tm):
    b, t, d = x.shape
    mod_rows = mod.shape[0]
    mod_idx = (lambda bi, ti: (bi, 0, 0)) if mod_rows > 1 else (lambda bi, ti: (0, 0, 0))
    tok = lambda n: pl.BlockSpec((1, tm, n), lambda bi, ti: (bi, ti, 0))
    wg = o_fwd.shape[2]
    kern = functools.partial(_mixout_kernel, alpha=alpha)
    return pl.pallas_call(
        kern,
        grid=(b, t // tm),
        in_specs=[
            tok(d), pl.BlockSpec((1, N_MOD, d), mod_idx), tok(yconv.shape[2]),
            tok(wg), tok(wg), tok(wg), tok(ydiff.shape[2]),
            _const_spec(gn.shape), _const_spec(w.shape), _const_spec((1, d)), _const_spec((1, d)),
        ],
        out_specs=tok(d),
        out_shape=jax.ShapeDtypeStruct((b, t, d), F32),
        compiler_params=_cparams(("parallel", "parallel")),
        name="mixer_out",
    )(x, mod, yconv, o_fwd, o_bwd, gate, ydiff, gn, w, lng, lnb)


def _rope_tables(t):
    rows = t // GRID_W
    row = jnp.repeat(jnp.arange(rows, dtype=F32), GRID_W)
    col = jnp.tile(jnp.arange(GRID_W, dtype=F32), rows)
    seg = DH_DIFF // 2
    inv = ROPE_BASE ** (-jnp.arange(0, seg, 2, dtype=F32) / seg)
    a_r = row[:, None] * inv
    a_c = col[:, None] * inv
    ang = jnp.concatenate([a_r, a_r, a_c, a_c], axis=-1)
    ang = jnp.concatenate([ang, ang], axis=-1)
    sign = jnp.where((jnp.arange(LANES) % 32) < 16, -1.0, 1.0).astype(F32)
    return jnp.cos(ang), jnp.sin(ang) * sign


def _ffn_weights(w_in, w_out, ck):
    depth, ff, d = w_out.shape
    return w_in.astype(BF16), w_out.reshape(depth, ff // ck, ck, d).astype(BF16)


def _pick_tile(t, pref):
    return pref if t % pref == 0 else t


def kernel(x_prompt, x_sample, cache_diff_k, cache_diff_v, state_gla, c, c_ctx, w_ada, b_ada, w_ffn1_in,
           w_ffn1_out, w_ffn2_in, w_ffn2_out, w_in, conv_w, conv_b, conv_ln_g, conv_ln_b, gla_w_a2, gla_b_a,
           gla_norm_g, diff_lam, diff_norm_g, w_out, ln_g, ln_b):
    depth, d, _ = w_ada.shape
    alpha = (2 * depth) ** 0.25
    n_dec = c.shape[0]
    c_conv = conv_w.shape[2]
    w_gla = H_GLA * DK_GLA
    w_diff = H_DIFF * 2 * DH_DIFF
    in_conv, in_gla = 2 * c_conv, 4 * w_gla + 2 * GLA_RANK
    ff_chunk = 256

    rows = -(-(n_dec + 1) // SUBLANES) * SUBLANES
    cvec = jnp.zeros((rows, d), F32).at[:n_dec].set(c).at[n_dec].set(c_ctx)
    mod = _ada(cvec, w_ada, b_ada).reshape(depth, rows, N_MOD, d)

    cos, sin = _rope_tables(x_sample.shape[1])

    ffn1_w = _ffn_weights(w_ffn1_in, w_ffn1_out, ff_chunk)
    ffn2_w = _ffn_weights(w_ffn2_in, w_ffn2_out, ff_chunk)
    layers = []
    for l in range(depth):
        wi = w_in[l]
        w_lr = jnp.pad(wi[:, in_conv + 4 * w_gla:in_conv + in_gla], ((0, 0), (0, LANES - 2 * GLA_RANK)))
        w_all = jnp.concatenate([wi[:, :in_conv + 4 * w_gla], wi[:, in_conv + in_gla:], w_lr], axis=1).astype(BF16)
        w2 = jnp.zeros((LANES, 2 * w_gla), F32)
        w2 = w2.at[:GLA_RANK, :w_gla].set(gla_w_a2[l, 0]).at[GLA_RANK:2 * GLA_RANK, w_gla:].set(gla_w_a2[l, 1])
        layers.append(dict(
            ffn1=ffn1_w + (l,), ffn2=ffn2_w + (l,),
            w_all=w_all, w2=w2.astype(BF16), b2=gla_b_a[l].reshape(1, 2 * w_gla),
            conv_w=conv_w[l], conv_b=conv_b[l][None], conv_g=conv_ln_g[l][None], conv_beta=conv_ln_b[l][None],
            gn=jnp.tile(gla_norm_g[l], H_GLA)[None], lam=diff_lam[l], dg=diff_norm_g[l][:, None],
            w_out=w_out[l].astype(BF16),
            lng=[ln_g[l, i][None] for i in range(3)], lnb=[ln_b[l, i][None] for i in range(3)],
            lam_init=0.8 - 0.6 * math.exp(-0.3 * l),
        ))

    def run_layer(x, mod_l, P, ctx):
        t = x.shape[1]
        tm = _pick_tile(t, 512)
        bsz, _, d_model = x.shape
        shared_mod = mod_l.shape[0] == 1
        tf = _pick_tile(bsz * t if shared_mod else t, 1024)

        def ffn(x, which, row0, ln_idx):
            y = _ffn(x.reshape(bsz * t, d_model), mod_l, *P[which], P['lng'][ln_idx], P['lnb'][ln_idx],
                     row0=row0, alpha=alpha, tm=tf)
            return y.reshape(bsz, t, d_model)

        x = ffn(x, 'ffn1', 0, 0)
        rope = ctx is not None
        yglu, gq, gk, gv, gate, gdec, dq, dk, dvt, *dv = _mixin(
            x, mod_l, P['w_all'], P['w2'], P['b2'], cos if rope else None, sin if rope else None, tm=tm)
        yconv = _conv(yglu, P['conv_w'], P['conv_b'], P['conv_g'], P['conv_beta'], tc=_pick_tile(t, 256))
        gla_out = _gla(gq, gk, gv, gdec, ctx['s0'] if rope else None, tb=_pick_tile(t, 512), want_state=not rope)
        ydiff = _attn(dq, dk, dvt, ctx['k'] if rope else None, ctx['vt'] if rope else None, P['lam'], P['dg'],
                      lam_init=P['lam_init'], tq=_pick_tile(t, 512))
        x = _mixout(x, mod_l, yconv, gla_out[0], gla_out[1], gate, ydiff, P['gn'], P['w_out'], P['lng'][1], P['lnb'][1],
                    alpha=alpha, tm=tm)
        x = ffn(x, 'ffn2', 6, 2)
        return x, (dk, dv[0] if dv else None, gla_out[2] if not rope else None)

    xp = x_prompt
    new_k, new_v, new_s = [], [], []
    for l in range(depth):
        xp, (k_l, v_l, s_l) = run_layer(xp, mod[l, n_dec:n_dec + 1], layers[l], None)
        new_k.append(k_l)
        new_v.append(v_l)
        new_s.append(s_l)

    xs = x_sample
    for l in range(depth):
        past = cache_diff_v.shape[3]
        tile = _pick_tile(x_sample.shape[1], 512)
        assert past % tile == 0, "cached context length must be a whole number of key tiles"
        cvt = cache_diff_v[:, l].astype(BF16).reshape(n_dec, H_DIFF, past // tile, tile, LANES)
        cvt = jnp.concatenate([jnp.swapaxes(cvt, -1, -2), jnp.ones(cvt.shape[:3] + (ONES_ROWS, tile), BF16)], axis=3)
        ctx = dict(k=cache_diff_k[:, l].astype(BF16), vt=cvt, s0=(state_gla, l))
        xs, _ = run_layer(xs, mod[l, :n_dec], layers[l], ctx)

    return (xp, xs, jnp.stack(new_k, axis=1), jnp.stack(new_v, axis=1), jnp.stack(new_s, axis=1))
```

```python
import functools
import math

import jax
import jax.numpy as jnp
from jax import lax
from jax.experimental import pallas as pl
from jax.experimental.pallas import tpu as pltpu

F32 = jnp.float32
BF16 = jnp.bfloat16

GRID_W = 64
CONV_WIDTH = 31
H_GLA = 4
DK_GLA = 64
GLA_RANK = 16
GLA_TAU = 16.0
GLA_CHUNK = 64
H_DIFF = 4
DH_DIFF = 64
ROPE_BASE = 10000.0
LN_EPS = 1e-5
N_MOD = 9

LANES = 128
SUBLANES = 8
V7X_VMEM_BYTES = 64 * 1024 * 1024
VMEM_LIMIT = V7X_VMEM_BYTES - 8 * 1024 * 1024

CONV_HALO = 16
CONV_ROWS = 32
ATTN_STRIP = 32
MIXIN_PARTS = 2
MIXOUT_PARTS = 2
ATTN_KEY_TILE = 512
ATTN_AHEAD = 1
ATTN_LAG = 1
ATTN_SLOTS = 3
ONES_ROWS = 16
LN_STRIP = 64


def _cparams(sem):
    return pltpu.CompilerParams(dimension_semantics=sem, vmem_limit_bytes=VMEM_LIMIT)


def _const_spec(shape):
    nd = len(shape)
    return pl.BlockSpec(shape, lambda *_: (0,) * nd, pipeline_mode=pl.Buffered(1))


def _dot(a, b):
    return jnp.dot(a, b, preferred_element_type=F32)


def _dot_nt(a, b):
    return lax.dot_general(a, b, (((1,), (1,)), ((), ())), preferred_element_type=F32)


def _dot_tn(a, b):
    return lax.dot_general(a, b, (((0,), (0,)), ((), ())), preferred_element_type=F32)


def _silu(x):
    return x * jax.nn.sigmoid(x)


def _layer_norm(y, g, b):
    mu = jnp.mean(y, axis=-1, keepdims=True)
    yc = y - mu
    var = jnp.mean(yc * yc, axis=-1, keepdims=True)
    return yc * lax.rsqrt(var + LN_EPS) * g + b


def _modulate(x, mod_ref, row0):
    sh = mod_ref[0, row0:row0 + 1, :]
    sc = mod_ref[0, row0 + 1:row0 + 2, :]
    g = mod_ref[0, row0 + 2:row0 + 3, :]
    return x * (1.0 + sc) + sh, g


def _ada_kernel(c_ref, w_ref, b_ref, o_ref):
    s = _silu(c_ref[...]).astype(BF16)
    o_ref[0] = _dot(s, w_ref[0].astype(BF16)) + b_ref[0]


def _ada(cvec, w_ada, b_ada):
    depth, d, n = w_ada.shape
    rows = cvec.shape[0]
    tn = d
    return pl.pallas_call(
        _ada_kernel,
        grid=(depth, n // tn),
        in_specs=[
            pl.BlockSpec((rows, d), lambda l, j: (0, 0)),
            pl.BlockSpec((1, d, tn), lambda l, j: (l, 0, j)),
            pl.BlockSpec((1, 1, tn), lambda l, j: (l, 0, j)),
        ],
        out_specs=pl.BlockSpec((1, rows, tn), lambda l, j: (l, 0, j)),
        out_shape=jax.ShapeDtypeStruct((depth, rows, n), F32),
        compiler_params=_cparams(("arbitrary", "arbitrary")),
        name="ada_mod",
    )(cvec, w_ada, b_ada.reshape(depth, 1, n))


def _layer_norm_rows(n_rows, make_y, o_ref, g, b):
    n = n_rows // LN_STRIP
    sl = lambda i: pl.ds(i * LN_STRIP if isinstance(i, int) else pl.multiple_of(i * LN_STRIP, LN_STRIP), LN_STRIP)

    def means(i):
        y = make_y(sl(i))
        o_ref[sl(i), :] = y
        return jnp.mean(y, axis=-1, keepdims=True)

    def rstd(i, mu):
        yc = o_ref[sl(i), :] - mu
        return lax.rsqrt(jnp.mean(yc * yc, axis=-1, keepdims=True) + LN_EPS)

    def normalise(i, mu, r):
        o_ref[sl(i), :] = (o_ref[sl(i), :] - mu) * r * g + b

    if n == 1:
        mu = means(0)
        normalise(0, mu, rstd(0, mu))
        return
    mu_a = means(0)
    mu_b = means(1)
    r_a = rstd(0, mu_a)

    def body(i, carry):
        mu_a, r_a, mu_b = carry
        r_b = rstd(i - 1, mu_b)
        mu_c = means(i)
        normalise(i - 2, mu_a, r_a)
        return mu_b, r_b, mu_c

    mu_a, r_a, mu_b = lax.fori_loop(2, n, body, (mu_a, r_a, mu_b))
    r_b = rstd(n - 1, mu_b)
    normalise(n - 2, mu_a, r_a)
    normalise(n - 1, mu_b, r_b)


def _ffn_kernel(x_ref, mod_ref, win_ref, wout_ref, lng_ref, lnb_ref, o_ref, h_scr, acc_ref, y_scr, *, row0, alpha):
    i = pl.program_id(0)
    n_tiles = pl.num_programs(0) - 1
    tm = x_ref.shape[0]
    n_chunks, ck, _ = wout_ref.shape
    lng = lng_ref[...]
    lnb = lnb_ref[...]
    n_strips = tm // LN_STRIP
    unroll = 2
    n_iters = (n_chunks - 2) // unroll
    tail = range(1 + n_iters * unroll, n_chunks)
    per_iter = -(-n_strips // (n_iters + 2))

    def norm_strip(idx):
        idx = jnp.minimum(idx, n_strips - 1)
        rs = pl.ds(pl.multiple_of(idx * LN_STRIP, LN_STRIP), LN_STRIP)
        o_ref[rs, :] = _layer_norm(y_scr[rs, :], lng, lnb)

    @pl.when(i == 0)
    def _():
        y_scr[...] = jnp.zeros_like(y_scr)

    @pl.when(i < n_tiles)
    def _():
        sh = mod_ref[0, row0:row0 + 1, :]
        sc1 = 1.0 + mod_ref[0, row0 + 1:row0 + 2, :]
        half_g = 0.5 * mod_ref[0, row0 + 2:row0 + 3, :]

        h_scr[...] = (x_ref[...] * sc1 + sh).astype(BF16)

        def chunk(j):
            cols = lambda start: pl.ds(pl.multiple_of(start, ck), ck)
            h = h_scr[...]
            a = _dot(h, win_ref[:, cols(j * ck)])
            b = _dot(h, win_ref[:, cols(n_chunks * ck + j * ck)])
            return _dot((_silu(a) * b).astype(BF16), wout_ref[j])

        acc_ref[...] = chunk(0)
        for s in range(per_iter):
            norm_strip(s)

        def body(k, carry):
            for u in range(unroll):
                acc_ref[...] += chunk(1 + k * unroll + u)
            for s in range(per_iter):
                norm_strip((k + 1) * per_iter + s)
            return carry

        lax.fori_loop(0, n_iters, body, 0)
        for s in range(per_iter):
            norm_strip((n_iters + 1) * per_iter + s)
        for j in tail[:-1]:
            acc_ref[...] += chunk(j)
        y_scr[...] = alpha * x_ref[...] + half_g * (acc_ref[...] + chunk(tail[-1]))

    @pl.when(i == n_tiles)
    def _():
        _layer_norm_rows(tm, lambda rs: y_scr[rs, :], o_ref, lng, lnb)


def _layer_spec(stacked, layer):
    nd = stacked.ndim - 1
    return pl.BlockSpec((None,) + stacked.shape[1:], lambda *_: (layer,) + (0,) * nd, pipeline_mode=pl.Buffered(1))


def _ffn(x, mod, win, wout, layer, lng, lnb, *, row0, alpha, tm):
    n_tok, d = x.shape
    n_tiles = n_tok // tm
    per_req = n_tok // mod.shape[0]
    assert n_tok % tm == 0 and per_req % tm == 0, "a token tile must not straddle two requests"
    cur = lambda ti: jnp.minimum(ti, n_tiles - 1)
    kern = functools.partial(_ffn_kernel, row0=row0, alpha=alpha)
    return pl.pallas_call(
        kern,
        grid=(n_tiles + 1,),
        in_specs=[
            pl.BlockSpec((tm, d), lambda ti: (cur(ti), 0)),
            pl.BlockSpec((1, N_MOD, d), lambda ti: ((cur(ti) * tm) // per_req, 0, 0)),
            _layer_spec(win, layer),
            _layer_spec(wout, layer),
            _const_spec((1, d)),
            _const_spec((1, d)),
        ],
        out_specs=pl.BlockSpec((tm, d), lambda ti: (jnp.maximum(ti - 1, 0), 0)),
        out_shape=jax.ShapeDtypeStruct((n_tok, d), F32),
        scratch_shapes=[pltpu.VMEM((tm, d), BF16), pltpu.VMEM((tm, d), F32), pltpu.VMEM((tm, d), F32)],
        compiler_params=_cparams(("arbitrary",)),
        name="ffn",
    )(x, mod, win, wout, lng, lnb)


def _log_sigmoid(x):
    return jnp.minimum(x, 0.0) - jnp.log(1.0 + jnp.exp(-jnp.abs(x)))


def _rope(x, cos, sin_signed, first_half):
    rot = jnp.where(first_half, pltpu.roll(x, LANES - 16, 1), pltpu.roll(x, 16, 1))
    return x * cos + rot * sin_signed


def _mixin_kernel(*refs, rope, c_conv, w_gla, w_diff):
    if rope:
        (x_ref, mod_ref, w_ref, w2_ref, b2_ref, cos_ref, sin_ref,
         yglu_ref, gq_ref, gk_ref, gv_ref, gate_ref, gdec_ref, dq_ref, dk_ref, dvt_ref) = refs
    else:
        (x_ref, mod_ref, w_ref, w2_ref, b2_ref,
         yglu_ref, gq_ref, gk_ref, gv_ref, gate_ref, gdec_ref, dq_ref, dk_ref, dvt_ref, dv_ref) = refs
    x = x_ref[0]
    h, _ = _modulate(x, mod_ref, 3)
    h = h.astype(BF16)
    o_gla = 2 * c_conv
    o_diff = o_gla + 4 * w_gla
    o_lr = o_diff + 3 * w_diff

    tm = x.shape[0]
    n_parts = MIXIN_PARTS if tm % (MIXIN_PARTS * LANES) == 0 else 1
    halves = [slice(i * tm // n_parts, (i + 1) * tm // n_parts) for i in range(n_parts)]
    zs, pres = [], []
    for rs in halves:
        hh = h[rs]
        zd = _dot(hh, w_ref[:, o_diff:o_lr])
        zl = _dot(hh, w_ref[:, o_lr:o_lr + LANES]).astype(BF16)
        zg = _dot(hh, w_ref[:, o_gla:o_diff])
        pres.append(_dot(zl, w2_ref[...]) + b2_ref[...])
        zs.append((_dot(hh, w_ref[:, 0:o_gla]), zg, zl, zd))

    for rs, (zc, zg, _, zd), pre in zip(halves, zs, pres):
        yglu_ref[0, rs, :] = zc[:, :c_conv] * jax.nn.sigmoid(zc[:, c_conv:])
        gq_ref[0, rs, :] = zg[:, 0:w_gla] * (DK_GLA ** -0.5)
        gk_ref[0, rs, :] = zg[:, w_gla:2 * w_gla]
        gv_ref[0, rs, :] = zg[:, 2 * w_gla:3 * w_gla].astype(BF16)
        gate_ref[0, rs, :] = zg[:, 3 * w_gla:4 * w_gla]
        gdec_ref[0, rs, :] = _log_sigmoid(pre) * (1.0 / GLA_TAU)
        if rope:
            cos = cos_ref[rs, :]
            sin = sin_ref[rs, :]
            lane = lax.broadcasted_iota(jnp.int32, cos.shape, 1)
            first_half = (lane % 32) < 16
        for hd in range(H_DIFF):
            sl = slice(hd * LANES, (hd + 1) * LANES)
            q = zd[:, sl]
            k = zd[:, w_diff + hd * LANES:w_diff + (hd + 1) * LANES]
            v = zd[:, 2 * w_diff + hd * LANES:2 * w_diff + (hd + 1) * LANES]
            if rope:
                q = _rope(q, cos, sin, first_half)
                k = _rope(k, cos, sin, first_half)
            dq_ref[0, rs, sl] = (q * (DH_DIFF ** -0.5 * math.log2(math.e))).astype(BF16)
            dk_ref[0, hd, rs, :] = k.astype(dk_ref.dtype)
            dvt_ref[0, hd, 0, 0:LANES, rs] = v.T.astype(BF16)
            if not rope:
                dv_ref[0, hd, rs, :] = v
    for hd in range(H_DIFF):
        dvt_ref[0, hd, 0, LANES:, :] = jnp.ones((ONES_ROWS, tm), BF16)


def _mixin(x, mod, w, w2, b2, cos, sin, *, tm):
    b, t, d = x.shape
    rope = cos is not None
    w_gla = H_GLA * DK_GLA
    w_diff = H_DIFF * 2 * DH_DIFF
    c_conv = (w.shape[1] - LANES - 4 * w_gla - 3 * w_diff) // 2
    kv_dtype = BF16 if rope else F32
    mod_rows = mod.shape[0]
    mod_idx = (lambda bi, ti: (bi, 0, 0)) if mod_rows > 1 else (lambda bi, ti: (0, 0, 0))
    tok = lambda n: pl.BlockSpec((1, tm, n), lambda bi, ti: (bi, ti, 0))
    heads = pl.BlockSpec((1, H_DIFF, tm, LANES), lambda bi, ti: (bi, 0, ti, 0))
    in_specs = [tok(d), pl.BlockSpec((1, N_MOD, d), mod_idx),
                _const_spec(w.shape), _const_spec(w2.shape), _const_spec(b2.shape)]
    args = [x, mod, w, w2, b2]
    if rope:
        in_specs += [pl.BlockSpec((tm, LANES), lambda bi, ti: (ti, 0))] * 2
        args += [cos, sin]
    sds = jax.ShapeDtypeStruct
    out_shape = [
        sds((b, t, c_conv), F32),
        sds((b, t, w_gla), F32),
        sds((b, t, w_gla), F32),
        sds((b, t, w_gla), BF16),
        sds((b, t, w_gla), F32),
        sds((b, t, 2 * w_gla), F32),
        sds((b, t, w_diff), BF16),
        sds((b, H_DIFF, t, LANES), kv_dtype),
        sds((b, H_DIFF, t // tm, LANES + ONES_ROWS, tm), BF16),
    ]
    out_specs = [tok(c_conv), tok(w_gla), tok(w_gla), tok(w_gla), tok(w_gla), tok(2 * w_gla),
                 tok(w_diff), heads,
                 pl.BlockSpec((1, H_DIFF, 1, LANES + ONES_ROWS, tm), lambda bi, ti: (bi, 0, ti, 0, 0))]
    if not rope:
        out_shape.append(sds((b, H_DIFF, t, LANES), F32))
        out_specs.append(heads)
    kern = functools.partial(_mixin_kernel, rope=rope, c_conv=c_conv, w_gla=w_gla, w_diff=w_diff)
    return pl.pallas_call(
        kern,
        grid=(b, t // tm),
        in_specs=in_specs,
        out_specs=out_specs,
        out_shape=out_shape,
        compiler_params=_cparams(("parallel", "parallel")),
        name="mixer_in",
    )(*args)


def _conv_kernel(prev_ref, cur_ref, next_ref, w_ref, b_ref, g_ref, beta_ref, o_ref, win_ref, *, tc, nt):
    i = pl.program_id(1)
    pad = CONV_WIDTH // 2
    win_ref[0:CONV_HALO] = jnp.where(i > 0, prev_ref[0], 0.0)
    win_ref[CONV_HALO:CONV_HALO + tc] = cur_ref[0]
    win_ref[CONV_HALO + tc:2 * CONV_HALO + tc] = jnp.where(i < nt - 1, next_ref[0], 0.0)
    w = w_ref[...]
    first = CONV_HALO - pad
    n_groups = -(-(first + CONV_WIDTH) // SUBLANES)
    for r in range(0, tc, CONV_ROWS):
        wide = win_ref[r:r + CONV_ROWS + n_groups * SUBLANES, :]
        acc = None
        for s in range(SUBLANES):
            z = None
            for a in range(n_groups):
                k = a * SUBLANES + s - first
                if 0 <= k < CONV_WIDTH:
                    term = wide[a * SUBLANES:a * SUBLANES + CONV_ROWS + SUBLANES, :] * w[k:k + 1, :]
                    z = term if z is None else z + term
            if z is not None:
                acc = z[s:s + CONV_ROWS, :] if acc is None else acc + z[s:s + CONV_ROWS, :]
        y = _layer_norm(acc + b_ref[...], g_ref[...], beta_ref[...])
        o_ref[0, r:r + CONV_ROWS, :] = _silu(y).astype(o_ref.dtype)


def _conv(yglu, w, bias, g, beta, *, tc):
    b, t, c = yglu.shape
    nt = t // tc
    hb = tc // CONV_HALO
    n_halo = t // CONV_HALO
    kern = functools.partial(_conv_kernel, tc=tc, nt=nt)
    return pl.pallas_call(
        kern,
        grid=(b, nt),
        in_specs=[
            pl.BlockSpec((1, CONV_HALO, c), lambda bi, i: (bi, jnp.maximum(i * hb - 1, 0), 0)),
            pl.BlockSpec((1, tc, c), lambda bi, i: (bi, i, 0)),
            pl.BlockSpec((1, CONV_HALO, c), lambda bi, i: (bi, jnp.minimum((i + 1) * hb, n_halo - 1), 0)),
            _const_spec(w.shape), _const_spec((1, c)), _const_spec((1, c)), _const_spec((1, c)),
        ],
        out_specs=pl.BlockSpec((1, tc, c), lambda bi, i: (bi, i, 0)),
        out_shape=jax.ShapeDtypeStruct((b, t, c), BF16),
        scratch_shapes=[pltpu.VMEM((tc + 2 * CONV_HALO, c), F32)],
        compiler_params=_cparams(("parallel", "parallel")),
        name="conv_module",
    )(yglu, yglu, yglu, w, bias, g, beta)


def _split3(x):
    hi = x.astype(BF16)
    r1 = x - hi.astype(F32)
    mid = r1.astype(BF16)
    lo = (r1 - mid.astype(F32)).astype(BF16)
    return hi, mid, lo


def _gla_kernel(*refs, nc, has_state, want_state):
    it = iter(refs)
    ins = [[next(it) for _ in range(4)] for _ in range(2)]
    s0_ref = next(it) if has_state else None
    o_refs = [next(it), next(it)]
    sfin_ref = next(it) if want_state else None
    s_scr, qt_scr, kv_scr, dec_scr, sin_scr = (next(it) for _ in range(5))

    i = pl.program_id(1)
    nblk = pl.num_programs(1)
    c = GLA_CHUNK
    width = s_scr.shape[1]
    nh = width // c

    @pl.when(i == 0)
    def _():
        if has_state:
            for d in range(2):
                bd_rows = [jnp.concatenate([s0_ref[d, h] if g == h else jnp.zeros((c, c), F32) for g in range(nh)],
                                           axis=1) for h in range(nh)]
                s_scr[d] = jnp.concatenate(bd_rows, axis=0).T
        else:
            s_scr[...] = jnp.zeros_like(s_scr)

    row = lax.broadcasted_iota(jnp.int32, (c, c), 0)
    col = lax.broadcasted_iota(jnp.int32, (c, c), 1)
    rowa = lax.broadcasted_iota(jnp.int32, (c, width), 0)
    cola = lax.broadcasted_iota(jnp.int32, (c, width), 1) % c
    tri_b = [jnp.where(row >= col, 1.0, 0.0).astype(BF16), jnp.where(row <= col, 1.0, 0.0).astype(BF16)]
    tri_cat = [rowa >= cola, rowa <= cola]
    rb = lax.broadcasted_iota(jnp.int32, (width, width), 0) // c
    cb = lax.broadcasted_iota(jnp.int32, (width, width), 1) // c
    bd = rb == cb
    rows = [slice(ci * c, (ci + 1) * c) for ci in range(nc)]
    work = [(d, ci) for ci in range(nc) for d in range(2)]

    gcums = {}
    for d, ci in work:
        ghi, gmid, glo = _split3(ins[d][3][0, rows[ci], :])
        gcums[d, ci] = _dot(tri_b[d], ghi) + _dot(tri_b[d], gmid) + _dot(tri_b[d], glo)
    scores = {}
    for d, ci in work:
        q_ref, k_ref, v_ref, _ = ins[d]
        rs = rows[ci]
        gcum = gcums[d, ci]
        gtot = gcum[c - 1:c, :] if d == 0 else gcum[0:1, :]
        k = k_ref[0, rs, :]
        q_t = (q_ref[0, rs, :] * jnp.exp(gcum)).astype(BF16)
        k_t = k * jnp.exp(-gcum)
        k_hat = (k * jnp.exp(gtot - gcum)).astype(BF16)
        k_bd = jnp.where(bd, jnp.concatenate([k_t] * nh, axis=0), 0.0).astype(BF16)
        scores[d, ci] = _dot_nt(q_t, k_bd)
        qt_scr[d, rs, :] = q_t
        kv_scr[d, ci] = jnp.where(bd, _dot_tn(v_ref[0, rs, :], k_hat), 0.0)
        dec_scr[d, ci] = jnp.broadcast_to(jnp.exp(gtot), dec_scr.shape[2:])
    for d, ci in work:
        v = ins[d][2][0, rows[ci], :]
        a = jnp.where(tri_cat[d], scores[d, ci], 0.0).astype(BF16)
        v_bd = jnp.where(bd, jnp.concatenate([v] * nh, axis=0), jnp.zeros((), v.dtype))
        o_refs[d][0, rows[ci], :] = _dot(a, v_bd)

    scan = [(d, j, j if d == 0 else nc - 1 - j) for j in range(nc) for d in range(2)]
    for d, j, ci in scan:
        s_in = s_scr[d]
        sin_scr[d, j] = s_in.astype(BF16)
        s_scr[d] = s_in * dec_scr[d, ci, 0:1, :] + kv_scr[d, ci]
    for d, j, ci in scan:
        o_refs[d][0, rows[ci], :] += _dot_nt(qt_scr[d, rows[ci], :], sin_scr[d, j])

    if want_state:
        @pl.when(i == nblk - 1)
        def _():
            for d in range(2):
                s_t = s_scr[d].T
                for h in range(nh):
                    sfin_ref[d, h] = s_t[h * c:(h + 1) * c, h * c:(h + 1) * c]


def _gla(q, k, v, gdec, s0, *, tb, want_state):
    b, t, w = q.shape
    nblk = t // tb
    nc = tb // GLA_CHUNK
    blk = [lambda i: i, lambda i: nblk - 1 - i]
    in_specs, args = [], []
    for d in range(2):
        tok = pl.BlockSpec((1, tb, w), lambda bi, i, d=d: (bi, blk[d](i), 0))
        in_specs += [tok, tok, tok, pl.BlockSpec((1, tb, w), lambda bi, i, d=d: (bi, blk[d](i), d))]
        args += [q, k, v, gdec]
    has_state = s0 is not None
    nh = w // GLA_CHUNK
    state_block = (None, 2, nh, GLA_CHUNK, GLA_CHUNK)
    if has_state:
        states, layer = s0
        in_specs.append(pl.BlockSpec((None,) + state_block, lambda bi, i: (bi, layer, 0, 0, 0, 0)))
        args.append(states)
    out_shape = [jax.ShapeDtypeStruct((b, t, w), F32)] * 2
    out_specs = [pl.BlockSpec((1, tb, w), lambda bi, i, d=d: (bi, blk[d](i), 0)) for d in range(2)]
    if want_state:
        out_shape.append(jax.ShapeDtypeStruct((b, 2, nh, GLA_CHUNK, GLA_CHUNK), F32))
        out_specs.append(pl.BlockSpec(state_block, lambda bi, i: (bi, 0, 0, 0, 0)))
    kern = functools.partial(_gla_kernel, nc=nc, has_state=has_state, want_state=want_state)
    return pl.pallas_call(
        kern,
        grid=(b, nblk),
        in_specs=in_specs,
        out_specs=out_specs,
        out_shape=out_shape,
        scratch_shapes=[pltpu.VMEM((2, w, w), F32), pltpu.VMEM((2, tb, w), BF16),
                        pltpu.VMEM((2, nc, w, w), F32), pltpu.VMEM((2, nc, SUBLANES, w), F32),
                        pltpu.VMEM((2, nc, w, w), BF16)],
        compiler_params=_cparams(("parallel", "arbitrary")),
        name="gla",
    )(*args)


def _attn_kernel(*refs, has_cache, lam_init, n_slots):
    it = iter(refs)
    q_ref, k_ref, vt_ref = next(it), next(it), next(it)
    kc_ref, vtc_ref = (next(it), next(it)) if has_cache else (None, None)
    lam_ref, g_ref, o_ref = next(it), next(it), next(it)
    s_scr = [next(it) for _ in range(n_slots)]
    e_scr = [next(it) for _ in range(n_slots)]
    acc_scr = next(it)
    p = lam_ref[...]
    lam = (jnp.exp(jnp.sum(p[0:1] * p[1:2], axis=-1, keepdims=True))
           - jnp.exp(jnp.sum(p[2:3] * p[3:4], axis=-1, keepdims=True)) + lam_init)
    for hh in range(k_ref.shape[1]):
        lanes = pl.ds(hh * LANES, LANES)
        _attn_head(q_ref.at[0, :, lanes], k_ref.at[0, hh], vt_ref.at[0, hh],
                   kc_ref.at[0, hh] if has_cache else None, vtc_ref.at[0, hh] if has_cache else None,
                   lam, g_ref, o_ref.at[0, :, lanes], s_scr, e_scr, acc_scr, lam_init)


def _attn_head(q_ref, k_ref, vt_ref, kc_ref, vtc_ref, lam, g_ref, o_ref, s_scr, e_scr, acc_scr, lam_init):
    has_cache = kc_ref is not None
    n_slots = len(s_scr)
    q = q_ref[...]
    tq = q.shape[0]
    lane = lax.broadcasted_iota(jnp.int32, q.shape, 1)
    zero = jnp.zeros((), q.dtype)
    qs = jnp.concatenate([jnp.where(lane < DH_DIFF, q, zero), jnp.where(lane >= DH_DIFF, q, zero)], axis=0)

    tk = vt_ref.shape[2]
    nq = 2 * tq
    acc_scr[...] = jnp.zeros_like(acc_scr)
    group = s_scr[0].shape[0] // tk
    tiles = []
    for kr, vr in [(k_ref, vt_ref)] + ([(kc_ref, vtc_ref)] if has_cache else []):
        for first in range(0, vr.shape[0], group):
            tiles.append((kr, vr, first, min(group, vr.shape[0] - first)))
    n = len(tiles)

    def scores(j, slot):
        kr, _, first, cnt = tiles[j]
        rows = cnt * tk
        s_scr[slot][0:rows, :] = _dot_nt(kr[first * tk:first * tk + rows, :].astype(BF16), qs)
        strips = [s_scr[slot][r:r + ATTN_STRIP, :] for r in range(0, rows, ATTN_STRIP)]
        return jnp.max(functools.reduce(jnp.maximum, strips), axis=0, keepdims=True)

    def softmax(j, slot, mt, m):
        m_new = jnp.maximum(m, mt)
        alpha = jnp.exp2(m - m_new)
        for r in range(0, tiles[j][3] * tk, ATTN_STRIP):
            e_scr[slot][r:r + ATTN_STRIP, :] = jnp.exp2(s_scr[slot][r:r + ATTN_STRIP, :] - m_new).astype(BF16)
        return m_new, alpha

    def weighted(j, slot, alpha):
        _, vr, first, cnt = tiles[j]
        pv = functools.reduce(lambda a, b: a + b, [_dot(vr[first + p], e_scr[slot][p * tk:(p + 1) * tk, :])
                                                   for p in range(cnt)])
        acc_scr[...] = acc_scr[...] * alpha + pv

    slot = lambda j: j % n_slots

    lag = min(ATTN_LAG, n_slots - 1)
    ahead = min(ATTN_AHEAD, n_slots - 1)
    m = jnp.full((1, nq), -1e30, F32)
    mts = {j: scores(j, slot(j)) for j in range(min(ahead, n))}
    alphas = {}
    for j in range(n + lag):
        if j + ahead < n:
            mts[j + ahead] = scores(j + ahead, slot(j + ahead))
        if j < n:
            m, alphas[j] = softmax(j, slot(j), mts.pop(j), m)
        if 0 <= j - lag < n:
            weighted(j - lag, slot(j - lag), alphas.pop(j - lag))
    dv = o_ref.shape[1]
    on = acc_scr[0:dv, :] / acc_scr[dv:dv + 1, :]
    o = on[:, :tq] - lam * on[:, tq:]
    ms = jnp.mean(o * o, axis=0, keepdims=True)
    y = o * lax.rsqrt(ms + LN_EPS) * g_ref[...] * (1.0 - lam_init)
    o_ref[...] = y.T.astype(o_ref.dtype)


def _attn(dq, dk, dvt, ck, cvt, lam_p, g_col, *, lam_init, tq):
    b, t, _ = dq.shape
    dv, tk = dvt.shape[3:]
    has_cache = ck is not None
    group = max(1, min(ATTN_KEY_TILE // tk, dvt.shape[2]))
    n_tiles = -(-dvt.shape[2] // group) + (-(-cvt.shape[2] // group) if has_cache else 0)
    n_slots = min(n_tiles, ATTN_SLOTS)
    rows = group * tk
    hps = H_DIFF if n_tiles == 1 else 1
    qspec = pl.BlockSpec((1, tq, hps * LANES), lambda bi, h, qi: (bi, qi, h))
    kspec = lambda a: pl.BlockSpec((1, hps) + a.shape[2:], lambda bi, h, qi: (bi, h, 0, 0))
    vspec = lambda a: pl.BlockSpec((1, hps) + a.shape[2:], lambda bi, h, qi: (bi, h, 0, 0, 0))
    kv_args = [dk, dvt] + ([ck, cvt] if has_cache else [])
    kv_specs = [kspec(dk), vspec(dvt)] + ([kspec(ck), vspec(cvt)] if has_cache else [])
    kern = functools.partial(_attn_kernel, has_cache=has_cache, lam_init=lam_init, n_slots=n_slots)
    return pl.pallas_call(
        kern,
        grid=(b, H_DIFF // hps, t // tq),
        in_specs=[qspec] + kv_specs + [_const_spec(lam_p.shape), _const_spec(g_col.shape)],
        out_specs=qspec,
        out_shape=jax.ShapeDtypeStruct(dq.shape, BF16),
        scratch_shapes=([pltpu.VMEM((rows, 2 * tq), F32)] * n_slots + [pltpu.VMEM((rows, 2 * tq), BF16)] * n_slots
                        + [pltpu.VMEM((dv, 2 * tq), F32)]),
        compiler_params=_cparams(("parallel", "parallel", "parallel")),
        name="diff_attn",
    )(dq, *kv_args, lam_p, g_col)


def _mixout_kernel(x_ref, mod_ref, yc_ref, of_ref, ob_ref, gate_ref, yd_ref, gn_ref, w_ref,
                   lng_ref, lnb_ref, o_ref, *, alpha):
    g1 = mod_ref[0, 5:6, :]
    tm = x_ref.shape[1]
    width = of_ref.shape[2]
    c_conv = yc_ref.shape[2]
    rb = lax.broadcasted_iota(jnp.int32, (width, width), 0) // DK_GLA
    cb = lax.broadcasted_iota(jnp.int32, (width, width), 1) // DK_GLA
    ones_bd = jnp.where(rb == cb, 1.0, 0.0).astype(BF16)
    n_parts = MIXOUT_PARTS if tm % (MIXOUT_PARTS * LANES) == 0 else 1
    parts = [slice(i * tm // n_parts, (i + 1) * tm // n_parts) for i in range(n_parts)]
    sums = []
    for rs in parts:
        o = of_ref[0, rs, :] + ob_ref[0, rs, :]
        sq = o * o
        hi = sq.astype(BF16)
        lo = (sq - hi.astype(F32)).astype(BF16)
        sums.append((o, _dot(hi, ones_bd) + _dot(lo, ones_bd)))
    ys = []
    for rs, (o, ss) in zip(parts, sums):
        yg = (o * lax.rsqrt(ss * (1.0 / DK_GLA) + LN_EPS) * gn_ref[...] * _silu(gate_ref[0, rs, :])).astype(BF16)
        ys.append(_dot(yc_ref[0, rs, :], w_ref[0:c_conv, :])
                  + _dot(yg, w_ref[c_conv:c_conv + width, :])
                  + _dot(yd_ref[0, rs, :], w_ref[c_conv + width:, :]))
    for rs, y in zip(parts, ys):
        o_ref[0, rs, :] = _layer_norm(alpha * x_ref[0, rs, :] + g1 * y, lng_ref[...], lnb_ref[...])


def _mixout(x, mod, yconv, o_fwd, o_bwd, gate, ydiff, gn, w, lng, lnb, *, alpha, tm):
    b, t, d = x.shape
    mod_rows = mod.shape[0]
    mod_idx = (lambda bi, ti: (bi, 0, 0)) if mod_rows > 1 else (lambda bi, ti: (0, 0, 0))
    tok = lambda n: pl.BlockSpec((1, tm, n), lambda bi, ti: (bi, ti, 0))
    wg = o_fwd.shape[2]
    kern = functools.partial(_mixout_kernel, alpha=alpha)
    return pl.pallas_call(
        kern,
        grid=(b, t // tm),
        in_specs=[
            tok(d), pl.BlockSpec((1, N_MOD, d), mod_idx), tok(yconv.shape[2]),
            tok(wg), tok(wg), tok(wg), tok(ydiff.shape[2]),
            _const_spec(gn.shape), _const_spec(w.shape), _const_spec((1, d)), _const_spec((1, d)),
        ],
        out_specs=tok(d),
        out_shape=jax.ShapeDtypeStruct((b, t, d), F32),
        compiler_params=_cparams(("parallel", "parallel")),
        name="mixer_out",
    )(x, mod, yconv, o_fwd, o_bwd, gate, ydiff, gn, w, lng, lnb)


def _rope_tables(t):
    rows = t // GRID_W
    row = jnp.repeat(jnp.arange(rows, dtype=F32), GRID_W)
    col = jnp.tile(jnp.arange(GRID_W, dtype=F32), rows)
    seg = DH_DIFF // 2
    inv = ROPE_BASE ** (-jnp.arange(0, seg, 2, dtype=F32) / seg)
    a_r = row[:, None] * inv
    a_c = col[:, None] * inv
    ang = jnp.concatenate([a_r, a_r, a_c, a_c], axis=-1)
    ang = jnp.concatenate([ang, ang], axis=-1)
    sign = jnp.where((jnp.arange(LANES) % 32) < 16, -1.0, 1.0).astype(F32)
    return jnp.cos(ang), jnp.sin(ang) * sign


def _ffn_weights(w_in, w_out, ck):
    depth, ff, d = w_out.shape
    return w_in.astype(BF16), w_out.reshape(depth, ff // ck, ck, d).astype(BF16)


def _pick_tile(t, pref):
    return pref if t % pref == 0 else t


def kernel(x_prompt, x_sample, cache_diff_k, cache_diff_v, state_gla, c, c_ctx, w_ada, b_ada, w_ffn1_in,
           w_ffn1_out, w_ffn2_in, w_ffn2_out, w_in, conv_w, conv_b, conv_ln_g, conv_ln_b, gla_w_a2, gla_b_a,
           gla_norm_g, diff_lam, diff_norm_g, w_out, ln_g, ln_b):
    depth, d, _ = w_ada.shape
    alpha = (2 * depth) ** 0.25
    n_dec = c.shape[0]
    c_conv = conv_w.shape[2]
    w_gla = H_GLA * DK_GLA
    w_diff = H_DIFF * 2 * DH_DIFF
    in_conv, in_gla = 2 * c_conv, 4 * w_gla + 2 * GLA_RANK
    ff_chunk = 256

    rows = -(-(n_dec + 1) // SUBLANES) * SUBLANES
    cvec = jnp.zeros((rows, d), F32).at[:n_dec].set(c).at[n_dec].set(c_ctx)
    mod = _ada(cvec, w_ada, b_ada).reshape(depth, rows, N_MOD, d)

    cos, sin = _rope_tables(x_sample.shape[1])

    ffn1_w = _ffn_weights(w_ffn1_in, w_ffn1_out, ff_chunk)
    ffn2_w = _ffn_weights(w_ffn2_in, w_ffn2_out, ff_chunk)
    layers = []
    for l in range(depth):
        wi = w_in[l]
        w_lr = jnp.pad(wi[:, in_conv + 4 * w_gla:in_conv + in_gla], ((0, 0), (0, LANES - 2 * GLA_RANK)))
        w_all = jnp.concatenate([wi[:, :in_conv + 4 * w_gla], wi[:, in_conv + in_gla:], w_lr], axis=1).astype(BF16)
        w2 = jnp.zeros((LANES, 2 * w_gla), F32)
        w2 = w2.at[:GLA_RANK, :w_gla].set(gla_w_a2[l, 0]).at[GLA_RANK:2 * GLA_RANK, w_gla:].set(gla_w_a2[l, 1])
        layers.append(dict(
            ffn1=ffn1_w + (l,), ffn2=ffn2_w + (l,),
            w_all=w_all, w2=w2.astype(BF16), b2=gla_b_a[l].reshape(1, 2 * w_gla),
            conv_w=conv_w[l], conv_b=conv_b[l][None], conv_g=conv_ln_g[l][None], conv_beta=conv_ln_b[l][None],
            gn=jnp.tile(gla_norm_g[l], H_GLA)[None], lam=diff_lam[l], dg=diff_norm_g[l][:, None],
            w_out=w_out[l].astype(BF16),
            lng=[ln_g[l, i][None] for i in range(3)], lnb=[ln_b[l, i][None] for i in range(3)],
            lam_init=0.8 - 0.6 * math.exp(-0.3 * l),
        ))

    def run_layer(x, mod_l, P, ctx):
        t = x.shape[1]
        tm = _pick_tile(t, 512)
        bsz, _, d_model = x.shape
        shared_mod = mod_l.shape[0] == 1
        tf = _pick_tile(bsz * t if shared_mod else t, 1024)

        def ffn(x, which, row0, ln_idx):
            y = _ffn(x.reshape(bsz * t, d_model), mod_l, *P[which], P['lng'][ln_idx], P['lnb'][ln_idx],
                     row0=row0, alpha=alpha, tm=tf)
            return y.reshape(bsz, t, d_model)

        x = ffn(x, 'ffn1', 0, 0)
        rope = ctx is not None
        yglu, gq, gk, gv, gate, gdec, dq, dk, dvt, *dv = _mixin(
            x, mod_l, P['w_all'], P['w2'], P['b2'], cos if rope else None, sin if rope else None, tm=tm)
        yconv = _conv(yglu, P['conv_w'], P['conv_b'], P['conv_g'], P['conv_beta'], tc=_pick_tile(t, 256))
        gla_out = _gla(gq, gk, gv, gdec, ctx['s0'] if rope else None, tb=_pick_tile(t, 512), want_state=not rope)
        ydiff = _attn(dq, dk, dvt, ctx['k'] if rope else None, ctx['vt'] if rope else None, P['lam'], P['dg'],
                      lam_init=P['lam_init'], tq=_pick_tile(t, 512))
        x = _mixout(x, mod_l, yconv, gla_out[0], gla_out[1], gate, ydiff, P['gn'], P['w_out'], P['lng'][1], P['lnb'][1],
                    alpha=alpha, tm=tm)
        x = ffn(x, 'ffn2', 6, 2)
        return x, (dk, dv[0] if dv else None, gla_out[2] if not rope else None)

    xp = x_prompt
    new_k, new_v, new_s = [], [], []
    for l in range(depth):
        xp, (k_l, v_l, s_l) = run_layer(xp, mod[l, n_dec:n_dec + 1], layers[l], None)
        new_k.append(k_l)
        new_v.append(v_l)
        new_s.append(s_l)

    xs = x_sample
    for l in range(depth):
        past = cache_diff_v.shape[3]
        tile = _pick_tile(x_sample.shape[1], 512)
        assert past % tile == 0, "cached context length must be a whole number of key tiles"
        cvt = cache_diff_v[:, l].astype(BF16).reshape(n_dec, H_DIFF, past // tile, tile, LANES)
        cvt = jnp.concatenate([jnp.swapaxes(cvt, -1, -2), jnp.ones(cvt.shape[:3] + (ONES_ROWS, tile), BF16)], axis=3)
        ctx = dict(k=cache_diff_k[:, l].astype(BF16), vt=cvt, s0=(state_gla, l))
        xs, _ = run_layer(xs, mod[l, :n_dec], layers[l], ctx)

    return (xp, xs, jnp.stack(new_k, axis=1), jnp.stack(new_v, axis=1), jnp.stack(new_s, axis=1))
```
